```python
import math
import jax, jax.numpy as jnp
from jax import lax
import numpy as np

D_MODEL = 1024
BATCH = 8
SEQ = 4096
DEPTH = 2

HEAD_DIM = 64
QBLK = 128
GRID_W = 64
EPS = 1e-6
A_HEADS = 8
A_KV = 2
A_WIN = 128
B_HEADS = 4
B_VDIM = 2 * HEAD_DIM
C_HEADS = 8
C_KV = 2
ROPE_THETA = 10000.0
D_PAIRS = ((128, 1), (512, 4), (2048, 16))
D_GROUPS = 3
D_HEADS_PER_GROUP = 8
D_BLK = 64
N_BRANCH = 4
BRANCH_W = 512
REL_BUCKETS = 32
REL_MAX_DIST = 1024
REL_HEADS = A_HEADS + B_HEADS + D_GROUPS * D_HEADS_PER_GROUP
IN_COLS = (
    ('a_q', A_HEADS * HEAD_DIM), ('a_k', A_KV * HEAD_DIM), ('a_v', A_KV * HEAD_DIM),
    ('b_q', B_HEADS * 2 * HEAD_DIM), ('b_k', B_HEADS * 2 * HEAD_DIM), ('b_v', B_HEADS * B_VDIM),
    ('c_q', C_HEADS * HEAD_DIM), ('c_k', C_KV * HEAD_DIM), ('c_v', C_KV * HEAD_DIM),
    ('d_q', D_GROUPS * D_HEADS_PER_GROUP * HEAD_DIM), ('d_k', D_GROUPS * D_HEADS_PER_GROUP * HEAD_DIM),
    ('d_v', D_GROUPS * D_HEADS_PER_GROUP * HEAD_DIM),
    ('gate', N_BRANCH * BRANCH_W),
    ('merge', N_BRANCH * D_MODEL),
)
IN_WIDTH = 13824

kernel_name = 'hybrid_gated_parallel_encoder'


def _rmsnorm(t, g):
    tf = t.astype(jnp.float32)
    y = tf * lax.rsqrt(jnp.mean(tf * tf, axis=-1, keepdims=True) + EPS)
    return (y * g.astype(jnp.float32)).astype(t.dtype)


def _project(h, w):
    out, off = {}, 0
    for name, width in IN_COLS:
        out[name] = h @ w[:, off:off + width]
        off += width
    return out


def _rel_bucket(rel):
    nb = REL_BUCKETS // 2
    exact = nb // 2
    n = jnp.abs(rel)
    large = exact + (jnp.log(jnp.maximum(n, exact).astype(jnp.float32) / exact)
                     / math.log(REL_MAX_DIST / exact) * (nb - exact)).astype(jnp.int32)
    large = jnp.minimum(large, nb - 1)
    return jnp.where(rel > 0, nb, 0) + jnp.where(n < exact, n, large)


def _band_rel(blk):
    return jnp.arange(3 * blk)[None, :] - blk - jnp.arange(blk)[:, None]


def _band(t, blk):
    bsz, L = t.shape[:2]
    nb = L // blk
    tp = jnp.pad(t, ((0, 0), (blk, blk)) + ((0, 0),) * (t.ndim - 2))
    tb = tp.reshape(bsz, nb + 2, blk, *t.shape[2:])
    return jnp.concatenate([tb[:, :-2], tb[:, 1:-1], tb[:, 2:]], axis=2)


def _band_logits(q, k, bias, blk, half_win, length):
    bsz, L = q.shape[:2]
    nb = L // blk
    qb = q.reshape(bsz, nb, blk, *q.shape[2:])
    s = jnp.einsum('bnqhgd,bnkhd->bnhgqk', qb, _band(k, blk)).astype(jnp.float32)
    s = s * (q.shape[-1] ** -0.5) + bias
    kidx = jnp.arange(nb)[:, None] * blk - blk + jnp.arange(3 * blk)[None, :]
    valid = (jnp.abs(_band_rel(blk)) <= half_win)[None] & ((kidx >= 0) & (kidx < length))[:, None, :]
    return jnp.where(valid[None, :, None, None], s, -jnp.inf)


def _band_values(p, v, blk):
    o = jnp.einsum('bnhgqk,bnkhd->bnqhgd', p, _band(v, blk))
    return o.reshape(o.shape[0], o.shape[1] * o.shape[2], *o.shape[3:])


def _window_sink_attention(q, k, v, bias, sink):
    s = _band_logits(q, k, bias, QBLK, A_WIN, q.shape[1])
    sk = sink.astype(jnp.float32)[:, :, None]
    m = jnp.maximum(jnp.max(s, axis=-1), sk)
    e = jnp.exp(s - m[..., None])
    den = jnp.sum(e, axis=-1) + jnp.exp(sk - m)
    o = _band_values((e / den[..., None]).astype(v.dtype), v, QBLK)
    return o.reshape(q.shape[0], q.shape[1], -1)


def _diff_attention(q, k, v, tbl, lam, lam_init, sub_gain):
    bsz, seq = q.shape[:2]
    nb = seq // QBLK
    scale = q.shape[-1] ** -0.5
    qb = jnp.moveaxis(q.reshape(bsz, nb, QBLK, *q.shape[2:]), 1, 0)
    kpos = jnp.arange(seq)

    def block(args):
        qi, i = args
        s = jnp.einsum('bqhcd,bkhcd->bhcqk', qi, k).astype(jnp.float32) * scale
        rel = kpos[None, :] - (i * QBLK + jnp.arange(QBLK))[:, None]
        bias = jnp.moveaxis(tbl[_rel_bucket(rel)], -1, 0)
        p = jax.nn.softmax(s + bias[None, :, None], axis=-1)
        w = (p[:, :, 0] - lam * p[:, :, 1]).astype(v.dtype)
        return jnp.einsum('bhqk,bkhe->bqhe', w, v)

    o = lax.map(block, (qb, jnp.arange(nb)))
    o = jnp.moveaxis(o, 0, 1).reshape(bsz, seq, *v.shape[2:])
    o = _rmsnorm(o, sub_gain) * (1.0 - lam_init)
    return o.reshape(bsz, seq, -1)


def _axial_rope(t, row, col):
    half = HEAD_DIM // 2
    nf = half // 2
    freqs = ROPE_THETA ** (-jnp.arange(nf, dtype=jnp.float32) / nf)
    ang = jnp.concatenate([row[:, None] * freqs, col[:, None] * freqs], axis=-1)
    shape = (1, t.shape[1]) + (1,) * (t.ndim - 3) + (half,)
    cos, sin = jnp.cos(ang).reshape(shape), jnp.sin(ang).reshape(shape)
    tf = t.astype(jnp.float32)
    t1, t2 = tf[..., :half], tf[..., half:]
    return jnp.concatenate([t1 * cos - t2 * sin, t2 * cos + t1 * sin], axis=-1).astype(t.dtype)


def _dense_gqa(q, k, v):
    bsz, seq = q.shape[:2]
    nb = seq // QBLK
    scale = q.shape[-1] ** -0.5
    qb = jnp.moveaxis(q.reshape(bsz, nb, QBLK, *q.shape[2:]), 1, 0)

    def block(qi):
        s = jnp.einsum('bqhgd,bkhd->bhgqk', qi, k).astype(jnp.float32) * scale
        p = jax.nn.softmax(s, axis=-1).astype(v.dtype)
        return jnp.einsum('bhgqk,bkhd->bqhgd', p, v)

    o = lax.map(block, qb)
    return jnp.moveaxis(o, 0, 1).reshape(bsz, seq, -1)


def _dilated_attention(q, k, v, bias_d):
    bsz, seq = q.shape[:2]
    hg, d = q.shape[3], q.shape[4]
    outs, lses = [], []
    for g, (win, r) in enumerate(D_PAIRS):
        half_m = win // (2 * r)
        M = seq // r
        Mp = -(-M // D_BLK) * D_BLK

        def strided(t):
            t = jnp.moveaxis(t[:, :, g].reshape(bsz, M, r, hg, d), 2, 1).reshape(bsz * r, M, hg, d)
            return jnp.pad(t, ((0, 0), (0, Mp - M), (0, 0), (0, 0)))

        qg, kg, vg = strided(q), strided(k), strided(v)
        s = _band_logits(qg[:, :, :, None], kg, bias_d[g], D_BLK, half_m, M)
        m = jnp.max(s, axis=-1)
        e = jnp.exp(s - m[..., None])
        den = jnp.sum(e, axis=-1)
        o = _band_values((e / den[..., None]).astype(v.dtype), vg, D_BLK)
        lse = jnp.moveaxis(m + jnp.log(den), -1, 2).reshape(bsz * r, Mp, hg)
        o = jnp.moveaxis(o[:, :M, :, 0].reshape(bsz, r, M, hg, d), 1, 2).reshape(bsz, seq, hg, d)
        lse = jnp.moveaxis(lse[:, :M].reshape(bsz, r, M, hg), 1, 2).reshape(bsz, seq, hg)
        outs.append(o)
        lses.append(lse)
    w = jax.nn.softmax(jnp.stack(lses, axis=0), axis=0)
    o = jnp.sum(w[..., None] * jnp.stack(outs, axis=0).astype(jnp.float32), axis=0)
    return o.astype(v.dtype).reshape(bsz, seq, -1)


def setup_inputs(seed: int = 0) -> dict:
    key = jax.random.key(seed)
    ks = jax.random.split(key, 10)
    nq = jax.random.normal
    return {
        'x': nq(ks[0], (BATCH, SEQ, D_MODEL), jnp.float32),
        'w_in': nq(ks[1], (DEPTH, D_MODEL, IN_WIDTH), jnp.float32) * D_MODEL ** -0.5,
        'w_branch': nq(ks[2], (DEPTH, N_BRANCH, BRANCH_W, D_MODEL), jnp.float32) * BRANCH_W ** -0.5,
        'w_out': nq(ks[3], (DEPTH, D_MODEL, D_MODEL), jnp.float32) * D_MODEL ** -0.5,
        'norm_gain': 1.0 + 0.05 * nq(ks[4], (DEPTH, D_MODEL), jnp.float32),
        'qk_gain': 1.0 + 0.05 * nq(ks[5], (DEPTH, N_BRANCH, 2, HEAD_DIM), jnp.float32),
        'sink': 0.5 * nq(ks[6], (DEPTH, A_HEADS), jnp.float32),
        'lambda_vec': 0.1 * nq(ks[7], (DEPTH, 4, HEAD_DIM), jnp.float32),
        'sub_norm_gain': 1.0 + 0.05 * nq(ks[8], (DEPTH, B_VDIM), jnp.float32),
        'rel_bias': 0.5 * nq(ks[9], (REL_BUCKETS, REL_HEADS), jnp.float32),
    }


def reference(x, w_in, w_branch, w_out, norm_gain, qk_gain, sink, lambda_vec, sub_norm_gain, rel_bias):
    bsz, seq, _ = x.shape
    rows = seq // GRID_W
    row = jnp.repeat(jnp.arange(rows), GRID_W).astype(jnp.float32)
    col = jnp.tile(jnp.arange(GRID_W), rows).astype(jnp.float32)
    ga = A_HEADS // A_KV
    gc = C_HEADS // C_KV
    rb = rel_bias.astype(jnp.float32)
    bias_a = jnp.moveaxis(rb[:, :A_HEADS][_rel_bucket(_band_rel(QBLK))], -1, 0).reshape(A_KV, ga, QBLK, 3 * QBLK)
    tbl_b = rb[:, A_HEADS:A_HEADS + B_HEADS]
    bias_d = []
    for g, (win, r) in enumerate(D_PAIRS):
        lo = A_HEADS + B_HEADS + g * D_HEADS_PER_GROUP
        tb = rb[:, lo:lo + D_HEADS_PER_GROUP][_rel_bucket(_band_rel(D_BLK) * r)]
        bias_d.append(jnp.moveaxis(tb, -1, 0)[:, None])

    for l in range(DEPTH):
        h = _rmsnorm(x, norm_gain[l])
        p = _project(h, w_in[l])
        qa = _rmsnorm(p['a_q'].reshape(bsz, seq, A_KV, ga, HEAD_DIM), qk_gain[l, 0, 0])
        ka = _rmsnorm(p['a_k'].reshape(bsz, seq, A_KV, HEAD_DIM), qk_gain[l, 0, 1])
        va = p['a_v'].reshape(bsz, seq, A_KV, HEAD_DIM)
        oa = _window_sink_attention(qa, ka, va, bias_a, sink[l].reshape(A_KV, ga))
        qb = _rmsnorm(p['b_q'].reshape(bsz, seq, B_HEADS, 2, HEAD_DIM), qk_gain[l, 1, 0])
        kb = _rmsnorm(p['b_k'].reshape(bsz, seq, B_HEADS, 2, HEAD_DIM), qk_gain[l, 1, 1])
        vb = p['b_v'].reshape(bsz, seq, B_HEADS, B_VDIM)
        lam_init = 0.8 - 0.6 * math.exp(-0.3 * l)
        lv = lambda_vec[l].astype(jnp.float32)
        lam = jnp.exp(jnp.sum(lv[0] * lv[1])) - jnp.exp(jnp.sum(lv[2] * lv[3])) + lam_init
        ob = _diff_attention(qb, kb, vb, tbl_b, lam, lam_init, sub_norm_gain[l])
        qc = _axial_rope(_rmsnorm(p['c_q'].reshape(bsz, seq, C_KV, gc, HEAD_DIM), qk_gain[l, 2, 0]), row, col)
        kc = _axial_rope(_rmsnorm(p['c_k'].reshape(bsz, seq, C_KV, HEAD_DIM), qk_gain[l, 2, 1]), row, col)
        vc = p['c_v'].reshape(bsz, seq, C_KV, HEAD_DIM)
        oc = _dense_gqa(qc, kc, vc)
        dshape = (bsz, seq, D_GROUPS, D_HEADS_PER_GROUP, HEAD_DIM)
        qd = _rmsnorm(p['d_q'].reshape(dshape), qk_gain[l, 3, 0])
        kd = _rmsnorm(p['d_k'].reshape(dshape), qk_gain[l, 3, 1])
        vd = p['d_v'].reshape(dshape)
        od = _dilated_attention(qd, kd, vd, bias_d)
        merged = jnp.zeros_like(x)
        for n, o in enumerate((oa, ob, oc, od)):
            gated = o * jax.nn.silu(p['gate'][..., n * BRANCH_W:(n + 1) * BRANCH_W])
            y = gated @ w_branch[l, n]
            merged = merged + jax.nn.sigmoid(p['merge'][..., n * D_MODEL:(n + 1) * D_MODEL]) * y
        x = x + merged @ w_out[l]
    return x
```

```python
import functools
import math

import numpy as np
import jax
import jax.numpy as jnp
from jax import lax
from jax.experimental import pallas as pl
from jax.experimental.pallas import tpu as pltpu

F32 = jnp.float32
BF16 = jnp.bfloat16

D_MODEL = 1024
HEAD_DIM = 64
GRID_W = 64
EPS = 1e-6
A_HEADS, A_KV, A_WIN = 8, 2, 128
B_HEADS = 4
C_HEADS, C_KV = 8, 2
ROPE_THETA = 10000.0
D_PAIRS = ((128, 1), (512, 4), (2048, 16))
D_HEADS_PER_GROUP = 8
N_BRANCH = 4
BRANCH_W = 512
REL_BUCKETS = 32
REL_MAX_DIST = 1024
IN_WIDTH = 13824
NEG = -1e30

OFF_AQ, OFF_AK, OFF_AV = 0, 512, 640
OFF_BQ, OFF_BK, OFF_BV = 768, 1280, 1792
OFF_CQ, OFF_CK, OFF_CV = 2304, 2816, 2944
OFF_DQ, OFF_DK, OFF_DV = 3072, 4608, 6144
OFF_GATE, OFF_MERGE = 7680, 9728

VMEM_LIMIT = 56 * 1024 * 1024

T_RAW, T_NORM, T_ROPE, T_SILU, T_SIGMOID = 0, 1, 2, 3, 4
PROJ_TN = 256


def _col_types():
    t = np.zeros((IN_WIDTH // PROJ_TN,), np.int32)
    norm = np.zeros((IN_WIDTH,), np.float32)
    for lo, hi in ((OFF_AQ, OFF_AV), (OFF_BQ, OFF_BV), (OFF_CQ, OFF_CV), (OFF_DQ, OFF_DV)):
        norm[lo:hi] = 1.0
    for j in range(t.shape[0]):
        lo = j * PROJ_TN
        if lo >= OFF_MERGE:
            t[j] = T_SIGMOID
        elif lo >= OFF_GATE:
            t[j] = T_SILU
        elif norm[lo:lo + PROJ_TN].any():
            t[j] = T_ROPE if OFF_CQ <= lo < OFF_CV else T_NORM
    return t, norm


def _gain_cols(g):
    sc = HEAD_DIM ** -0.5
    one = lambda n: jnp.ones((n,), F32)
    parts = [jnp.tile(g[0, 0], A_HEADS) * sc, jnp.tile(g[0, 1], A_KV), one(128),
             jnp.tile(g[1, 0], 2 * B_HEADS) * sc, jnp.tile(g[1, 1], 2 * B_HEADS), one(512),
             jnp.tile(g[2, 0], C_HEADS) * sc, jnp.tile(g[2, 1], C_KV), one(128),
             jnp.tile(g[3, 0], 24) * sc, jnp.tile(g[3, 1], 24), one(1536),
             one(IN_WIDTH - OFF_GATE)]
    return jnp.concatenate(parts).reshape(1, IN_WIDTH)


def _proj_kernel(types_ref, x_ref, g_ref, w_ref, gain_ref, flag_ref, bd_ref, cos_ref, sin_ref,
                 o_ref, h_ref):
    j = pl.program_id(1)

    @pl.when(j == 0)
    def _():
        x = x_ref[...]
        ms = jnp.mean(x * x, axis=-1, keepdims=True)
        h_ref[...] = (x * lax.rsqrt(ms + EPS) * g_ref[...]).astype(BF16)

    acc = jnp.dot(h_ref[...], w_ref[...], preferred_element_type=F32)
    t = types_ref[j]

    def normed():
        sq = acc * acc
        hi = sq.astype(BF16)
        lo = (sq - hi.astype(F32)).astype(BF16)
        bd = bd_ref[...]
        ms = (jnp.dot(hi, bd, preferred_element_type=F32)
              + jnp.dot(lo, bd, preferred_element_type=F32)) * (1.0 / HEAD_DIM)
        r = jnp.where(flag_ref[...] > 0.0, lax.rsqrt(ms + EPS), 1.0)
        return acc * r * gain_ref[...]

    @pl.when(t == T_RAW)
    def _():
        o_ref[...] = acc.astype(o_ref.dtype)

    @pl.when(t == T_NORM)
    def _():
        o_ref[...] = normed().astype(o_ref.dtype)

    @pl.when(t == T_ROPE)
    def _():
        y = normed()
        tn = y.shape[1]
        c = jnp.concatenate([cos_ref[...]] * (tn // 128), axis=1)
        s = jnp.concatenate([sin_ref[...]] * (tn // 128), axis=1)
        lane = lax.broadcasted_iota(jnp.int32, y.shape, 1) % HEAD_DIM
        half = HEAD_DIM // 2
        partner = jnp.where(lane < half, pltpu.roll(y, tn - half, axis=1), pltpu.roll(y, half, axis=1))
        roped = y * c + partner * s
        o_ref[...] = jnp.where(flag_ref[...] > 0.0, roped, y).astype(o_ref.dtype)

    @pl.when(t == T_SILU)
    def _():
        o_ref[...] = (acc * jax.nn.sigmoid(acc)).astype(o_ref.dtype)

    @pl.when(t == T_SIGMOID)
    def _():
        o_ref[...] = jax.nn.sigmoid(acc).astype(o_ref.dtype)


def _project(x, g, w, gain, flag, types, bd, cos2, sin2, seq, tm=1024):
    n = x.shape[0]
    tn = PROJ_TN
    ns = seq // tm
    grid_spec = pltpu.PrefetchScalarGridSpec(
        num_scalar_prefetch=1,
        grid=(n // tm, IN_WIDTH // tn),
        in_specs=[
            pl.BlockSpec((tm, D_MODEL), lambda i, j, t: (i, 0)),
            pl.BlockSpec((1, D_MODEL), lambda i, j, t: (0, 0)),
            pl.BlockSpec((D_MODEL, tn), lambda i, j, t: (0, j)),
            pl.BlockSpec((1, tn), lambda i, j, t: (0, j)),
            pl.BlockSpec((1, tn), lambda i, j, t: (0, j)),
            pl.BlockSpec((tn, tn), lambda i, j, t: (0, 0)),
            pl.BlockSpec((tm, 128), lambda i, j, t: (i % ns, 0)),
            pl.BlockSpec((tm, 128), lambda i, j, t: (i % ns, 0)),
        ],
        out_specs=pl.BlockSpec((tm, tn), lambda i, j, t: (i, j)),
        scratch_shapes=[pltpu.VMEM((tm, D_MODEL), BF16)],
    )
    return pl.pallas_call(
        _proj_kernel,
        grid_spec=grid_spec,
        out_shape=jax.ShapeDtypeStruct((n, IN_WIDTH), BF16),
        compiler_params=pltpu.CompilerParams(
            dimension_semantics=("parallel", "arbitrary"), vmem_limit_bytes=VMEM_LIMIT),
    )(types, x, g, w, gain, flag, bd, cos2, sin2)


def _band_kernel(*refs, qi_axis, nh, kv_of, tq, w, pad, m_len, has_sink, chain_in, lse_out):
    refs = list(refs)
    q_ref, k_ref, v_ref, bias_ref = refs[:4]
    pos = 4
    sink_ref = prev_o_ref = prev_l_ref = lse_ref = None
    if has_sink:
        sink_ref = refs[pos]; pos += 1
    if chain_in:
        prev_o_ref, prev_l_ref = refs[pos], refs[pos + 1]; pos += 2
    o_ref = refs[pos]; pos += 1
    if lse_out:
        lse_ref = refs[pos]

    qi = pl.program_id(qi_axis)
    ks = pl.multiple_of(jnp.clip(qi * tq - pad, 0, m_len - w), 64)
    dn = (((1,), (1,)), ((), ()))
    for h in range(nh):
        kv = kv_of(h)
        hs = slice(h * HEAD_DIM, (h + 1) * HEAD_DIM)
        q = q_ref[:, hs]
        k = k_ref[pl.ds(ks, w), kv * HEAD_DIM:(kv + 1) * HEAD_DIM]
        v = v_ref[pl.ds(ks, w), kv * HEAD_DIM:(kv + 1) * HEAD_DIM]
        s = lax.dot_general(q, k, dn, preferred_element_type=F32) + bias_ref[h]
        m = jnp.max(s, axis=-1, keepdims=True)
        if has_sink:
            sk = sink_ref[:, h:h + 1]
            m = jnp.maximum(m, sk)
        e = jnp.exp(s - m)
        l = jnp.sum(e, axis=-1, keepdims=True)
        if has_sink:
            l = l + jnp.exp(sk - m)
        o = jnp.dot(e.astype(BF16), v, preferred_element_type=F32) / l
        if chain_in or lse_out:
            lse = jnp.broadcast_to(m + jnp.log(l), o.shape)
            if chain_in:
                lp = prev_l_ref[:, hs]
                op = prev_o_ref[:, hs]
                mx = jnp.maximum(lp, lse)
                a = jnp.exp(lp - mx)
                b = jnp.exp(lse - mx)
                den = a + b
                o = (a * op + b * o) / den
                lse = mx + jnp.log(den)
            if lse_out:
                lse_ref[:, hs] = lse
        o_ref[:, hs] = o.astype(o_ref.dtype)


def _band_bias(tbl, stride, half, tq, w, pad):
    tiles = []
    for off in (0, pad, w - tq):
        rel = jnp.arange(w)[None, :] - off - jnp.arange(tq)[:, None]
        b = jnp.moveaxis(tbl[_bucket(rel * stride)], -1, 0)
        tiles.append(jnp.where((jnp.abs(rel) <= half)[None], b, NEG))
    return jnp.stack(tiles, axis=0).astype(F32)


def _bucket(rel):
    nb = REL_BUCKETS // 2
    exact = nb // 2
    n = jnp.abs(rel)
    large = exact + (jnp.log(jnp.maximum(n, exact).astype(F32) / exact)
                     / math.log(REL_MAX_DIST / exact) * (nb - exact)).astype(jnp.int32)
    large = jnp.minimum(large, nb - 1)
    return jnp.where(rel > 0, nb, 0) + jnp.where(n < exact, n, large)


def _window_attention(p3, bias, sink, tq=128):
    bsz, seq, _ = p3.shape
    pad = A_WIN
    w = tq + 2 * pad
    nq = seq // tq
    case = lambda qi: jnp.where(qi == 0, 0, jnp.where(qi == nq - 1, 2, 1))
    kern = functools.partial(_band_kernel, qi_axis=1, nh=A_HEADS, kv_of=lambda h: h // (A_HEADS // A_KV), tq=tq, w=w,
                             pad=pad, m_len=seq, has_sink=True, chain_in=False, lse_out=False)
    return pl.pallas_call(
        kern,
        grid=(bsz, nq),
        in_specs=[
            pl.BlockSpec((None, tq, 512), lambda b, qi: (b, qi, OFF_AQ // 512)),
            pl.BlockSpec((None, seq, 128), lambda b, qi: (b, 0, OFF_AK // 128)),
            pl.BlockSpec((None, seq, 128), lambda b, qi: (b, 0, OFF_AV // 128)),
            pl.BlockSpec((None, A_HEADS, tq, w), lambda b, qi: (case(qi), 0, 0, 0)),
            pl.BlockSpec((1, A_HEADS), lambda b, qi: (0, 0)),
        ],
        out_specs=pl.BlockSpec((None, tq, BRANCH_W), lambda b, qi: (b, qi, 0)),
        out_shape=jax.ShapeDtypeStruct((bsz, seq, BRANCH_W), BF16),
        compiler_params=pltpu.CompilerParams(
            dimension_semantics=("parallel", "arbitrary"), vmem_limit_bytes=VMEM_LIMIT),
    )(p3, p3, p3, bias, sink.reshape(1, A_HEADS))


def _dilated_group(p3, g, r, bias, prev, last, tq=128):
    bsz, seq, _ = p3.shape
    m_len = seq // r
    pad = 64
    w = min(tq + 2 * pad, m_len)
    nq = m_len // tq
    nblk = IN_WIDTH // 512
    pv = p3.reshape(bsz, m_len, r * IN_WIDTH)
    case = lambda qi: jnp.where(qi == 0, 0, jnp.where(qi == nq - 1, 2, 1))
    chain_in = prev is not None
    kern = functools.partial(_band_kernel, qi_axis=2, nh=D_HEADS_PER_GROUP, kv_of=lambda h: h, tq=tq, w=w, pad=pad,
                             m_len=m_len, has_sink=False, chain_in=chain_in, lse_out=not last)
    in_specs = [
        pl.BlockSpec((None, tq, 512), lambda b, c, qi: (b, qi, c * nblk + OFF_DQ // 512 + g)),
        pl.BlockSpec((None, m_len, 512), lambda b, c, qi: (b, 0, c * nblk + OFF_DK // 512 + g)),
        pl.BlockSpec((None, m_len, 512), lambda b, c, qi: (b, 0, c * nblk + OFF_DV // 512 + g)),
        pl.BlockSpec((None, D_HEADS_PER_GROUP, tq, w), lambda b, c, qi: (case(qi), 0, 0, 0)),
    ]
    args = [pv, pv, pv, bias]
    row_spec = pl.BlockSpec((None, tq, BRANCH_W), lambda b, c, qi: (b, qi, c))
    if chain_in:
        in_specs += [row_spec, row_spec]
        args += [a.reshape(bsz, m_len, r * BRANCH_W) for a in prev]
    o_dtype = BF16 if last else F32
    out_shape = [jax.ShapeDtypeStruct((bsz, m_len, r * BRANCH_W), o_dtype)]
    out_specs = [row_spec]
    if not last:
        out_shape.append(jax.ShapeDtypeStruct((bsz, m_len, r * BRANCH_W), F32))
        out_specs.append(row_spec)
    outs = pl.pallas_call(
        kern,
        grid=(bsz, r, nq),
        in_specs=in_specs,
        out_specs=out_specs,
        out_shape=out_shape,
        compiler_params=pltpu.CompilerParams(
            dimension_semantics=("parallel", "parallel", "arbitrary"), vmem_limit_bytes=VMEM_LIMIT),
    )(*args)
    return [o.reshape(bsz, seq, BRANCH_W) for o in outs]


DIFF_UNIT = 256


def _saturation_distance():
    nb = REL_BUCKETS // 2
    exact = nb // 2
    n = np.arange(exact, 4 * REL_MAX_DIST)
    large = exact + (np.log(n.astype(np.float32) / exact) / math.log(REL_MAX_DIST / exact)
                     * (nb - exact)).astype(np.int32)
    return int(n[large < nb - 1].max()) + 1 + 2


def _ceil_div(a, b):
    return -(-a // b)


def _diff_tiles(tq, tk):
    far = _saturation_distance()
    return -_ceil_div(far + tk - 1, DIFF_UNIT), _ceil_div(far + tq - 1, DIFF_UNIT)


def _diff_kernel(lv_ref, sg_ref, q_ref, k_ref, v_ref, bias_ref, o_ref, m_ref, l_ref, acc_ref,
                 *, tq, tk, nk, lo, hi, lam_init):
    qi = pl.program_id(2)
    m_ref[...] = jnp.full(m_ref.shape, NEG, F32)
    l_ref[...] = jnp.zeros(l_ref.shape, F32)
    acc_ref[...] = jnp.zeros(acc_ref.shape, F32)
    q = q_ref[...]
    dn = (((1,), (1,)), ((), ()))

    def body(ki, carry):
        k0 = pl.multiple_of(ki * tk, tk)
        k = k_ref[pl.ds(k0, tk), :]
        v = v_ref[pl.ds(k0, tk), :]
        dd = jnp.clip(ki * (tk // DIFF_UNIT) - qi * (tq // DIFF_UNIT), lo, hi) - lo
        bias = bias_ref[dd]
        for c in range(2):
            cs = slice(c * HEAD_DIM, (c + 1) * HEAD_DIM)
            s = lax.dot_general(q[:, cs], k[:, cs], dn, preferred_element_type=F32) + bias
            m_prev = m_ref[c]
            m_new = jnp.maximum(m_prev, jnp.max(s, axis=-1, keepdims=True))
            alpha = jnp.exp(m_prev - m_new)
            p = jnp.exp(s - m_new)
            l_ref[c] = alpha * l_ref[c] + jnp.sum(p, axis=-1, keepdims=True)
            acc_ref[c] = alpha * acc_ref[c] + jnp.dot(p.astype(BF16), v, preferred_element_type=F32)
            m_ref[c] = m_new
        return carry

    lax.fori_loop(0, nk, body, 0)
    lv = lv_ref[...]
    lam = (jnp.exp(jnp.sum(lv[0:1] * lv[1:2], axis=-1, keepdims=True))
           - jnp.exp(jnp.sum(lv[2:3] * lv[3:4], axis=-1, keepdims=True)) + lam_init)
    o = acc_ref[0] / l_ref[0] - lam * (acc_ref[1] / l_ref[1])
    ms = jnp.mean(o * o, axis=-1, keepdims=True)
    o = o * lax.rsqrt(ms + EPS) * sg_ref[...] * (1.0 - lam_init)
    o_ref[...] = o.astype(o_ref.dtype)


def _diff_bias(tbl, tq, tk):
    lo, hi = _diff_tiles(tq, tk)
    tiles = []
    for d in range(lo, hi + 1):
        rel = d * DIFF_UNIT + jnp.arange(tk)[None, :] - jnp.arange(tq)[:, None]
        tiles.append(jnp.moveaxis(tbl[_bucket(rel)], -1, 0))
    return jnp.stack(tiles, axis=1).astype(F32)


def _diff_attention(p3, bias, lam_vec, sub_gain, lam_init, tq=256, tk=512):
    bsz, seq, _ = p3.shape
    lo, hi = _diff_tiles(tq, tk)
    nt = hi - lo + 1
    vd = 2 * HEAD_DIM
    kern = functools.partial(_diff_kernel, tq=tq, tk=tk, nk=seq // tk, lo=lo, hi=hi, lam_init=lam_init)
    return pl.pallas_call(
        kern,
        grid=(B_HEADS, bsz, seq // tq),
        in_specs=[
            pl.BlockSpec((4, HEAD_DIM), lambda h, b, qi: (0, 0)),
            pl.BlockSpec((1, vd), lambda h, b, qi: (0, 0)),
            pl.BlockSpec((None, tq, vd), lambda h, b, qi: (b, qi, OFF_BQ // vd + h)),
            pl.BlockSpec((None, seq, vd), lambda h, b, qi: (b, 0, OFF_BK // vd + h)),
            pl.BlockSpec((None, seq, vd), lambda h, b, qi: (b, 0, OFF_BV // vd + h)),
            pl.BlockSpec((None, nt, tq, tk), lambda h, b, qi: (h, 0, 0, 0)),
        ],
        out_specs=pl.BlockSpec((None, tq, vd), lambda h, b, qi: (b, qi, h)),
        out_shape=jax.ShapeDtypeStruct((bsz, seq, BRANCH_W), BF16),
        scratch_shapes=[pltpu.VMEM((2, tq, 1), F32), pltpu.VMEM((2, tq, 1), F32),
                        pltpu.VMEM((2, tq, vd), F32)],
        compiler_params=pltpu.CompilerParams(
            dimension_semantics=("parallel", "parallel", "arbitrary"), vmem_limit_bytes=VMEM_LIMIT),
    )(lam_vec, sub_gain.reshape(1, vd), p3, p3, p3, bias)


def _gqa_kernel(q0_ref, q1_ref, k_ref, v_ref, o_ref, m_ref, l_ref, acc_ref, *, tq, tk, nk):
    grp = C_HEADS // C_KV
    dn = (((1,), (1,)), ((), ()))
    for kv, q_ref in enumerate((q0_ref, q1_ref)):
        q4 = jnp.concatenate([q_ref[:, g * HEAD_DIM:(g + 1) * HEAD_DIM] for g in range(grp)], axis=0)
        m_ref[...] = jnp.full(m_ref.shape, NEG, F32)
        l_ref[...] = jnp.zeros(l_ref.shape, F32)
        acc_ref[...] = jnp.zeros(acc_ref.shape, F32)

        def body(ki, carry):
            k0 = pl.multiple_of(ki * tk, tk)
            k = k_ref[pl.ds(k0, tk), kv * HEAD_DIM:(kv + 1) * HEAD_DIM]
            v = v_ref[pl.ds(k0, tk), kv * HEAD_DIM:(kv + 1) * HEAD_DIM]
            s = lax.dot_general(q4, k, dn, preferred_element_type=F32)
            m_prev = m_ref[...]
            m_new = jnp.maximum(m_prev, jnp.max(s, axis=-1, keepdims=True))
            alpha = jnp.exp(m_prev - m_new)
            p = jnp.exp(s - m_new)
            l_ref[...] = alpha * l_ref[...] + jnp.sum(p, axis=-1, keepdims=True)
            acc_ref[...] = alpha * acc_ref[...] + jnp.dot(p.astype(BF16), v, preferred_element_type=F32)
            m_ref[...] = m_new
            return carry

        lax.fori_loop(0, nk, body, 0)
        o = acc_ref[...] / l_ref[...]
        for g in range(grp):
            h = kv * grp + g
            o_ref[:, h * HEAD_DIM:(h + 1) * HEAD_DIM] = o[g * tq:(g + 1) * tq].astype(o_ref.dtype)


def _dense_gqa(p3, tq=256, tk=512):
    bsz, seq, _ = p3.shape
    grp = C_HEADS // C_KV
    kern = functools.partial(_gqa_kernel, tq=tq, tk=tk, nk=seq // tk)
    return pl.pallas_call(
        kern,
        grid=(bsz, seq // tq),
        in_specs=[
            pl.BlockSpec((None, tq, 256), lambda b, qi: (b, qi, OFF_CQ // 256)),
            pl.BlockSpec((None, tq, 256), lambda b, qi: (b, qi, OFF_CQ // 256 + 1)),
            pl.BlockSpec((None, seq, 128), lambda b, qi: (b, 0, OFF_CK // 128)),
            pl.BlockSpec((None, seq, 128), lambda b, qi: (b, 0, OFF_CV // 128)),
        ],
        out_specs=pl.BlockSpec((None, tq, BRANCH_W), lambda b, qi: (b, qi, 0)),
        out_shape=jax.ShapeDtypeStruct((bsz, seq, BRANCH_W), BF16),
        scratch_shapes=[pltpu.VMEM((grp * tq, 1), F32), pltpu.VMEM((grp * tq, 1), F32),
                        pltpu.VMEM((grp * tq, HEAD_DIM), F32)],
        compiler_params=pltpu.CompilerParams(
            dimension_semantics=("parallel", "arbitrary"), vmem_limit_bytes=VMEM_LIMIT),
    )(p3, p3, p3, p3)


def _merge_kernel(*refs):
    x_ref = refs[0]
    o_refs = refs[1:5]
    gate_refs = refs[5:9]
    merge_refs = refs[9:17]
    wb_ref, wo_ref, out_ref = refs[17], refs[18], refs[19]
    halves = []
    for half in range(2):
        merged = None
        for n in range(N_BRANCH):
            gated = (o_refs[n][...].astype(F32) * gate_refs[n][...].astype(F32)).astype(BF16)
            y = jnp.dot(gated, wb_ref[n, :, half * 512:(half + 1) * 512], preferred_element_type=F32)
            term = merge_refs[2 * n + half][...].astype(F32) * y
            merged = term if merged is None else merged + term
        halves.append(merged.astype(BF16))
    merged = jnp.concatenate(halves, axis=1)
    out_ref[...] = x_ref[...] + jnp.dot(merged, wo_ref[...], preferred_element_type=F32)


def _merge(x, outs, p, wb, wo, tm=256):
    n = x.shape[0]
    row = lambda c: pl.BlockSpec((tm, 512), lambda i, c=c: (i, c))
    in_specs = [pl.BlockSpec((tm, D_MODEL), lambda i: (i, 0))]
    in_specs += [row(0)] * N_BRANCH
    in_specs += [row(OFF_GATE // 512 + k) for k in range(N_BRANCH)]
    in_specs += [row(OFF_MERGE // 512 + k) for k in range(2 * N_BRANCH)]
    in_specs += [pl.BlockSpec((N_BRANCH, BRANCH_W, D_MODEL), lambda i: (0, 0, 0)),
                 pl.BlockSpec((D_MODEL, D_MODEL), lambda i: (0, 0))]
    return pl.pallas_call(
        _merge_kernel,
        grid=(n // tm,),
        in_specs=in_specs,
        out_specs=pl.BlockSpec((tm, D_MODEL), lambda i: (i, 0)),
        out_shape=jax.ShapeDtypeStruct((n, D_MODEL), F32),
        compiler_params=pltpu.CompilerParams(
            dimension_semantics=("parallel",), vmem_limit_bytes=VMEM_LIMIT),
    )(x, *outs, *([p] * 12), wb, wo)


def kernel(x, w_in, w_branch, w_out, norm_gain, qk_gain, sink, lambda_vec, sub_norm_gain, rel_bias):
    bsz, seq, _ = x.shape
    n = bsz * seq
    depth = w_in.shape[0]
    rb = rel_bias.astype(F32)

    rows = seq // GRID_W
    row = jnp.repeat(jnp.arange(rows), GRID_W).astype(F32)
    col = jnp.tile(jnp.arange(GRID_W), rows).astype(F32)
    nf = HEAD_DIM // 4
    freqs = ROPE_THETA ** (-jnp.arange(nf, dtype=F32) / nf)
    ang = jnp.concatenate([row[:, None] * freqs, col[:, None] * freqs], axis=-1)
    cos2 = jnp.tile(jnp.cos(ang), (1, 4))
    sin2 = jnp.tile(jnp.concatenate([-jnp.sin(ang), jnp.sin(ang)], axis=-1), (1, 2))

    tq_a = 128
    bias_a = _band_bias(rb[:, :A_HEADS], 1, A_WIN, tq_a, tq_a + 2 * A_WIN, A_WIN)
    tq_b, tk_b = 256, 512
    bias_b = _diff_bias(rb[:, A_HEADS:A_HEADS + B_HEADS], tq_b, tk_b)
    tq_d = 128
    bias_d = []
    for g, (win, r) in enumerate(D_PAIRS):
        lo = A_HEADS + B_HEADS + g * D_HEADS_PER_GROUP
        wd = min(tq_d + 128, seq // r)
        bias_d.append(_band_bias(rb[:, lo:lo + D_HEADS_PER_GROUP], r, win // (2 * r), tq_d, wd, 64))

    types_np, flag_np = _col_types()
    types = jnp.asarray(types_np)
    flag = jnp.asarray(flag_np).reshape(1, IN_WIDTH)
    blk = np.arange(PROJ_TN) // HEAD_DIM
    bd = jnp.asarray((blk[:, None] == blk[None, :]).astype(np.float32), dtype=BF16)

    xf = x.reshape(n, D_MODEL)
    for l in range(depth):
        p = _project(xf, norm_gain[l].reshape(1, D_MODEL), w_in[l].astype(BF16), _gain_cols(qk_gain[l]),
                     flag, types, bd, cos2, sin2, seq)
        p3 = p.reshape(bsz, seq, IN_WIDTH)
        oa = _window_attention(p3, bias_a, sink[l], tq=tq_a)
        lam_init = 0.8 - 0.6 * math.exp(-0.3 * l)
        ob = _diff_attention(p3, bias_b, lambda_vec[l], sub_norm_gain[l], lam_init, tq=tq_b, tk=tk_b)
        oc = _dense_gqa(p3)
        prev = None
        for g, (win, r) in enumerate(D_PAIRS):
            prev = _dilated_group(p3, g, r, bias_d[g], prev, last=(g == len(D_PAIRS) - 1), tq=tq_d)
        od = prev[0]
        outs = [o.reshape(n, BRANCH_W) for o in (oa, ob, oc, od)]
        xf = _merge(xf, outs, p, w_branch[l].astype(BF16), w_out[l].astype(BF16))
    return xf.reshape(bsz, seq, D_MODEL)
```

```python
import functools
import math

import numpy as np
import jax
import jax.numpy as jnp
from jax import lax
from jax.experimental import pallas as pl
from jax.experimental.pallas import tpu as pltpu

F32 = jnp.float32
BF16 = jnp.bfloat16

D_MODEL = 1024
HEAD_DIM = 64
GRID_W = 64
EPS = 1e-6
A_HEADS, A_KV, A_WIN = 8, 2, 128
B_HEADS = 4
C_HEADS, C_KV = 8, 2
ROPE_THETA = 10000.0
D_PAIRS = ((128, 1), (512, 4), (2048, 16))
D_HEADS_PER_GROUP = 8
N_BRANCH = 4
BRANCH_W = 512
REL_BUCKETS = 32
REL_MAX_DIST = 1024
IN_WIDTH = 13824
NEG = -1e30
LANES = 128

OFF_AQ, OFF_AK, OFF_AV = 0, 512, 640
OFF_BQ, OFF_BK, OFF_BV = 768, 1280, 1792
OFF_CQ, OFF_CK, OFF_CV = 2304, 2816, 2944
OFF_DQ, OFF_DK, OFF_DV = 3072, 4608, 6144
OFF_GATE, OFF_MERGE = 7680, 9728

VMEM_LIMIT = 56 * 1024 * 1024

K_RAW, K_NORM, K_ROPE, K_SILU, K_SIGMOID = range(5)


def _norm_flags():
    norm = np.zeros((IN_WIDTH,), np.float32)
    for lo, hi in ((OFF_AQ, OFF_AV), (OFF_BQ, OFF_BV), (OFF_CQ, OFF_CV), (OFF_DQ, OFF_DV)):
        norm[lo:hi] = 1.0
    return norm.reshape(1, IN_WIDTH)


def _gain_cols(g):
    sc = HEAD_DIM ** -0.5
    one = lambda n: jnp.ones((n,), F32)
    parts = [jnp.tile(g[0, 0], A_HEADS) * sc, jnp.tile(g[0, 1], A_KV), one(128),
             jnp.tile(g[1, 0], 2 * B_HEADS) * sc, jnp.tile(g[1, 1], 2 * B_HEADS), one(512),
             jnp.tile(g[2, 0], C_HEADS) * sc, jnp.tile(g[2, 1], C_KV), one(128),
             jnp.tile(g[3, 0], 24) * sc, jnp.tile(g[3, 1], 24), one(1536),
             one(IN_WIDTH - OFF_GATE)]
    return jnp.concatenate(parts).reshape(1, IN_WIDTH)


def _prenorm_kernel(x_ref, g_ref, o_ref):
    x = x_ref[...]
    ms = jnp.mean(x * x, axis=-1, keepdims=True)
    o_ref[...] = (x * lax.rsqrt(ms + EPS) * g_ref[...]).astype(o_ref.dtype)


def _prenorm(x, g, tm=1024):
    n = x.shape[0]
    return pl.pallas_call(
        _prenorm_kernel,
        grid=(n // tm,),
        in_specs=[pl.BlockSpec((tm, D_MODEL), lambda i: (i, 0)),
                  pl.BlockSpec((1, D_MODEL), lambda i: (0, 0))],
        out_specs=pl.BlockSpec((tm, D_MODEL), lambda i: (i, 0)),
        out_shape=jax.ShapeDtypeStruct((n, D_MODEL), BF16),
        compiler_params=pltpu.CompilerParams(dimension_semantics=("parallel",), vmem_limit_bytes=VMEM_LIMIT),
    )(x, g)


PROJ_CHUNK = 256
NORM_W = 256


def _proj_kernel(*refs, kind, r, tm, tn):
    refs = list(refs)
    h_ref, w_ref = refs[:2]
    pos = 2
    if kind in (K_NORM, K_ROPE):
        gain_ref, flag_ref, bd_ref = refs[pos:pos + 3]; pos += 3
    if kind == K_ROPE:
        cos_ref, sin_ref = refs[pos:pos + 2]; pos += 2
    o_ref = refs[pos]; pos += 1
    slab_ref = refs[pos] if r > 1 else None

    for rc in range(tm // PROJ_CHUNK):
        rows = slice(rc * PROJ_CHUNK, (rc + 1) * PROJ_CHUNK)
        acc = jnp.dot(h_ref[rows, :], w_ref[...], preferred_element_type=F32)
        if kind in (K_NORM, K_ROPE):
            sq = acc * acc
            hi = sq.astype(BF16)
            lo = (sq - hi.astype(F32)).astype(BF16)
            bd = bd_ref[...]
            ms = jnp.concatenate(
                [jnp.dot(hi[:, c:c + NORM_W], bd, preferred_element_type=F32)
                 + jnp.dot(lo[:, c:c + NORM_W], bd, preferred_element_type=F32) for c in range(0, tn, NORM_W)],
                axis=1) * (1.0 / HEAD_DIM)
            flag = flag_ref[...] > 0.0
            y = acc * jnp.where(flag, lax.rsqrt(ms + EPS), 1.0) * gain_ref[...]
            if kind == K_ROPE:
                c = jnp.concatenate([cos_ref[rows, :]] * (tn // LANES), axis=1)
                s = jnp.concatenate([sin_ref[rows, :]] * (tn // LANES), axis=1)
                lane = lax.broadcasted_iota(jnp.int32, y.shape, 1) & (HEAD_DIM - 1)
                half = HEAD_DIM // 2
                partner = jnp.where(lane < half, pltpu.roll(y, tn - half, axis=1), pltpu.roll(y, half, axis=1))
                y = jnp.where(flag, y * c + partner * s, y)
        elif kind == K_SILU:
            y = acc * (0.5 * jnp.tanh(0.5 * acc) + 0.5)
        elif kind == K_SIGMOID:
            y = 0.5 * jnp.tanh(0.5 * acc) + 0.5
        else:
            y = acc
        if r == 1:
            o_ref[rows, :] = y.astype(o_ref.dtype)
        else:
            for s_ in range(tn // LANES):
                slab_ref[s_, rows, :] = y[:, s_ * LANES:(s_ + 1) * LANES]
    if r > 1:
        for c in range(r):
            for s_ in range(tn // LANES):
                o_ref[c, :, s_ * LANES:(s_ + 1) * LANES] = (
                    slab_ref[s_, pl.ds(c, tm // r, stride=r), :].astype(o_ref.dtype))


def _proj_call(h, w, *, kind, col0, step, nj, tn, seq, r=1, gain=None, flag=None, bd=None, cos=None, sin=None,
               tm=1024):
    n = h.shape[0]
    ns = seq // tm
    cb = col0 // tn
    in_specs = [pl.BlockSpec((tm, D_MODEL), lambda i, j: (i, 0)),
                pl.BlockSpec((D_MODEL, tn), lambda i, j: (0, cb + j * step))]
    args = [h, w]
    if kind in (K_NORM, K_ROPE):
        in_specs += [pl.BlockSpec((1, tn), lambda i, j: (0, cb + j * step)),
                     pl.BlockSpec((1, tn), lambda i, j: (0, cb + j * step)),
                     pl.BlockSpec((NORM_W, NORM_W), lambda i, j: (0, 0))]
        args += [gain, flag, bd]
    if kind == K_ROPE:
        in_specs += [pl.BlockSpec((tm, LANES), lambda i, j: (i % ns, 0))] * 2
        args += [cos, sin]
    scratch = []
    if r == 1:
        out_shape = jax.ShapeDtypeStruct((n, nj * tn), BF16)
        out_spec = pl.BlockSpec((tm, tn), lambda i, j: (i, j))
    else:
        out_shape = jax.ShapeDtypeStruct((n // seq, r, seq // r, nj * tn), BF16)
        out_spec = pl.BlockSpec((None, r, tm // r, tn), lambda i, j: (i // ns, 0, i % ns, j))
        scratch = [pltpu.VMEM((tn // LANES, tm, LANES), F32)]
    return pl.pallas_call(
        functools.partial(_proj_kernel, kind=kind, r=r, tm=tm, tn=tn),
        grid=(n // tm, nj),
        in_specs=in_specs,
        out_specs=out_spec,
        out_shape=out_shape,
        scratch_shapes=scratch,
        compiler_params=pltpu.CompilerParams(
            dimension_semantics=("parallel", "parallel"), vmem_limit_bytes=VMEM_LIMIT),
    )(*args)


def _bucket(rel):
    nb = REL_BUCKETS // 2
    exact = nb // 2
    n = jnp.abs(rel)
    large = exact + (jnp.log(jnp.maximum(n, exact).astype(F32) / exact)
                     / math.log(REL_MAX_DIST / exact) * (nb - exact)).astype(jnp.int32)
    large = jnp.minimum(large, nb - 1)
    return jnp.where(rel > 0, nb, 0) + jnp.where(n < exact, n, large)


def _lookup(tbl, bucket):
    shape = (tbl.shape[1],) + (1,) * bucket.ndim
    out = jnp.zeros((tbl.shape[1],) + bucket.shape, F32)
    for j in range(REL_BUCKETS):
        out = jnp.where(bucket[None] == j, tbl[j].reshape(shape), out)
    return out


def _band_bias(tbl, stride, half, tq, w, pad):
    tiles = []
    for off in (0, pad, w - tq):
        rel = jnp.arange(w)[None, :] - off - jnp.arange(tq)[:, None]
        b = _lookup(tbl, _bucket(rel * stride))
        tiles.append(jnp.where((jnp.abs(rel) <= half)[None], b, NEG))
    return jnp.stack(tiles, axis=0)


def _band_kernel(*refs, qi_axis, nh, kv_of, tq, w, pad, m_len, has_sink, lse_out):
    refs = list(refs)
    q_ref, k_ref, v_ref, bias_ref = refs[:4]
    pos = 4
    sink_ref = lse_ref = None
    if has_sink:
        sink_ref = refs[pos]; pos += 1
    o_ref = refs[pos]; pos += 1
    if lse_out:
        lse_ref = refs[pos]

    qi = pl.program_id(qi_axis)
    ks = pl.multiple_of(jnp.clip(qi * tq - pad, 0, m_len - w), 64)
    dn = (((1,), (1,)), ((), ()))
    for h in range(nh):
        kv = kv_of(h)
        hs = slice(h * HEAD_DIM, (h + 1) * HEAD_DIM)
        q = q_ref[:, hs]
        k = k_ref[pl.ds(ks, w), kv * HEAD_DIM:(kv + 1) * HEAD_DIM]
        v = v_ref[pl.ds(ks, w), kv * HEAD_DIM:(kv + 1) * HEAD_DIM]
        s = lax.dot_general(q, k, dn, preferred_element_type=F32) + bias_ref[h]
        m = jnp.max(s, axis=-1, keepdims=True)
        if has_sink:
            sk = sink_ref[:, h:h + 1]
            m = jnp.maximum(m, sk)
        e = jnp.exp(s - m)
        l = jnp.sum(e, axis=-1, keepdims=True)
        if has_sink:
            l = l + jnp.exp(sk - m)
        o = jnp.dot(e.astype(BF16), v, preferred_element_type=F32) / l
        if lse_out:
            lse_ref[:, hs] = jnp.broadcast_to(m + jnp.log(l), o.shape)
        o_ref[:, hs] = o.astype(o_ref.dtype)


def _window_attention(p1, bias, sink, tq=128):
    bsz, seq, _ = p1.shape
    pad = A_WIN
    w = tq + 2 * pad
    nq = seq // tq
    case = lambda qi: jnp.where(qi == 0, 0, jnp.where(qi == nq - 1, 2, 1))
    kern = functools.partial(_band_kernel, qi_axis=1, nh=A_HEADS, kv_of=lambda h: h // (A_HEADS // A_KV),
                             tq=tq, w=w, pad=pad, m_len=seq, has_sink=True, lse_out=False)
    return pl.pallas_call(
        kern,
        grid=(bsz, nq),
        in_specs=[
            pl.BlockSpec((None, tq, 512), lambda b, qi: (b, qi, OFF_AQ // 512)),
            pl.BlockSpec((None, seq, 128), lambda b, qi: (b, 0, OFF_AK // 128)),
            pl.BlockSpec((None, seq, 128), lambda b, qi: (b, 0, OFF_AV // 128)),
            pl.BlockSpec((None, A_HEADS, tq, w), lambda b, qi: (case(qi), 0, 0, 0)),
            pl.BlockSpec((1, A_HEADS), lambda b, qi: (0, 0)),
        ],
        out_specs=pl.BlockSpec((None, tq, BRANCH_W), lambda b, qi: (b, qi, 0)),
        out_shape=jax.ShapeDtypeStruct((bsz, seq, BRANCH_W), BF16),
        compiler_params=pltpu.CompilerParams(
            dimension_semantics=("parallel", "arbitrary"), vmem_limit_bytes=VMEM_LIMIT),
    )(p1, p1, p1, bias, sink.reshape(1, A_HEADS))


def _dilated_group(qk, v, bias, tq=128):
    bsz, r, m_len, _ = v.shape
    pad = 64
    w = min(tq + 2 * pad, m_len)
    nq = m_len // tq
    case = lambda qi: jnp.where(qi == 0, 0, jnp.where(qi == nq - 1, 2, 1))
    kern = functools.partial(_band_kernel, qi_axis=2, nh=D_HEADS_PER_GROUP, kv_of=lambda h: h, tq=tq, w=w,
                             pad=pad, m_len=m_len, has_sink=False, lse_out=True)
    row_spec = pl.BlockSpec((None, None, tq, BRANCH_W), lambda b, c, qi: (b, c, qi, 0))
    return pl.pallas_call(
        kern,
        grid=(bsz, r, nq),
        in_specs=[
            row_spec,
            pl.BlockSpec((None, None, m_len, 512), lambda b, c, qi: (b, c, 0, 1)),
            pl.BlockSpec((None, None, m_len, 512), lambda b, c, qi: (b, c, 0, 0)),
            pl.BlockSpec((None, D_HEADS_PER_GROUP, tq, w), lambda b, c, qi: (case(qi), 0, 0, 0)),
        ],
        out_specs=[row_spec, row_spec],
        out_shape=[jax.ShapeDtypeStruct((bsz, r, m_len, BRANCH_W), BF16),
                   jax.ShapeDtypeStruct((bsz, r, m_len, BRANCH_W), F32)],
        compiler_params=pltpu.CompilerParams(
            dimension_semantics=("parallel", "parallel", "arbitrary"), vmem_limit_bytes=VMEM_LIMIT),
    )(qk, qk, v, bias)


DIFF_UNIT = 256


def _saturation_distance():
    nb = REL_BUCKETS // 2
    exact = nb // 2
    n = np.arange(exact, 4 * REL_MAX_DIST)
    large = exact + (np.log(n.astype(np.float32) / exact) / math.log(REL_MAX_DIST / exact)
                     * (nb - exact)).astype(np.int32)
    return int(n[large < nb - 1].max()) + 1 + 2


def _ceil_div(a, b):
    return -(-a // b)


def _diff_tiles(tq, tk):
    far = _saturation_distance()
    return -_ceil_div(far + tk - 1, DIFF_UNIT), _ceil_div(far + tq - 1, DIFF_UNIT)


def _diff_kernel(lv_ref, sg_ref, q_ref, k_ref, v_ref, bias_ref, o_ref, m_ref, l_ref, acc_ref,
                 *, tq, tk, nk, lo, hi, lam_init):
    qi = pl.program_id(2)
    m_ref[...] = jnp.full(m_ref.shape, NEG, F32)
    l_ref[...] = jnp.zeros(l_ref.shape, F32)
    acc_ref[...] = jnp.zeros(acc_ref.shape, F32)
    q = q_ref[...]
    dn = (((1,), (1,)), ((), ()))

    def body(ki, carry):
        k0 = pl.multiple_of(ki * tk, tk)
        k = k_ref[pl.ds(k0, tk), :]
        v = v_ref[pl.ds(k0, tk), :]
        dd = jnp.clip(ki * (tk // DIFF_UNIT) - qi * (tq // DIFF_UNIT), lo, hi) - lo
        bias = bias_ref[dd]
        for c in range(2):
            cs = slice(c * HEAD_DIM, (c + 1) * HEAD_DIM)
            s = lax.dot_general(q[:, cs], k[:, cs], dn, preferred_element_type=F32) + bias
            m_prev = m_ref[c]
            m_new = jnp.maximum(m_prev, jnp.max(s, axis=-1, keepdims=True))
            alpha = jnp.exp(m_prev - m_new)
            p = jnp.exp(s - m_new)
            l_ref[c] = alpha * l_ref[c] + jnp.sum(p, axis=-1, keepdims=True)
            acc_ref[c] = alpha * acc_ref[c] + jnp.dot(p.astype(BF16), v, preferred_element_type=F32)
            m_ref[c] = m_new
        return carry

    lax.fori_loop(0, nk, body, 0)
    lv = lv_ref[...]
    lam = (jnp.exp(jnp.sum(lv[0:1] * lv[1:2], axis=-1, keepdims=True))
           - jnp.exp(jnp.sum(lv[2:3] * lv[3:4], axis=-1, keepdims=True)) + lam_init)
    o = acc_ref[0] / l_ref[0] - lam * (acc_ref[1] / l_ref[1])
    ms = jnp.mean(o * o, axis=-1, keepdims=True)
    o = o * lax.rsqrt(ms + EPS) * sg_ref[...] * (1.0 - lam_init)
    o_ref[...] = o.astype(o_ref.dtype)


def _diff_bias(tbl, tq, tk):
    lo, hi = _diff_tiles(tq, tk)
    tiles = []
    for d in range(lo, hi + 1):
        rel = d * DIFF_UNIT + jnp.arange(tk)[None, :] - jnp.arange(tq)[:, None]
        tiles.append(_lookup(tbl, _bucket(rel)))
    return jnp.stack(tiles, axis=1)


def _diff_attention(p1, pv, bias, lam_vec, sub_gain, lam_init, tq=256, tk=512):
    bsz, seq, _ = p1.shape
    lo, hi = _diff_tiles(tq, tk)
    nt = hi - lo + 1
    vd = 2 * HEAD_DIM
    kern = functools.partial(_diff_kernel, tq=tq, tk=tk, nk=seq // tk, lo=lo, hi=hi, lam_init=lam_init)
    return pl.pallas_call(
        kern,
        grid=(B_HEADS, bsz, seq // tq),
        in_specs=[
            pl.BlockSpec((4, HEAD_DIM), lambda h, b, qi: (0, 0)),
            pl.BlockSpec((1, vd), lambda h, b, qi: (0, 0)),
            pl.BlockSpec((None, tq, vd), lambda h, b, qi: (b, qi, OFF_BQ // vd + h)),
            pl.BlockSpec((None, seq, vd), lambda h, b, qi: (b, 0, OFF_BK // vd + h)),
            pl.BlockSpec((None, seq, vd), lambda h, b, qi: (b, 0, h)),
            pl.BlockSpec((None, nt, tq, tk), lambda h, b, qi: (h, 0, 0, 0)),
        ],
        out_specs=pl.BlockSpec((None, tq, vd), lambda h, b, qi: (b, qi, h)),
        out_shape=jax.ShapeDtypeStruct((bsz, seq, BRANCH_W), BF16),
        scratch_shapes=[pltpu.VMEM((2, tq, 1), F32), pltpu.VMEM((2, tq, 1), F32),
                        pltpu.VMEM((2, tq, vd), F32)],
        compiler_params=pltpu.CompilerParams(
            dimension_semantics=("parallel", "parallel", "arbitrary"), vmem_limit_bytes=VMEM_LIMIT),
    )(lam_vec, sub_gain.reshape(1, vd), p1, p1, pv, bias)


def _gqa_kernel(q0_ref, q1_ref, k_ref, v_ref, o_ref, m_ref, l_ref, acc_ref, *, tq, tk, nk):
    grp = C_HEADS // C_KV
    dn = (((1,), (1,)), ((), ()))
    for kv, q_ref in enumerate((q0_ref, q1_ref)):
        q4 = jnp.concatenate([q_ref[:, g * HEAD_DIM:(g + 1) * HEAD_DIM] for g in range(grp)], axis=0)
        m_ref[...] = jnp.full(m_ref.shape, NEG, F32)
        l_ref[...] = jnp.zeros(l_ref.shape, F32)
        acc_ref[...] = jnp.zeros(acc_ref.shape, F32)

        def body(ki, carry):
            k0 = pl.multiple_of(ki * tk, tk)
            k = k_ref[pl.ds(k0, tk), kv * HEAD_DIM:(kv + 1) * HEAD_DIM]
            v = v_ref[pl.ds(k0, tk), kv * HEAD_DIM:(kv + 1) * HEAD_DIM]
            s = lax.dot_general(q4, k, dn, preferred_element_type=F32)
            m_prev = m_ref[...]
            m_new = jnp.maximum(m_prev, jnp.max(s, axis=-1, keepdims=True))
            alpha = jnp.exp(m_prev - m_new)
            p = jnp.exp(s - m_new)
            l_ref[...] = alpha * l_ref[...] + jnp.sum(p, axis=-1, keepdims=True)
            acc_ref[...] = alpha * acc_ref[...] + jnp.dot(p.astype(BF16), v, preferred_element_type=F32)
            m_ref[...] = m_new
            return carry

        lax.fori_loop(0, nk, body, 0)
        o = acc_ref[...] / l_ref[...]
        for g in range(grp):
            h = kv * grp + g
            o_ref[:, h * HEAD_DIM:(h + 1) * HEAD_DIM] = o[g * tq:(g + 1) * tq].astype(o_ref.dtype)


def _dense_gqa(pc, tq=256, tk=512):
    bsz, seq, _ = pc.shape
    grp = C_HEADS // C_KV
    kern = functools.partial(_gqa_kernel, tq=tq, tk=tk, nk=seq // tk)
    return pl.pallas_call(
        kern,
        grid=(bsz, seq // tq),
        in_specs=[
            pl.BlockSpec((None, tq, 256), lambda b, qi: (b, qi, 0)),
            pl.BlockSpec((None, tq, 256), lambda b, qi: (b, qi, 1)),
            pl.BlockSpec((None, seq, 128), lambda b, qi: (b, 0, 4)),
            pl.BlockSpec((None, seq, 128), lambda b, qi: (b, 0, 5)),
        ],
        out_specs=pl.BlockSpec((None, tq, BRANCH_W), lambda b, qi: (b, qi, 0)),
        out_shape=jax.ShapeDtypeStruct((bsz, seq, BRANCH_W), BF16),
        scratch_shapes=[pltpu.VMEM((grp * tq, 1), F32), pltpu.VMEM((grp * tq, 1), F32),
                        pltpu.VMEM((grp * tq, HEAD_DIM), F32)],
        compiler_params=pltpu.CompilerParams(
            dimension_semantics=("parallel", "arbitrary"), vmem_limit_bytes=VMEM_LIMIT),
    )(pc, pc, pc, pc)


def _merge_kernel(*refs, tm):
    x_ref, oa_ref, ob_ref, oc_ref = refs[:4]
    d_refs = refs[4:10]
    gate_ref, merge_ref, wb_ref, wo_ref, out_ref, so_ref, sl_ref = refs[10:17]
    nslab = BRANCH_W // LANES

    def natural(ref, scr, r):
        if r == 1:
            return ref[0].astype(F32)
        for c in range(r):
            blk = ref[c].astype(F32)
            for s in range(nslab):
                scr[s, pl.ds(c, tm // r, stride=r), :] = blk[:, s * LANES:(s + 1) * LANES]
        return jnp.concatenate([scr[s] for s in range(nslab)], axis=1)

    os_, ls_ = [], []
    for g, (_, r) in enumerate(D_PAIRS):
        os_.append(natural(d_refs[2 * g], so_ref, r))
        ls_.append(natural(d_refs[2 * g + 1], sl_ref, r))
    mx = jnp.maximum(jnp.maximum(ls_[0], ls_[1]), ls_[2])
    es = [jnp.exp(l - mx) for l in ls_]
    od = (es[0] * os_[0] + es[1] * os_[1] + es[2] * os_[2]) / (es[0] + es[1] + es[2])

    branches = [oa_ref[...].astype(F32), ob_ref[...].astype(F32), oc_ref[...].astype(F32), od]
    gated = [(branches[n] * gate_ref[:, n * BRANCH_W:(n + 1) * BRANCH_W].astype(F32)).astype(BF16)
             for n in range(N_BRANCH)]
    halves = []
    for half in range(2):
        cols = slice(half * 512, (half + 1) * 512)
        merged = None
        for n in range(N_BRANCH):
            y = jnp.dot(gated[n], wb_ref[n, :, cols], preferred_element_type=F32)
            term = merge_ref[:, n * D_MODEL + half * 512:n * D_MODEL + (half + 1) * 512].astype(F32) * y
            merged = term if merged is None else merged + term
        halves.append(merged.astype(BF16))
    merged = jnp.concatenate(halves, axis=1)
    out_ref[...] = x_ref[...] + jnp.dot(merged, wo_ref[...], preferred_element_type=F32)


def _merge(x, oa, ob, oc, d_outs, gate, merge, wb, wo, seq, tm=256):
    n = x.shape[0]
    ns = seq // tm
    row = pl.BlockSpec((tm, BRANCH_W), lambda i: (i, 0))
    in_specs = [pl.BlockSpec((tm, D_MODEL), lambda i: (i, 0)), row, row, row]
    for _, r in D_PAIRS:
        in_specs += [pl.BlockSpec((None, r, tm // r, BRANCH_W), lambda i: (i // ns, 0, i % ns, 0))] * 2
    in_specs += [pl.BlockSpec((tm, N_BRANCH * BRANCH_W), lambda i: (i, 0)),
                 pl.BlockSpec((tm, N_BRANCH * D_MODEL), lambda i: (i, 0)),
                 pl.BlockSpec((N_BRANCH, BRANCH_W, D_MODEL), lambda i: (0, 0, 0)),
                 pl.BlockSpec((D_MODEL, D_MODEL), lambda i: (0, 0))]
    return pl.pallas_call(
        functools.partial(_merge_kernel, tm=tm),
        grid=(n // tm,),
        in_specs=in_specs,
        out_specs=pl.BlockSpec((tm, D_MODEL), lambda i: (i, 0)),
        out_shape=jax.ShapeDtypeStruct((n, D_MODEL), F32),
        scratch_shapes=[pltpu.VMEM((BRANCH_W // LANES, tm, LANES), F32)] * 2,
        compiler_params=pltpu.CompilerParams(
            dimension_semantics=("parallel",), vmem_limit_bytes=VMEM_LIMIT),
    )(x, oa, ob, oc, *d_outs, gate, merge, wb, wo)


def kernel(x, w_in, w_branch, w_out, norm_gain, qk_gain, sink, lambda_vec, sub_norm_gain, rel_bias):
    bsz, seq, _ = x.shape
    n = bsz * seq
    depth = w_in.shape[0]
    rb = rel_bias.astype(F32)

    rows = seq // GRID_W
    row = jnp.repeat(jnp.arange(rows), GRID_W).astype(F32)
    col = jnp.tile(jnp.arange(GRID_W), rows).astype(F32)
    nf = HEAD_DIM // 4
    freqs = ROPE_THETA ** (-jnp.arange(nf, dtype=F32) / nf)
    ang = jnp.concatenate([row[:, None] * freqs, col[:, None] * freqs], axis=-1)
    cos2 = jnp.tile(jnp.cos(ang), (1, 4))
    sin2 = jnp.tile(jnp.concatenate([-jnp.sin(ang), jnp.sin(ang)], axis=-1), (1, 2))

    tq_a = 128
    bias_a = _band_bias(rb[:, :A_HEADS], 1, A_WIN, tq_a, tq_a + 2 * A_WIN, A_WIN)
    tq_b, tk_b = 256, 512
    bias_b = _diff_bias(rb[:, A_HEADS:A_HEADS + B_HEADS], tq_b, tk_b)
    tq_d = 128
    bias_d = []
    for g, (win, r) in enumerate(D_PAIRS):
        lo = A_HEADS + B_HEADS + g * D_HEADS_PER_GROUP
        wd = min(tq_d + 128, seq // r)
        bias_d.append(_band_bias(rb[:, lo:lo + D_HEADS_PER_GROUP], r, win // (2 * r), tq_d, wd, 64))

    flag = jnp.asarray(_norm_flags())
    blk = np.arange(NORM_W) // HEAD_DIM
    bd = jnp.asarray((blk[:, None] == blk[None, :]).astype(np.float32), dtype=BF16)

    xf = x.reshape(n, D_MODEL)
    for l in range(depth):
        h = _prenorm(xf, norm_gain[l].reshape(1, D_MODEL))
        w = w_in[l].astype(BF16)
        gain = _gain_cols(qk_gain[l])
        norm_args = dict(gain=gain, flag=flag, bd=bd)
        p1 = _proj_call(h, w, kind=K_NORM, col0=0, step=1, nj=OFF_BV // 256, tn=256, seq=seq, **norm_args)
        pbv = _proj_call(h, w, kind=K_RAW, col0=OFF_BV, step=1, nj=2, tn=256, seq=seq)
        pc = _proj_call(h, w, kind=K_ROPE, col0=OFF_CQ, step=1, nj=3, tn=256, seq=seq, cos=cos2, sin=sin2,
                        **norm_args)
        gate = _proj_call(h, w, kind=K_SILU, col0=OFF_GATE, step=1, nj=4, tn=512, seq=seq)
        merge = _proj_call(h, w, kind=K_SIGMOID, col0=OFF_MERGE, step=1, nj=8, tn=512, seq=seq)
        p1 = p1.reshape(bsz, seq, -1)
        oa = _window_attention(p1, bias_a, sink[l], tq=tq_a)
        lam_init = 0.8 - 0.6 * math.exp(-0.3 * l)
        ob = _diff_attention(p1, pbv.reshape(bsz, seq, -1), bias_b, lambda_vec[l], sub_norm_gain[l], lam_init,
                             tq=tq_b, tk=tk_b)
        oc = _dense_gqa(pc.reshape(bsz, seq, -1))
        d_outs = []
        for g, (win, r) in enumerate(D_PAIRS):
            dqk = _proj_call(h, w, kind=K_NORM, col0=OFF_DQ + g * 512, step=(OFF_DK - OFF_DQ) // 512, nj=2,
                             tn=512, seq=seq, r=r, **norm_args)
            dv = _proj_call(h, w, kind=K_RAW, col0=OFF_DV + g * 512, step=1, nj=1, tn=512, seq=seq, r=r)
            if r == 1:
                dqk, dv = dqk.reshape(bsz, 1, seq, -1), dv.reshape(bsz, 1, seq, -1)
            d_outs += _dilated_group(dqk, dv, bias_d[g], tq=tq_d)
        xf = _merge(xf, oa.reshape(n, -1), ob.reshape(n, -1), oc.reshape(n, -1), d_outs, gate, merge,
                    w_branch[l].astype(BF16), w_out[l].astype(BF16), seq)
    return xf.reshape(bsz, seq, D_MODEL)
```

```python
import functools
import math

import numpy as np
import jax
import jax.numpy as jnp
from jax import lax
from jax.experimental import pallas as pl
from jax.experimental.pallas import tpu as pltpu

F32 = jnp.float32
BF16 = jnp.bfloat16

D_MODEL = 1024
HEAD_DIM = 64
GRID_W = 64
EPS = 1e-6
A_HEADS, A_KV, A_WIN = 8, 2, 128
B_HEADS = 4
C_HEADS, C_KV = 8, 2
ROPE_THETA = 10000.0
D_PAIRS = ((128, 1), (512, 4), (2048, 16))
D_HEADS_PER_GROUP = 8
N_BRANCH = 4
BRANCH_W = 512
REL_BUCKETS = 32
REL_MAX_DIST = 1024
IN_WIDTH = 13824
NEG = -1e30
LOG2E = math.log2(math.e)
LANES = 128

OFF_AQ, OFF_AK, OFF_AV = 0, 512, 640
OFF_BQ, OFF_BK, OFF_BV = 768, 1280, 1792
OFF_CQ, OFF_CK, OFF_CV = 2304, 2816, 2944
OFF_DQ, OFF_DK, OFF_DV = 3072, 4608, 6144
OFF_GATE, OFF_MERGE = 7680, 9728

VMEM_LIMIT = 56 * 1024 * 1024

K_RAW, K_NORM, K_ROPE, K_SILU, K_SIGMOID = range(5)


def _norm_flags():
    norm = np.zeros((IN_WIDTH,), np.float32)
    for lo, hi in ((OFF_AQ, OFF_AV), (OFF_BQ, OFF_BV), (OFF_CQ, OFF_CV), (OFF_DQ, OFF_DV)):
        norm[lo:hi] = 1.0
    return norm.reshape(1, IN_WIDTH)


def _gain_cols(g):
    sc = HEAD_DIM ** -0.5
    sc2 = sc * LOG2E
    one = lambda n: jnp.ones((n,), F32)
    parts = [jnp.tile(g[0, 0], A_HEADS) * sc, jnp.tile(g[0, 1], A_KV), one(128),
             jnp.tile(g[1, 0], 2 * B_HEADS) * sc2, jnp.tile(g[1, 1], 2 * B_HEADS), one(512),
             jnp.tile(g[2, 0], C_HEADS) * sc2, jnp.tile(g[2, 1], C_KV), one(128),
             jnp.tile(g[3, 0], 24) * sc, jnp.tile(g[3, 1], 24), one(1536),
             one(IN_WIDTH - OFF_GATE)]
    return jnp.concatenate(parts).reshape(1, IN_WIDTH)


def _prenorm_kernel(x_ref, g_ref, o_ref):
    x = x_ref[...]
    ms = jnp.mean(x * x, axis=-1, keepdims=True)
    o_ref[...] = (x * lax.rsqrt(ms + EPS) * g_ref[...]).astype(o_ref.dtype)


def _prenorm(x, g, tm=1024):
    n = x.shape[0]
    return pl.pallas_call(
        _prenorm_kernel,
        grid=(n // tm,),
        in_specs=[pl.BlockSpec((tm, D_MODEL), lambda i: (i, 0)),
                  pl.BlockSpec((1, D_MODEL), lambda i: (0, 0))],
        out_specs=pl.BlockSpec((tm, D_MODEL), lambda i: (i, 0)),
        out_shape=jax.ShapeDtypeStruct((n, D_MODEL), BF16),
        compiler_params=pltpu.CompilerParams(dimension_semantics=("parallel",), vmem_limit_bytes=VMEM_LIMIT),
    )(x, g)


PROJ_CHUNK = 256
NORM_W = 256


def _proj_kernel(*refs, kind, r, tm, tn):
    refs = list(refs)
    h_ref, w_ref = refs[:2]
    pos = 2
    if kind in (K_NORM, K_ROPE):
        gain_ref, flag_ref, bd_ref = refs[pos:pos + 3]; pos += 3
    if kind == K_ROPE:
        cos_ref, sin_ref = refs[pos:pos + 2]; pos += 2
    o_ref = refs[pos]; pos += 1
    slab_ref = refs[pos] if r > 1 else None

    for rc in range(tm // PROJ_CHUNK):
        rows = slice(rc * PROJ_CHUNK, (rc + 1) * PROJ_CHUNK)
        acc = jnp.dot(h_ref[rows, :], w_ref[...], preferred_element_type=F32)
        if kind in (K_NORM, K_ROPE):
            sq = acc * acc
            hi = sq.astype(BF16)
            lo = (sq - hi.astype(F32)).astype(BF16)
            bd = bd_ref[...]
            ms = jnp.concatenate(
                [jnp.dot(hi[:, c:c + NORM_W], bd, preferred_element_type=F32)
                 + jnp.dot(lo[:, c:c + NORM_W], bd, preferred_element_type=F32) for c in range(0, tn, NORM_W)],
                axis=1) * (1.0 / HEAD_DIM)
            flag = flag_ref[...] > 0.0
            y = acc * jnp.where(flag, lax.rsqrt(ms + EPS), 1.0) * gain_ref[...]
            if kind == K_ROPE:
                c = jnp.concatenate([cos_ref[rows, :]] * (tn // LANES), axis=1)
                s = jnp.concatenate([sin_ref[rows, :]] * (tn // LANES), axis=1)
                lane = lax.broadcasted_iota(jnp.int32, y.shape, 1) & (HEAD_DIM - 1)
                half = HEAD_DIM // 2
                partner = jnp.where(lane < half, pltpu.roll(y, tn - half, axis=1), pltpu.roll(y, half, axis=1))
                y = jnp.where(flag, y * c + partner * s, y)
        elif kind == K_SILU:
            y = acc * (0.5 * jnp.tanh(0.5 * acc) + 0.5)
        elif kind == K_SIGMOID:
            y = 0.5 * jnp.tanh(0.5 * acc) + 0.5
        else:
            y = acc
        if r == 1:
            o_ref[rows, :] = y.astype(o_ref.dtype)
        else:
            for s_ in range(tn // LANES):
                slab_ref[s_, rows, :] = y[:, s_ * LANES:(s_ + 1) * LANES]
    if r > 1:
        for c in range(r):
            for s_ in range(tn // LANES):
                o_ref[c, :, s_ * LANES:(s_ + 1) * LANES] = (
                    slab_ref[s_, pl.ds(c, tm // r, stride=r), :].astype(o_ref.dtype))


def _proj_call(h, w, *, kind, col0, step, nj, tn, seq, r=1, gain=None, flag=None, bd=None, cos=None, sin=None,
               tm=1024):
    n = h.shape[0]
    ns = seq // tm
    cb = col0 // tn
    in_specs = [pl.BlockSpec((tm, D_MODEL), lambda i, j: (i, 0)),
                pl.BlockSpec((D_MODEL, tn), lambda i, j: (0, cb + j * step))]
    args = [h, w]
    if kind in (K_NORM, K_ROPE):
        in_specs += [pl.BlockSpec((1, tn), lambda i, j: (0, cb + j * step)),
                     pl.BlockSpec((1, tn), lambda i, j: (0, cb + j * step)),
                     pl.BlockSpec((NORM_W, NORM_W), lambda i, j: (0, 0))]
        args += [gain, flag, bd]
    if kind == K_ROPE:
        in_specs += [pl.BlockSpec((tm, LANES), lambda i, j: (i % ns, 0))] * 2
        args += [cos, sin]
    scratch = []
    if r == 1:
        out_shape = jax.ShapeDtypeStruct((n, nj * tn), BF16)
        out_spec = pl.BlockSpec((tm, tn), lambda i, j: (i, j))
    else:
        out_shape = jax.ShapeDtypeStruct((n // seq, r, seq // r, nj * tn), BF16)
        out_spec = pl.BlockSpec((None, r, tm // r, tn), lambda i, j: (i // ns, 0, i % ns, j))
        scratch = [pltpu.VMEM((tn // LANES, tm, LANES), F32)]
    return pl.pallas_call(
        functools.partial(_proj_kernel, kind=kind, r=r, tm=tm, tn=tn),
        grid=(n // tm, nj),
        in_specs=in_specs,
        out_specs=out_spec,
        out_shape=out_shape,
        scratch_shapes=scratch,
        compiler_params=pltpu.CompilerParams(
            dimension_semantics=("parallel", "parallel"), vmem_limit_bytes=VMEM_LIMIT),
    )(*args)


def _bucket(rel):
    nb = REL_BUCKETS // 2
    exact = nb // 2
    n = jnp.abs(rel)
    large = exact + (jnp.log(jnp.maximum(n, exact).astype(F32) / exact)
                     / math.log(REL_MAX_DIST / exact) * (nb - exact)).astype(jnp.int32)
    large = jnp.minimum(large, nb - 1)
    return jnp.where(rel > 0, nb, 0) + jnp.where(n < exact, n, large)


def _lookup(tbl, bucket):
    shape = (tbl.shape[1],) + (1,) * bucket.ndim
    out = jnp.zeros((tbl.shape[1],) + bucket.shape, F32)
    for j in range(REL_BUCKETS):
        out = jnp.where(bucket[None] == j, tbl[j].reshape(shape), out)
    return out


def _band_bias(tbl, stride, half, tq, w, pad):
    tiles = []
    for off in (0, pad, w - tq):
        rel = jnp.arange(w)[None, :] - off - jnp.arange(tq)[:, None]
        b = _lookup(tbl, _bucket(rel * stride))
        tiles.append(jnp.where((jnp.abs(rel) <= half)[None], b, NEG))
    return jnp.stack(tiles, axis=0)


def _band_kernel(*refs, qi_axis, nh, kv_of, tq, w, pad, m_len, has_sink, lse_out):
    refs = list(refs)
    q_ref, k_ref, v_ref, bias_ref = refs[:4]
    pos = 4
    sink_ref = lse_ref = None
    if has_sink:
        sink_ref = refs[pos]; pos += 1
    o_ref = refs[pos]; pos += 1
    if lse_out:
        lse_ref = refs[pos]

    qi = pl.program_id(qi_axis)
    ks = pl.multiple_of(jnp.clip(qi * tq - pad, 0, m_len - w), 64)
    dn = (((1,), (1,)), ((), ()))
    for h in range(nh):
        kv = kv_of(h)
        hs = slice(h * HEAD_DIM, (h + 1) * HEAD_DIM)
        q = q_ref[:, hs]
        k = k_ref[pl.ds(ks, w), kv * HEAD_DIM:(kv + 1) * HEAD_DIM]
        v = v_ref[pl.ds(ks, w), kv * HEAD_DIM:(kv + 1) * HEAD_DIM]
        s = lax.dot_general(q, k, dn, preferred_element_type=F32) + bias_ref[h]
        m = jnp.max(s, axis=-1, keepdims=True)
        if has_sink:
            sk = sink_ref[:, h:h + 1]
            m = jnp.maximum(m, sk)
        e = jnp.exp(s - m)
        l = jnp.sum(e, axis=-1, keepdims=True)
        if has_sink:
            l = l + jnp.exp(sk - m)
        o = jnp.dot(e.astype(BF16), v, preferred_element_type=F32) / l
        if lse_out:
            lse_ref[:, hs] = jnp.broadcast_to(m + jnp.log(l), o.shape)
        o_ref[:, hs] = o.astype(o_ref.dtype)


def _window_attention(p1, bias, sink, tq=128):
    bsz, seq, _ = p1.shape
    pad = A_WIN
    w = tq + 2 * pad
    nq = seq // tq
    case = lambda qi: jnp.where(qi == 0, 0, jnp.where(qi == nq - 1, 2, 1))
    kern = functools.partial(_band_kernel, qi_axis=1, nh=A_HEADS, kv_of=lambda h: h // (A_HEADS // A_KV),
                             tq=tq, w=w, pad=pad, m_len=seq, has_sink=True, lse_out=False)
    return pl.pallas_call(
        kern,
        grid=(bsz, nq),
        in_specs=[
            pl.BlockSpec((None, tq, 512), lambda b, qi: (b, qi, OFF_AQ // 512)),
            pl.BlockSpec((None, seq, 128), lambda b, qi: (b, 0, OFF_AK // 128)),
            pl.BlockSpec((None, seq, 128), lambda b, qi: (b, 0, OFF_AV // 128)),
            pl.BlockSpec((None, A_HEADS, tq, w), lambda b, qi: (case(qi), 0, 0, 0)),
            pl.BlockSpec((1, A_HEADS), lambda b, qi: (0, 0)),
        ],
        out_specs=pl.BlockSpec((None, tq, BRANCH_W), lambda b, qi: (b, qi, 0)),
        out_shape=jax.ShapeDtypeStruct((bsz, seq, BRANCH_W), BF16),
        compiler_params=pltpu.CompilerParams(
            dimension_semantics=("parallel", "arbitrary"), vmem_limit_bytes=VMEM_LIMIT),
    )(p1, p1, p1, bias, sink.reshape(1, A_HEADS))


def _dilated_group(qk, v, bias, tq=128):
    bsz, r, m_len, _ = v.shape
    pad = 64
    w = min(tq + 2 * pad, m_len)
    nq = m_len // tq
    case = lambda qi: jnp.where(qi == 0, 0, jnp.where(qi == nq - 1, 2, 1))
    kern = functools.partial(_band_kernel, qi_axis=2, nh=D_HEADS_PER_GROUP, kv_of=lambda h: h, tq=tq, w=w,
                             pad=pad, m_len=m_len, has_sink=False, lse_out=True)
    row_spec = pl.BlockSpec((None, None, tq, BRANCH_W), lambda b, c, qi: (b, c, qi, 0))
    return pl.pallas_call(
        kern,
        grid=(bsz, r, nq),
        in_specs=[
            row_spec,
            pl.BlockSpec((None, None, m_len, 512), lambda b, c, qi: (b, c, 0, 1)),
            pl.BlockSpec((None, None, m_len, 512), lambda b, c, qi: (b, c, 0, 0)),
            pl.BlockSpec((None, D_HEADS_PER_GROUP, tq, w), lambda b, c, qi: (case(qi), 0, 0, 0)),
        ],
        out_specs=[row_spec, row_spec],
        out_shape=[jax.ShapeDtypeStruct((bsz, r, m_len, BRANCH_W), BF16),
                   jax.ShapeDtypeStruct((bsz, r, m_len, BRANCH_W), F32)],
        compiler_params=pltpu.CompilerParams(
            dimension_semantics=("parallel", "parallel", "arbitrary"), vmem_limit_bytes=VMEM_LIMIT),
    )(qk, qk, v, bias)


DIFF_UNIT = 512


def _saturation_distance():
    nb = REL_BUCKETS // 2
    exact = nb // 2
    n = np.arange(exact, 4 * REL_MAX_DIST)
    large = exact + (np.log(n.astype(np.float32) / exact) / math.log(REL_MAX_DIST / exact)
                     * (nb - exact)).astype(np.int32)
    return int(n[large < nb - 1].max()) + 1 + 2


def _ceil_div(a, b):
    return -(-a // b)


def _diff_tiles(tq, tk):
    far = _saturation_distance()
    return -_ceil_div(far + tk - 1, DIFF_UNIT), _ceil_div(far + tq - 1, DIFF_UNIT)


ONES_ROWS = 16


def _diff_kernel(lv_ref, sg_ref, q_ref, k_ref, v_ref, bias_ref, o_ref, *, tq, tk, nk, lo, hi, lam_init):
    qi = pl.program_id(2)
    vd = 2 * HEAD_DIM
    qt = q_ref[...].T
    qts = (qt[:HEAD_DIM], qt[HEAD_DIM:])
    m0 = jnp.full((1, tq), NEG, F32)
    a0 = jnp.zeros((vd + ONES_ROWS, tq), F32)
    ones = jnp.ones((ONES_ROWS, tk), BF16)

    def body(ki, carry):
        k0 = pl.multiple_of(ki * tk, tk)
        k = k_ref[pl.ds(k0, tk), :]
        vt = jnp.concatenate([v_ref[pl.ds(k0, tk), :].T, ones], axis=0)
        dd = jnp.clip(ki * (tk // DIFF_UNIT) - qi * (tq // DIFF_UNIT), lo, hi) - lo
        bias = bias_ref[dd]
        out = []
        for c in range(2):
            m, acc = carry[c]
            st = jnp.dot(k[:, c * HEAD_DIM:(c + 1) * HEAD_DIM], qts[c], preferred_element_type=F32) + bias
            m_new = jnp.maximum(m, jnp.max(st, axis=0, keepdims=True))
            pt = jnp.exp2(st - m_new).astype(BF16)
            acc = jnp.exp2(m - m_new) * acc + jnp.dot(vt, pt, preferred_element_type=F32)
            out.append((m_new, acc))
        return tuple(out)

    (_, a1), (_, a2) = lax.fori_loop(0, nk, body, ((m0, a0), (m0, a0)))
    lv = lv_ref[...]
    lam = (jnp.exp(jnp.sum(lv[0:1] * lv[1:2], axis=-1, keepdims=True))
           - jnp.exp(jnp.sum(lv[2:3] * lv[3:4], axis=-1, keepdims=True)) + lam_init)
    ot = a1[:vd] / a1[vd:vd + 1] - lam * (a2[:vd] / a2[vd:vd + 1])
    ms = jnp.mean(ot * ot, axis=0, keepdims=True)
    ot = ot * lax.rsqrt(ms + EPS) * sg_ref[...] * (1.0 - lam_init)
    o_ref[...] = ot.T.astype(o_ref.dtype)


def _diff_bias(tbl, tq, tk):
    lo, hi = _diff_tiles(tq, tk)
    tiles = []
    for d in range(lo, hi + 1):
        rel = d * DIFF_UNIT + jnp.arange(tk)[:, None] - jnp.arange(tq)[None, :]
        tiles.append(_lookup(tbl, _bucket(rel)) * LOG2E)
    return jnp.stack(tiles, axis=1)


def _diff_attention(p1, pv, bias, lam_vec, sub_gain, lam_init, tq, tk):
    bsz, seq, _ = p1.shape
    lo, hi = _diff_tiles(tq, tk)
    nt = hi - lo + 1
    vd = 2 * HEAD_DIM
    kern = functools.partial(_diff_kernel, tq=tq, tk=tk, nk=seq // tk, lo=lo, hi=hi, lam_init=lam_init)
    return pl.pallas_call(
        kern,
        grid=(B_HEADS, bsz, seq // tq),
        in_specs=[
            pl.BlockSpec((4, HEAD_DIM), lambda h, b, qi: (0, 0)),
            pl.BlockSpec((vd, 1), lambda h, b, qi: (0, 0)),
            pl.BlockSpec((None, tq, vd), lambda h, b, qi: (b, qi, OFF_BQ // vd + h)),
            pl.BlockSpec((None, seq, vd), lambda h, b, qi: (b, 0, OFF_BK // vd + h)),
            pl.BlockSpec((None, seq, vd), lambda h, b, qi: (b, 0, h)),
            pl.BlockSpec((None, nt, tk, tq), lambda h, b, qi: (h, 0, 0, 0), pipeline_mode=pl.Buffered(1)),
        ],
        out_specs=pl.BlockSpec((None, tq, vd), lambda h, b, qi: (b, qi, h)),
        out_shape=jax.ShapeDtypeStruct((bsz, seq, BRANCH_W), BF16),
        compiler_params=pltpu.CompilerParams(
            dimension_semantics=("parallel", "parallel", "arbitrary"), vmem_limit_bytes=VMEM_LIMIT),
    )(lam_vec, sub_gain.reshape(vd, 1), p1, p1, pv, bias)


def _gqa_kernel(q0_ref, q1_ref, k_ref, v_ref, o_ref, *, tq, tk, nk):
    grp = C_HEADS // C_KV
    qts = []
    for q_ref in (q0_ref, q1_ref):
        qt = q_ref[...].T
        qts.append(jnp.concatenate([qt[g * HEAD_DIM:(g + 1) * HEAD_DIM] for g in range(grp)], axis=1))
    m0 = jnp.full((1, grp * tq), NEG, F32)
    a0 = jnp.zeros((HEAD_DIM + ONES_ROWS, grp * tq), F32)
    ones = jnp.ones((ONES_ROWS, tk), BF16)

    def body(ki, carry):
        k0 = pl.multiple_of(ki * tk, tk)
        out = []
        for kv in range(C_KV):
            m, acc = carry[kv]
            k = k_ref[pl.ds(k0, tk), kv * HEAD_DIM:(kv + 1) * HEAD_DIM]
            v = v_ref[pl.ds(k0, tk), kv * HEAD_DIM:(kv + 1) * HEAD_DIM]
            vt = jnp.concatenate([v.T, ones], axis=0)
            st = jnp.dot(k, qts[kv], preferred_element_type=F32)
            m_new = jnp.maximum(m, jnp.max(st, axis=0, keepdims=True))
            pt = jnp.exp2(st - m_new).astype(BF16)
            acc = jnp.exp2(m - m_new) * acc + jnp.dot(vt, pt, preferred_element_type=F32)
            out.append((m_new, acc))
        return tuple(out)

    res = lax.fori_loop(0, nk, body, ((m0, a0), (m0, a0)))
    for kv in range(C_KV):
        acc = res[kv][1]
        ot = acc[:HEAD_DIM] / acc[HEAD_DIM:HEAD_DIM + 1]
        for g in range(grp):
            h = kv * grp + g
            o_ref[:, h * HEAD_DIM:(h + 1) * HEAD_DIM] = ot[:, g * tq:(g + 1) * tq].T.astype(o_ref.dtype)


def _dense_gqa(pc, tq=512, tk=512):
    bsz, seq, _ = pc.shape
    kern = functools.partial(_gqa_kernel, tq=tq, tk=tk, nk=seq // tk)
    return pl.pallas_call(
        kern,
        grid=(bsz, seq // tq),
        in_specs=[
            pl.BlockSpec((None, tq, 256), lambda b, qi: (b, qi, 0)),
            pl.BlockSpec((None, tq, 256), lambda b, qi: (b, qi, 1)),
            pl.BlockSpec((None, seq, 128), lambda b, qi: (b, 0, 4)),
            pl.BlockSpec((None, seq, 128), lambda b, qi: (b, 0, 5)),
        ],
        out_specs=pl.BlockSpec((None, tq, BRANCH_W), lambda b, qi: (b, qi, 0)),
        out_shape=jax.ShapeDtypeStruct((bsz, seq, BRANCH_W), BF16),
        compiler_params=pltpu.CompilerParams(
            dimension_semantics=("parallel", "arbitrary"), vmem_limit_bytes=VMEM_LIMIT),
    )(pc, pc, pc, pc)


def _merge_kernel(*refs, tm):
    x_ref, oa_ref, ob_ref, oc_ref = refs[:4]
    d_refs = refs[4:10]
    gate_ref, merge_ref, wb_ref, wo_ref, out_ref, so_ref, sl_ref = refs[10:17]
    nslab = BRANCH_W // LANES

    def natural(ref, scr, r):
        if r == 1:
            return ref[0].astype(F32)
        for c in range(r):
            blk = ref[c].astype(F32)
            for s in range(nslab):
                scr[s, pl.ds(c, tm // r, stride=r), :] = blk[:, s * LANES:(s + 1) * LANES]
        return jnp.concatenate([scr[s] for s in range(nslab)], axis=1)

    os_, ls_ = [], []
    for g, (_, r) in enumerate(D_PAIRS):
        os_.append(natural(d_refs[2 * g], so_ref, r))
        ls_.append(natural(d_refs[2 * g + 1], sl_ref, r))
    mx = jnp.maximum(jnp.maximum(ls_[0], ls_[1]), ls_[2])
    es = [jnp.exp(l - mx) for l in ls_]
    od = (es[0] * os_[0] + es[1] * os_[1] + es[2] * os_[2]) / (es[0] + es[1] + es[2])

    branches = [oa_ref[...].astype(F32), ob_ref[...].astype(F32), oc_ref[...].astype(F32), od]
    gated = [(branches[n] * gate_ref[:, n * BRANCH_W:(n + 1) * BRANCH_W].astype(F32)).astype(BF16)
             for n in range(N_BRANCH)]
    halves = []
    for half in range(2):
        cols = slice(half * 512, (half + 1) * 512)
        merged = None
        for n in range(N_BRANCH):
            y = jnp.dot(gated[n], wb_ref[n, :, cols], preferred_element_type=F32)
            term = merge_ref[:, n * D_MODEL + half * 512:n * D_MODEL + (half + 1) * 512].astype(F32) * y
            merged = term if merged is None else merged + term
        halves.append(merged.astype(BF16))
    merged = jnp.concatenate(halves, axis=1)
    out_ref[...] = x_ref[...] + jnp.dot(merged, wo_ref[...], preferred_element_type=F32)


def _merge(x, oa, ob, oc, d_outs, gate, merge, wb, wo, seq, tm=256):
    n = x.shape[0]
    ns = seq // tm
    row = pl.BlockSpec((tm, BRANCH_W), lambda i: (i, 0))
    in_specs = [pl.BlockSpec((tm, D_MODEL), lambda i: (i, 0)), row, row, row]
    for _, r in D_PAIRS:
        in_specs += [pl.BlockSpec((None, r, tm // r, BRANCH_W), lambda i: (i // ns, 0, i % ns, 0))] * 2
    in_specs += [pl.BlockSpec((tm, N_BRANCH * BRANCH_W), lambda i: (i, 0)),
                 pl.BlockSpec((tm, N_BRANCH * D_MODEL), lambda i: (i, 0)),
                 pl.BlockSpec((N_BRANCH, BRANCH_W, D_MODEL), lambda i: (0, 0, 0)),
                 pl.BlockSpec((D_MODEL, D_MODEL), lambda i: (0, 0))]
    return pl.pallas_call(
        functools.partial(_merge_kernel, tm=tm),
        grid=(n // tm,),
        in_specs=in_specs,
        out_specs=pl.BlockSpec((tm, D_MODEL), lambda i: (i, 0)),
        out_shape=jax.ShapeDtypeStruct((n, D_MODEL), F32),
        scratch_shapes=[pltpu.VMEM((BRANCH_W // LANES, tm, LANES), F32)] * 2,
        compiler_params=pltpu.CompilerParams(
            dimension_semantics=("parallel",), vmem_limit_bytes=VMEM_LIMIT),
    )(x, oa, ob, oc, *d_outs, gate, merge, wb, wo)


def kernel(x, w_in, w_branch, w_out, norm_gain, qk_gain, sink, lambda_vec, sub_norm_gain, rel_bias):
    bsz, seq, _ = x.shape
    n = bsz * seq
    depth = w_in.shape[0]
    rb = rel_bias.astype(F32)

    rows = seq // GRID_W
    row = jnp.repeat(jnp.arange(rows), GRID_W).astype(F32)
    col = jnp.tile(jnp.arange(GRID_W), rows).astype(F32)
    nf = HEAD_DIM // 4
    freqs = ROPE_THETA ** (-jnp.arange(nf, dtype=F32) / nf)
    ang = jnp.concatenate([row[:, None] * freqs, col[:, None] * freqs], axis=-1)
    cos2 = jnp.tile(jnp.cos(ang), (1, 4))
    sin2 = jnp.tile(jnp.concatenate([-jnp.sin(ang), jnp.sin(ang)], axis=-1), (1, 2))

    tq_a = 128
    bias_a = _band_bias(rb[:, :A_HEADS], 1, A_WIN, tq_a, tq_a + 2 * A_WIN, A_WIN)
    tq_b, tk_b = 1024, 512
    bias_b = _diff_bias(rb[:, A_HEADS:A_HEADS + B_HEADS], tq_b, tk_b)
    tq_d = 128
    bias_d = []
    for g, (win, r) in enumerate(D_PAIRS):
        lo = A_HEADS + B_HEADS + g * D_HEADS_PER_GROUP
        wd = min(tq_d + 128, seq // r)
        bias_d.append(_band_bias(rb[:, lo:lo + D_HEADS_PER_GROUP], r, win // (2 * r), tq_d, wd, 64))

    flag = jnp.asarray(_norm_flags())
    blk = np.arange(NORM_W) // HEAD_DIM
    bd = jnp.asarray((blk[:, None] == blk[None, :]).astype(np.float32), dtype=BF16)

    xf = x.reshape(n, D_MODEL)
    for l in range(depth):
        h = _prenorm(xf, norm_gain[l].reshape(1, D_MODEL))
        w = w_in[l].astype(BF16)
        gain = _gain_cols(qk_gain[l])
        norm_args = dict(gain=gain, flag=flag, bd=bd)
        p1 = _proj_call(h, w, kind=K_NORM, col0=0, step=1, nj=OFF_BV // 256, tn=256, seq=seq, **norm_args)
        pbv = _proj_call(h, w, kind=K_RAW, col0=OFF_BV, step=1, nj=2, tn=256, seq=seq)
        pc = _proj_call(h, w, kind=K_ROPE, col0=OFF_CQ, step=1, nj=3, tn=256, seq=seq, cos=cos2, sin=sin2,
                        **norm_args)
        gate = _proj_call(h, w, kind=K_SILU, col0=OFF_GATE, step=1, nj=4, tn=512, seq=seq)
        merge = _proj_call(h, w, kind=K_SIGMOID, col0=OFF_MERGE, step=1, nj=8, tn=512, seq=seq)
        p1 = p1.reshape(bsz, seq, -1)
        oa = _window_attention(p1, bias_a, sink[l], tq=tq_a)
        lam_init = 0.8 - 0.6 * math.exp(-0.3 * l)
        ob = _diff_attention(p1, pbv.reshape(bsz, seq, -1), bias_b, lambda_vec[l], sub_norm_gain[l], lam_init,
                             tq=tq_b, tk=tk_b)
        oc = _dense_gqa(pc.reshape(bsz, seq, -1))
        d_outs = []
        for g, (win, r) in enumerate(D_PAIRS):
            dqk = _proj_call(h, w, kind=K_NORM, col0=OFF_DQ + g * 512, step=(OFF_DK - OFF_DQ) // 512, nj=2,
                             tn=512, seq=seq, r=r, **norm_args)
            dv = _proj_call(h, w, kind=K_RAW, col0=OFF_DV + g * 512, step=1, nj=1, tn=512, seq=seq, r=r)
            if r == 1:
                dqk, dv = dqk.reshape(bsz, 1, seq, -1), dv.reshape(bsz, 1, seq, -1)
            d_outs += _dilated_group(dqk, dv, bias_d[g], tq=tq_d)
        xf = _merge(xf, oa.reshape(n, -1), ob.reshape(n, -1), oc.reshape(n, -1), d_outs, gate, merge,
                    w_branch[l].astype(BF16), w_out[l].astype(BF16), seq)
    return xf.reshape(bsz, seq, D_MODEL)
```

```python
import functools
import math

import numpy as np
import jax
import jax.numpy as jnp
from jax import lax
from jax.experimental import pallas as pl
from jax.experimental.pallas import tpu as pltpu

F32 = jnp.float32
BF16 = jnp.bfloat16

D_MODEL = 1024
HEAD_DIM = 64
GRID_W = 64
EPS = 1e-6
A_HEADS, A_KV, A_WIN = 8, 2, 128
B_HEADS = 4
C_HEADS, C_KV = 8, 2
ROPE_THETA = 10000.0
D_PAIRS = ((128, 1), (512, 4), (2048, 16))
D_HEADS_PER_GROUP = 8
N_BRANCH = 4
BRANCH_W = 512
REL_BUCKETS = 32
REL_MAX_DIST = 1024
IN_WIDTH = 13824
NEG = -1e30
LOG2E = math.log2(math.e)
LANES = 128
ONES_ROWS = 16

OFF_AQ, OFF_AK, OFF_AV = 0, 512, 640
OFF_BQ, OFF_BK, OFF_BV = 768, 1280, 1792
OFF_CQ, OFF_CK, OFF_CV = 2304, 2816, 2944
OFF_DQ, OFF_DK, OFF_DV = 3072, 4608, 6144
OFF_GATE, OFF_MERGE = 7680, 9728
PERM = ((OFF_CQ, 768), (OFF_AK, 256), (OFF_AQ, 512), (OFF_BQ, 1024), (OFF_BV, 512))
P_ROPE, P_NORM, P_BV = 0, 1024, 2560

VMEM_LIMIT = 56 * 1024 * 1024

K_RAW, K_NORM, K_ROPE, K_SILU, K_SIGMOID = range(5)


def _permute_cols(a):
    return jnp.concatenate([a[..., lo:lo + n] for lo, n in PERM] + [a[..., OFF_DQ:]], axis=-1)


def _col_flags():
    norm = np.zeros((IN_WIDTH,), np.float32)
    for lo, hi in ((OFF_AQ, OFF_AV), (OFF_BQ, OFF_BV), (OFF_CQ, OFF_CV), (OFF_DQ, OFF_DV)):
        norm[lo:hi] = 1.0
    rope = np.zeros((IN_WIDTH,), np.float32)
    rope[OFF_CQ:OFF_CV] = 1.0
    return norm.reshape(1, IN_WIDTH), rope.reshape(1, IN_WIDTH)


def _gain_cols(g):
    sc = HEAD_DIM ** -0.5 * LOG2E
    one = lambda n: jnp.ones((n,), F32)
    parts = [jnp.tile(g[0, 0], A_HEADS) * sc, jnp.tile(g[0, 1], A_KV), one(128),
             jnp.tile(g[1, 0], 2 * B_HEADS) * sc, jnp.tile(g[1, 1], 2 * B_HEADS), one(512),
             jnp.tile(g[2, 0], C_HEADS) * sc, jnp.tile(g[2, 1], C_KV), one(128),
             jnp.tile(g[3, 0], 24) * sc, jnp.tile(g[3, 1], 24), one(1536),
             one(IN_WIDTH - OFF_GATE)]
    return jnp.concatenate(parts).reshape(1, IN_WIDTH)


def _prenorm_kernel(x_ref, g_ref, o_ref):
    x = x_ref[...]
    ms = jnp.mean(x * x, axis=-1, keepdims=True)
    o_ref[...] = (x * lax.rsqrt(ms + EPS) * g_ref[...]).astype(o_ref.dtype)


def _prenorm(x, g, tm=1024):
    n = x.shape[0]
    return pl.pallas_call(
        _prenorm_kernel,
        grid=(n // tm,),
        in_specs=[pl.BlockSpec((tm, D_MODEL), lambda i: (i, 0)),
                  pl.BlockSpec((1, D_MODEL), lambda i: (0, 0))],
        out_specs=pl.BlockSpec((tm, D_MODEL), lambda i: (i, 0)),
        out_shape=jax.ShapeDtypeStruct((n, D_MODEL), BF16),
        compiler_params=pltpu.CompilerParams(dimension_semantics=("parallel",), vmem_limit_bytes=VMEM_LIMIT),
    )(x, g)


PROJ_TM = 2048
PROJ_TN = 512
PROJ_CHUNK = 256
NORM_W = 256


def _proj_kernel(*refs, kind, r, tm, tn):
    refs = list(refs)
    h_ref, w_ref = refs[:2]
    pos = 2
    if kind in (K_NORM, K_ROPE):
        gain_ref, flag_ref, bd_ref = refs[pos:pos + 3]; pos += 3
    if kind == K_ROPE:
        rflag_ref, cos_ref, sin_ref = refs[pos:pos + 3]; pos += 3
    o_ref = refs[pos]; pos += 1
    slab_ref = refs[pos] if r > 1 else None

    for rc in range(tm // PROJ_CHUNK):
        rows = slice(rc * PROJ_CHUNK, (rc + 1) * PROJ_CHUNK)
        acc = jnp.dot(h_ref[rows, :], w_ref[...], preferred_element_type=F32)
        if kind in (K_NORM, K_ROPE):
            sq = (acc * acc).astype(BF16)
            bd = bd_ref[...]
            ms = jnp.concatenate([jnp.dot(sq[:, c:c + NORM_W], bd, preferred_element_type=F32)
                                  for c in range(0, tn, NORM_W)], axis=1) * (1.0 / HEAD_DIM)
            y = acc * jnp.where(flag_ref[...] > 0.0, lax.rsqrt(ms + EPS), 1.0) * gain_ref[...]
            if kind == K_ROPE:
                c = jnp.concatenate([cos_ref[rows, :]] * (tn // LANES), axis=1)
                s = jnp.concatenate([sin_ref[rows, :]] * (tn // LANES), axis=1)
                lane = lax.broadcasted_iota(jnp.int32, y.shape, 1) & (HEAD_DIM - 1)
                half = HEAD_DIM // 2
                partner = jnp.where(lane < half, pltpu.roll(y, tn - half, axis=1), pltpu.roll(y, half, axis=1))
                y = jnp.where(rflag_ref[...] > 0.0, y * c + partner * s, y)
        elif kind == K_SILU:
            y = acc * (0.5 * jnp.tanh(0.5 * acc) + 0.5)
        elif kind == K_SIGMOID:
            y = 0.5 * jnp.tanh(0.5 * acc) + 0.5
        else:
            y = acc
        if r == 1:
            o_ref[rows, :] = y.astype(o_ref.dtype)
        else:
            for s_ in range(tn // LANES):
                slab_ref[s_, rows, :] = y[:, s_ * LANES:(s_ + 1) * LANES]
    if r > 1:
        for c in range(r):
            for s_ in range(tn // LANES):
                o_ref[c, :, s_ * LANES:(s_ + 1) * LANES] = (
                    slab_ref[s_, pl.ds(c, tm // r, stride=r), :].astype(o_ref.dtype))


def _proj_call(h, w, *, kind, col0, step, nj, seq, r=1, gain=None, flag=None, rflag=None, bd=None, cos=None,
               sin=None):
    tm, tn = PROJ_TM, PROJ_TN
    n = h.shape[0]
    ns = seq // tm
    cb = col0 // tn
    col_spec = pl.BlockSpec((1, tn), lambda i, j: (0, cb + j * step))
    in_specs = [pl.BlockSpec((tm, D_MODEL), lambda i, j: (i, 0)),
                pl.BlockSpec((D_MODEL, tn), lambda i, j: (0, cb + j * step))]
    args = [h, w]
    if kind in (K_NORM, K_ROPE):
        in_specs += [col_spec, col_spec, pl.BlockSpec((NORM_W, NORM_W), lambda i, j: (0, 0))]
        args += [gain, flag, bd]
    if kind == K_ROPE:
        in_specs += [col_spec] + [pl.BlockSpec((tm, LANES), lambda i, j: (i % ns, 0))] * 2
        args += [rflag, cos, sin]
    scratch = []
    if r == 1:
        out_shape = jax.ShapeDtypeStruct((n, nj * tn), BF16)
        out_spec = pl.BlockSpec((tm, tn), lambda i, j: (i, j))
    else:
        out_shape = jax.ShapeDtypeStruct((n // seq, r, seq // r, nj * tn), BF16)
        out_spec = pl.BlockSpec((None, r, tm // r, tn), lambda i, j: (i // ns, 0, i % ns, j))
        scratch = [pltpu.VMEM((tn // LANES, tm, LANES), F32)]
    return pl.pallas_call(
        functools.partial(_proj_kernel, kind=kind, r=r, tm=tm, tn=tn),
        grid=(n // tm, nj),
        in_specs=in_specs,
        out_specs=out_spec,
        out_shape=out_shape,
        scratch_shapes=scratch,
        compiler_params=pltpu.CompilerParams(
            dimension_semantics=("parallel", "parallel"), vmem_limit_bytes=VMEM_LIMIT),
    )(*args)


def _bucket(rel):
    nb = REL_BUCKETS // 2
    exact = nb // 2
    n = jnp.abs(rel)
    large = exact + (jnp.log(jnp.maximum(n, exact).astype(F32) / exact)
                     / math.log(REL_MAX_DIST / exact) * (nb - exact)).astype(jnp.int32)
    large = jnp.minimum(large, nb - 1)
    return jnp.where(rel > 0, nb, 0) + jnp.where(n < exact, n, large)


def _lookup(tbl, bucket):
    shape = (tbl.shape[1],) + (1,) * bucket.ndim
    out = jnp.zeros((tbl.shape[1],) + bucket.shape, F32)
    for j in range(REL_BUCKETS):
        out = jnp.where(bucket[None] == j, tbl[j].reshape(shape), out)
    return out


def _band_bias(tbl, stride, half, tq, w, pad, hpc):
    nh = tbl.shape[1]
    tiles = []
    for off in (0, pad, w - tq):
        rel = jnp.arange(w)[:, None] - off - jnp.arange(tq)[None, :]
        b = jnp.where((jnp.abs(rel) <= half)[None], _lookup(tbl, _bucket(rel * stride)) * LOG2E, NEG)
        b = b.reshape(nh // hpc, hpc, w, tq).transpose(0, 2, 1, 3).reshape(nh // hpc, w, hpc * tq)
        tiles.append(b)
    return jnp.stack(tiles, axis=0)


BAND_HPC = 4


def _band_kernel(*refs, qi_axis, shared_kv, tq, w, pad, m_len, has_sink, lse_out):
    refs = list(refs)
    q_ref, k_ref, v_ref, bias_ref = refs[:4]
    pos = 4
    sink_ref = lse_ref = None
    if has_sink:
        sink_ref = refs[pos]; pos += 1
    o_ref = refs[pos]; pos += 1
    if lse_out:
        lse_ref = refs[pos]

    hpc = BAND_HPC
    kw = HEAD_DIM if shared_kv else hpc * HEAD_DIM
    qi = pl.program_id(qi_axis)
    ks = pl.multiple_of(jnp.clip(qi * tq - pad, 0, m_len - w), 64)
    qt = q_ref[...].T
    ones = jnp.ones((ONES_ROWS, w), BF16)
    for c in range(q_ref.shape[1] // (hpc * HEAD_DIM)):
        slab = qt[c * hpc * HEAD_DIM:(c + 1) * hpc * HEAD_DIM]
        if shared_kv:
            qc = jnp.concatenate([slab[g * HEAD_DIM:(g + 1) * HEAD_DIM] for g in range(hpc)], axis=1)
        else:
            head = lax.broadcasted_iota(jnp.int32, slab.shape, 0) // HEAD_DIM
            qc = jnp.concatenate([jnp.where(head == g, slab, jnp.zeros_like(slab)) for g in range(hpc)], axis=1)
        k = k_ref[pl.ds(ks, w), c * kw:(c + 1) * kw]
        v = v_ref[pl.ds(ks, w), c * kw:(c + 1) * kw]
        vt = jnp.concatenate([v.T, ones], axis=0)
        st = jnp.dot(k, qc, preferred_element_type=F32) + bias_ref[c]
        m = jnp.max(st, axis=0, keepdims=True)
        if has_sink:
            sk = sink_ref[c]
            m = jnp.maximum(m, sk)
        pt = jnp.exp2(st - m).astype(BF16)
        acc = jnp.dot(vt, pt, preferred_element_type=F32)
        l = acc[kw:kw + 1]
        if has_sink:
            l = l + jnp.exp2(sk - m)
        lse = m + jnp.log2(l)
        for g in range(hpc):
            h = c * hpc + g
            lanes = slice(g * tq, (g + 1) * tq)
            rows = slice(0, HEAD_DIM) if shared_kv else slice(g * HEAD_DIM, (g + 1) * HEAD_DIM)
            ot = acc[rows, lanes] / l[:, lanes]
            o_ref[:, h * HEAD_DIM:(h + 1) * HEAD_DIM] = ot.T.astype(o_ref.dtype)
            if lse_out:
                lse_ref[:, h * HEAD_DIM:(h + 1) * HEAD_DIM] = jnp.broadcast_to(lse[:, lanes], (HEAD_DIM, tq)).T


def _window_attention(pn, pr, bias, sink_rows, tq):
    bsz, seq, _ = pn.shape
    pad = A_WIN
    w = tq + 2 * pad
    nq = seq // tq
    lanes = BAND_HPC * tq
    case = lambda qi: jnp.where(qi == 0, 0, jnp.where(qi == nq - 1, 2, 1))
    kern = functools.partial(_band_kernel, qi_axis=1, shared_kv=True, tq=tq, w=w, pad=pad, m_len=seq,
                             has_sink=True, lse_out=False)
    return pl.pallas_call(
        kern,
        grid=(bsz, nq),
        in_specs=[
            pl.BlockSpec((None, tq, 512), lambda b, qi: (b, qi, 0)),
            pl.BlockSpec((None, seq, 128), lambda b, qi: (b, 0, 6)),
            pl.BlockSpec((None, seq, 128), lambda b, qi: (b, 0, 7)),
            pl.BlockSpec((None, A_KV, w, lanes), lambda b, qi: (case(qi), 0, 0, 0)),
            pl.BlockSpec((A_KV, 1, lanes), lambda b, qi: (0, 0, 0)),
        ],
        out_specs=pl.BlockSpec((None, tq, BRANCH_W), lambda b, qi: (b, qi, 0)),
        out_shape=jax.ShapeDtypeStruct((bsz, seq, BRANCH_W), BF16),
        compiler_params=pltpu.CompilerParams(
            dimension_semantics=("parallel", "arbitrary"), vmem_limit_bytes=VMEM_LIMIT),
    )(pn, pr, pr, bias, sink_rows)


def _dilated_group(qk, v, bias, tq):
    bsz, r, m_len, _ = v.shape
    pad = 64
    w = min(tq + 2 * pad, m_len)
    nq = m_len // tq
    nchain = D_HEADS_PER_GROUP // BAND_HPC
    case = lambda qi: jnp.where(qi == 0, 0, jnp.where(qi == nq - 1, 2, 1))
    kern = functools.partial(_band_kernel, qi_axis=2, shared_kv=False, tq=tq, w=w, pad=pad, m_len=m_len,
                             has_sink=False, lse_out=True)
    row_spec = pl.BlockSpec((None, None, tq, BRANCH_W), lambda b, c, qi: (b, c, qi, 0))
    return pl.pallas_call(
        kern,
        grid=(bsz, r, nq),
        in_specs=[
            row_spec,
            pl.BlockSpec((None, None, m_len, 512), lambda b, c, qi: (b, c, 0, 1)),
            pl.BlockSpec((None, None, m_len, 512), lambda b, c, qi: (b, c, 0, 0)),
            pl.BlockSpec((None, nchain, w, BAND_HPC * tq), lambda b, c, qi: (case(qi), 0, 0, 0)),
        ],
        out_specs=[row_spec, row_spec],
        out_shape=[jax.ShapeDtypeStruct((bsz, r, m_len, BRANCH_W), BF16),
                   jax.ShapeDtypeStruct((bsz, r, m_len, BRANCH_W), F32)],
        compiler_params=pltpu.CompilerParams(
            dimension_semantics=("parallel", "parallel", "arbitrary"), vmem_limit_bytes=VMEM_LIMIT),
    )(qk, qk, v, bias)


DIFF_UNIT = 512
DIFF_SUB = 2


def _saturation_distance():
    nb = REL_BUCKETS // 2
    exact = nb // 2
    n = np.arange(exact, 4 * REL_MAX_DIST)
    large = exact + (np.log(n.astype(np.float32) / exact) / math.log(REL_MAX_DIST / exact)
                     * (nb - exact)).astype(np.int32)
    return int(n[large < nb - 1].max()) + 1 + 2


def _ceil_div(a, b):
    return -(-a // b)


def _diff_tiles(tq, tk):
    far = _saturation_distance()
    return -_ceil_div(far + tk - 1, DIFF_UNIT), _ceil_div(far + tq - 1, DIFF_UNIT)


def _diff_kernel(lv_ref, sg_ref, q_ref, k_ref, v_ref, bias_ref, o_ref, *, tq, tk, nk, lo, hi, lam_init):
    qi = pl.program_id(2)
    vd = 2 * HEAD_DIM
    ts = tq // DIFF_SUB
    qt = q_ref[...].T
    first = lax.broadcasted_iota(jnp.int32, (vd, ts), 0) < HEAD_DIM
    zero = jnp.zeros((vd, ts), BF16)
    qbd = []
    for s in range(DIFF_SUB):
        slab = qt[:, s * ts:(s + 1) * ts]
        qbd.append(jnp.concatenate([jnp.where(first, slab, zero), jnp.where(first, zero, slab)], axis=1))
    m0 = jnp.full((1, 2 * ts), NEG, F32)
    a0 = jnp.zeros((vd + ONES_ROWS, 2 * ts), F32)
    ones = jnp.ones((ONES_ROWS, tk), BF16)

    def body(ki, carry):
        k0 = pl.multiple_of(ki * tk, tk)
        k = k_ref[pl.ds(k0, tk), :]
        vt = jnp.concatenate([v_ref[pl.ds(k0, tk), :].T, ones], axis=0)
        dd = jnp.clip(ki * (tk // DIFF_UNIT) - qi * (tq // DIFF_UNIT), lo, hi) - lo
        out = []
        for s in range(DIFF_SUB):
            m, acc = carry[s]
            b = bias_ref[dd, :, s * ts:(s + 1) * ts]
            st = jnp.dot(k, qbd[s], preferred_element_type=F32)
            st = jnp.concatenate([st[:, :ts] + b, st[:, ts:] + b], axis=1)
            m_new = jnp.maximum(m, jnp.max(st, axis=0, keepdims=True))
            pt = jnp.exp2(st - m_new).astype(BF16)
            acc = jnp.exp2(m - m_new) * acc + jnp.dot(vt, pt, preferred_element_type=F32)
            out.append((m_new, acc))
        return tuple(out)

    res = lax.fori_loop(0, nk, body, tuple((m0, a0) for _ in range(DIFF_SUB)))
    lv = lv_ref[...]
    lam = (jnp.exp(jnp.sum(lv[0:1] * lv[1:2], axis=-1, keepdims=True))
           - jnp.exp(jnp.sum(lv[2:3] * lv[3:4], axis=-1, keepdims=True)) + lam_init)
    for s in range(DIFF_SUB):
        a = res[s][1]
        ot = a[:vd, :ts] / a[vd:vd + 1, :ts] - lam * (a[:vd, ts:] / a[vd:vd + 1, ts:])
        ms = jnp.mean(ot * ot, axis=0, keepdims=True)
        ot = ot * lax.rsqrt(ms + EPS) * sg_ref[...] * (1.0 - lam_init)
        o_ref[s * ts:(s + 1) * ts, :] = ot.T.astype(o_ref.dtype)


def _diff_bias(tbl, tq, tk):
    lo, hi = _diff_tiles(tq, tk)
    tiles = []
    for d in range(lo, hi + 1):
        rel = d * DIFF_UNIT + jnp.arange(tk)[:, None] - jnp.arange(tq)[None, :]
        tiles.append(_lookup(tbl, _bucket(rel)) * LOG2E)
    return jnp.stack(tiles, axis=1)


def _diff_attention(pn, pbv, bias, lam_vec, sub_gain, lam_init, tq, tk):
    bsz, seq, _ = pn.shape
    lo, hi = _diff_tiles(tq, tk)
    nt = hi - lo + 1
    vd = 2 * HEAD_DIM
    kern = functools.partial(_diff_kernel, tq=tq, tk=tk, nk=seq // tk, lo=lo, hi=hi, lam_init=lam_init)
    return pl.pallas_call(
        kern,
        grid=(B_HEADS, bsz, seq // tq),
        in_specs=[
            pl.BlockSpec((4, HEAD_DIM), lambda h, b, qi: (0, 0)),
            pl.BlockSpec((vd, 1), lambda h, b, qi: (0, 0)),
            pl.BlockSpec((None, tq, vd), lambda h, b, qi: (b, qi, 512 // vd + h)),
            pl.BlockSpec((None, seq, vd), lambda h, b, qi: (b, 0, 1024 // vd + h)),
            pl.BlockSpec((None, seq, vd), lambda h, b, qi: (b, 0, h)),
            pl.BlockSpec((None, nt, tk, tq), lambda h, b, qi: (h, 0, 0, 0), pipeline_mode=pl.Buffered(1)),
        ],
        out_specs=pl.BlockSpec((None, tq, vd), lambda h, b, qi: (b, qi, h)),
        out_shape=jax.ShapeDtypeStruct((bsz, seq, BRANCH_W), BF16),
        compiler_params=pltpu.CompilerParams(
            dimension_semantics=("parallel", "parallel", "arbitrary"), vmem_limit_bytes=VMEM_LIMIT),
    )(lam_vec, sub_gain.reshape(vd, 1), pn, pn, pbv, bias)


def _gqa_kernel(q0_ref, q1_ref, k_ref, v_ref, o_ref, *, tq, tk, nk):
    grp = C_HEADS // C_KV
    qts = []
    for q_ref in (q0_ref, q1_ref):
        qt = q_ref[...].T
        qts.append(jnp.concatenate([qt[g * HEAD_DIM:(g + 1) * HEAD_DIM] for g in range(grp)], axis=1))
    m0 = jnp.full((1, grp * tq), NEG, F32)
    a0 = jnp.zeros((HEAD_DIM + ONES_ROWS, grp * tq), F32)
    ones = jnp.ones((ONES_ROWS, tk), BF16)

    def body(ki, carry):
        k0 = pl.multiple_of(ki * tk, tk)
        out = []
        for kv in range(C_KV):
            m, acc = carry[kv]
            k = k_ref[pl.ds(k0, tk), kv * HEAD_DIM:(kv + 1) * HEAD_DIM]
            v = v_ref[pl.ds(k0, tk), kv * HEAD_DIM:(kv + 1) * HEAD_DIM]
            vt = jnp.concatenate([v.T, ones], axis=0)
            st = jnp.dot(k, qts[kv], preferred_element_type=F32)
            m_new = jnp.maximum(m, jnp.max(st, axis=0, keepdims=True))
            pt = jnp.exp2(st - m_new).astype(BF16)
            acc = jnp.exp2(m - m_new) * acc + jnp.dot(vt, pt, preferred_element_type=F32)
            out.append((m_new, acc))
        return tuple(out)

    res = lax.fori_loop(0, nk, body, ((m0, a0), (m0, a0)))
    for kv in range(C_KV):
        acc = res[kv][1]
        ot = acc[:HEAD_DIM] / acc[HEAD_DIM:HEAD_DIM + 1]
        for g in range(grp):
            h = kv * grp + g
            o_ref[:, h * HEAD_DIM:(h + 1) * HEAD_DIM] = ot[:, g * tq:(g + 1) * tq].T.astype(o_ref.dtype)


def _dense_gqa(pr, tq=512, tk=512):
    bsz, seq, _ = pr.shape
    kern = functools.partial(_gqa_kernel, tq=tq, tk=tk, nk=seq // tk)
    return pl.pallas_call(
        kern,
        grid=(bsz, seq // tq),
        in_specs=[
            pl.BlockSpec((None, tq, 256), lambda b, qi: (b, qi, 0)),
            pl.BlockSpec((None, tq, 256), lambda b, qi: (b, qi, 1)),
            pl.BlockSpec((None, seq, 128), lambda b, qi: (b, 0, 4)),
            pl.BlockSpec((None, seq, 128), lambda b, qi: (b, 0, 5)),
        ],
        out_specs=pl.BlockSpec((None, tq, BRANCH_W), lambda b, qi: (b, qi, 0)),
        out_shape=jax.ShapeDtypeStruct((bsz, seq, BRANCH_W), BF16),
        compiler_params=pltpu.CompilerParams(
            dimension_semantics=("parallel", "arbitrary"), vmem_limit_bytes=VMEM_LIMIT),
    )(pr, pr, pr, pr)


def _merge_kernel(*refs, tm):
    x_ref, oa_ref, ob_ref, oc_ref = refs[:4]
    d_refs = refs[4:10]
    gate_ref, merge_ref, wb_ref, wo_ref, out_ref, so_ref, sl_ref = refs[10:17]
    nslab = BRANCH_W // LANES

    def natural(ref, scr, r):
        if r == 1:
            return ref[0].astype(F32)
        for c in range(r):
            blk = ref[c].astype(F32)
            for s in range(nslab):
                scr[s, pl.ds(c, tm // r, stride=r), :] = blk[:, s * LANES:(s + 1) * LANES]
        return jnp.concatenate([scr[s] for s in range(nslab)], axis=1)

    os_, ls_ = [], []
    for g, (_, r) in enumerate(D_PAIRS):
        os_.append(natural(d_refs[2 * g], so_ref, r))
        ls_.append(natural(d_refs[2 * g + 1], sl_ref, r))
    mx = jnp.maximum(jnp.maximum(ls_[0], ls_[1]), ls_[2])
    es = [jnp.exp2(l - mx) for l in ls_]
    od = (es[0] * os_[0] + es[1] * os_[1] + es[2] * os_[2]) / (es[0] + es[1] + es[2])

    branches = [oa_ref[...].astype(F32), ob_ref[...].astype(F32), oc_ref[...].astype(F32), od]
    gated = [(branches[n] * gate_ref[:, n * BRANCH_W:(n + 1) * BRANCH_W].astype(F32)).astype(BF16)
             for n in range(N_BRANCH)]
    halves = []
    for half in range(2):
        cols = slice(half * 512, (half + 1) * 512)
        merged = None
        for n in range(N_BRANCH):
            y = jnp.dot(gated[n], wb_ref[n, :, cols], preferred_element_type=F32)
            term = merge_ref[:, n * D_MODEL + half * 512:n * D_MODEL + (half + 1) * 512].astype(F32) * y
            merged = term if merged is None else merged + term
        halves.append(merged.astype(BF16))
    merged = jnp.concatenate(halves, axis=1)
    out_ref[...] = x_ref[...] + jnp.dot(merged, wo_ref[...], preferred_element_type=F32)


def _merge(x, oa, ob, oc, d_outs, gate, merge, wb, wo, seq, tm=256):
    n = x.shape[0]
    ns = seq // tm
    row = pl.BlockSpec((tm, BRANCH_W), lambda i: (i, 0))
    in_specs = [pl.BlockSpec((tm, D_MODEL), lambda i: (i, 0)), row, row, row]
    for _, r in D_PAIRS:
        in_specs += [pl.BlockSpec((None, r, tm // r, BRANCH_W), lambda i: (i // ns, 0, i % ns, 0))] * 2
    in_specs += [pl.BlockSpec((tm, N_BRANCH * BRANCH_W), lambda i: (i, 0)),
                 pl.BlockSpec((tm, N_BRANCH * D_MODEL), lambda i: (i, 0)),
                 pl.BlockSpec((N_BRANCH, BRANCH_W, D_MODEL), lambda i: (0, 0, 0)),
                 pl.BlockSpec((D_MODEL, D_MODEL), lambda i: (0, 0))]
    return pl.pallas_call(
        functools.partial(_merge_kernel, tm=tm),
        grid=(n // tm,),
        in_specs=in_specs,
        out_specs=pl.BlockSpec((tm, D_MODEL), lambda i: (i, 0)),
        out_shape=jax.ShapeDtypeStruct((n, D_MODEL), F32),
        scratch_shapes=[pltpu.VMEM((BRANCH_W // LANES, tm, LANES), F32)] * 2,
        compiler_params=pltpu.CompilerParams(
            dimension_semantics=("parallel",), vmem_limit_bytes=VMEM_LIMIT),
    )(x, oa, ob, oc, *d_outs, gate, merge, wb, wo)


def kernel(x, w_in, w_branch, w_out, norm_gain, qk_gain, sink, lambda_vec, sub_norm_gain, rel_bias):
    bsz, seq, _ = x.shape
    n = bsz * seq
    depth = w_in.shape[0]
    rb = rel_bias.astype(F32)

    rows = seq // GRID_W
    row = jnp.repeat(jnp.arange(rows), GRID_W).astype(F32)
    col = jnp.tile(jnp.arange(GRID_W), rows).astype(F32)
    nf = HEAD_DIM // 4
    freqs = ROPE_THETA ** (-jnp.arange(nf, dtype=F32) / nf)
    ang = jnp.concatenate([row[:, None] * freqs, col[:, None] * freqs], axis=-1)
    cos2 = jnp.tile(jnp.cos(ang), (1, 4))
    sin2 = jnp.tile(jnp.concatenate([-jnp.sin(ang), jnp.sin(ang)], axis=-1), (1, 2))

    tq_a = 128
    bias_a = _band_bias(rb[:, :A_HEADS], 1, A_WIN, tq_a, tq_a + 2 * A_WIN, A_WIN, BAND_HPC)
    tq_b, tk_b = 1024, 512
    bias_b = _diff_bias(rb[:, A_HEADS:A_HEADS + B_HEADS], tq_b, tk_b)
    tq_d = 128
    bias_d = []
    for g, (win, r) in enumerate(D_PAIRS):
        lo = A_HEADS + B_HEADS + g * D_HEADS_PER_GROUP
        wd = min(tq_d + 128, seq // r)
        bias_d.append(_band_bias(rb[:, lo:lo + D_HEADS_PER_GROUP], r, win // (2 * r), tq_d, wd, 64, BAND_HPC))

    flag_np, rflag_np = _col_flags()
    flag = _permute_cols(jnp.asarray(flag_np))
    rflag = _permute_cols(jnp.asarray(rflag_np))
    blk = np.arange(NORM_W) // HEAD_DIM
    bd = jnp.asarray((blk[:, None] == blk[None, :]).astype(np.float32), dtype=BF16)

    xf = x.reshape(n, D_MODEL)
    for l in range(depth):
        h = _prenorm(xf, norm_gain[l].reshape(1, D_MODEL))
        w = _permute_cols(w_in[l].astype(BF16))
        gain = _permute_cols(_gain_cols(qk_gain[l]))
        norm_args = dict(gain=gain, flag=flag, bd=bd)
        pr = _proj_call(h, w, kind=K_ROPE, col0=P_ROPE, step=1, nj=2, seq=seq, rflag=rflag, cos=cos2, sin=sin2,
                        **norm_args).reshape(bsz, seq, -1)
        pn = _proj_call(h, w, kind=K_NORM, col0=P_NORM, step=1, nj=3, seq=seq, **norm_args).reshape(bsz, seq, -1)
        pbv = _proj_call(h, w, kind=K_RAW, col0=P_BV, step=1, nj=1, seq=seq).reshape(bsz, seq, -1)
        gate = _proj_call(h, w, kind=K_SILU, col0=OFF_GATE, step=1, nj=4, seq=seq)
        merge = _proj_call(h, w, kind=K_SIGMOID, col0=OFF_MERGE, step=1, nj=8, seq=seq)
        sink_rows = jnp.repeat(sink[l].astype(F32) * LOG2E, tq_a).reshape(A_KV, 1, BAND_HPC * tq_a)
        oa = _window_attention(pn, pr, bias_a, sink_rows, tq=tq_a)
        lam_init = 0.8 - 0.6 * math.exp(-0.3 * l)
        ob = _diff_attention(pn, pbv, bias_b, lambda_vec[l], sub_norm_gain[l], lam_init, tq=tq_b, tk=tk_b)
        oc = _dense_gqa(pr)
        d_outs = []
        for g, (win, r) in enumerate(D_PAIRS):
            dqk = _proj_call(h, w, kind=K_NORM, col0=OFF_DQ + g * 512, step=(OFF_DK - OFF_DQ) // 512, nj=2,
                             seq=seq, r=r, **norm_args)
            dv = _proj_call(h, w, kind=K_RAW, col0=OFF_DV + g * 512, step=1, nj=1, seq=seq, r=r)
            if r == 1:
                dqk, dv = dqk.reshape(bsz, 1, seq, -1), dv.reshape(bsz, 1, seq, -1)
            d_outs += _dilated_group(dqk, dv, bias_d[g], tq=tq_d)
        xf = _merge(xf, oa.reshape(n, -1), ob.reshape(n, -1), oc.reshape(n, -1), d_outs, gate, merge,
                    w_branch[l].astype(BF16), w_out[l].astype(BF16), seq)
    return xf.reshape(bsz, seq, D_MODEL)
```

```python
import functools
import math

import numpy as np
import jax
import jax.numpy as jnp
from jax import lax
from jax.experimental import pallas as pl
from jax.experimental.pallas import tpu as pltpu

F32 = jnp.float32
BF16 = jnp.bfloat16

D_MODEL = 1024
HEAD_DIM = 64
GRID_W = 64
EPS = 1e-6
A_HEADS, A_KV, A_WIN = 8, 2, 128
B_HEADS = 4
C_HEADS, C_KV = 8, 2
ROPE_THETA = 10000.0
D_PAIRS = ((128, 1), (512, 4), (2048, 16))
D_HEADS_PER_GROUP = 8
N_BRANCH = 4
BRANCH_W = 512
REL_BUCKETS = 32
REL_MAX_DIST = 1024
IN_WIDTH = 13824
NEG = -1e30
LOG2E = math.log2(math.e)
LANES = 128
ONES_ROWS = 16

OFF_AQ, OFF_AK, OFF_AV = 0, 512, 640
OFF_BQ, OFF_BK, OFF_BV = 768, 1280, 1792
OFF_CQ, OFF_CK, OFF_CV = 2304, 2816, 2944
OFF_DQ, OFF_DK, OFF_DV = 3072, 4608, 6144
OFF_GATE, OFF_MERGE = 7680, 9728
PERM = ((OFF_CQ, 768), (OFF_AK, 256), (OFF_AQ, 512), (OFF_BQ, 1024), (OFF_BV, 512))
P_ROPE, P_NORM, P_BV = 0, 1024, 2560

VMEM_LIMIT = 56 * 1024 * 1024

K_RAW, K_NORM, K_ROPE, K_SILU, K_SIGMOID = range(5)


def _permute_cols(a):
    return jnp.concatenate([a[..., lo:lo + n] for lo, n in PERM] + [a[..., OFF_DQ:]], axis=-1)


def _col_flags():
    norm = np.zeros((IN_WIDTH,), np.float32)
    for lo, hi in ((OFF_AQ, OFF_AV), (OFF_BQ, OFF_BV), (OFF_CQ, OFF_CV), (OFF_DQ, OFF_DV)):
        norm[lo:hi] = 1.0
    rope = np.zeros((IN_WIDTH,), np.float32)
    rope[OFF_CQ:OFF_CV] = 1.0
    return norm.reshape(1, IN_WIDTH), rope.reshape(1, IN_WIDTH)


def _gain_cols(g):
    sc = HEAD_DIM ** -0.5 * LOG2E
    one = lambda n: jnp.ones((n,), F32)
    parts = [jnp.tile(g[0, 0], A_HEADS) * sc, jnp.tile(g[0, 1], A_KV), one(128),
             jnp.tile(g[1, 0], 2 * B_HEADS) * sc, jnp.tile(g[1, 1], 2 * B_HEADS), one(512),
             jnp.tile(g[2, 0], C_HEADS) * sc, jnp.tile(g[2, 1], C_KV), one(128),
             jnp.tile(g[3, 0], 24) * sc, jnp.tile(g[3, 1], 24), one(1536),
             one(IN_WIDTH - OFF_GATE)]
    return jnp.concatenate(parts).reshape(1, IN_WIDTH)


def _prenorm_kernel(x_ref, g_ref, o_ref):
    x = x_ref[...]
    ms = jnp.mean(x * x, axis=-1, keepdims=True)
    o_ref[...] = (x * lax.rsqrt(ms + EPS) * g_ref[...]).astype(o_ref.dtype)


def _prenorm(x, g, tm=1024):
    n = x.shape[0]
    return pl.pallas_call(
        _prenorm_kernel,
        grid=(n // tm,),
        in_specs=[pl.BlockSpec((tm, D_MODEL), lambda i: (i, 0)),
                  pl.BlockSpec((1, D_MODEL), lambda i: (0, 0))],
        out_specs=pl.BlockSpec((tm, D_MODEL), lambda i: (i, 0)),
        out_shape=jax.ShapeDtypeStruct((n, D_MODEL), BF16),
        compiler_params=pltpu.CompilerParams(dimension_semantics=("parallel",), vmem_limit_bytes=VMEM_LIMIT),
    )(x, g)


PROJ_TM = 2048
PROJ_TN = 512
PROJ_CHUNK = 256
NORM_W = 256


def _proj_kernel(*refs, kind, r, tm, tn):
    refs = list(refs)
    h_ref, w_ref = refs[:2]
    pos = 2
    if kind in (K_NORM, K_ROPE):
        gain_ref, flag_ref, bd_ref = refs[pos:pos + 3]; pos += 3
    if kind == K_ROPE:
        rflag_ref, cos_ref, sin_ref = refs[pos:pos + 3]; pos += 3
    o_ref = refs[pos]; pos += 1
    slab_ref = refs[pos] if r > 1 else None

    for rc in range(tm // PROJ_CHUNK):
        rows = slice(rc * PROJ_CHUNK, (rc + 1) * PROJ_CHUNK)
        acc = jnp.dot(h_ref[rows, :], w_ref[...], preferred_element_type=F32)
        if kind in (K_NORM, K_ROPE):
            sq = (acc * acc).astype(BF16)
            bd = bd_ref[...]
            ms = jnp.concatenate([jnp.dot(sq[:, c:c + NORM_W], bd, preferred_element_type=F32)
                                  for c in range(0, tn, NORM_W)], axis=1) * (1.0 / HEAD_DIM)
            y = acc * jnp.where(flag_ref[...] > 0.0, lax.rsqrt(ms + EPS), 1.0) * gain_ref[...]
            if kind == K_ROPE:
                c = jnp.concatenate([cos_ref[rows, :]] * (tn // LANES), axis=1)
                s = jnp.concatenate([sin_ref[rows, :]] * (tn // LANES), axis=1)
                lane = lax.broadcasted_iota(jnp.int32, y.shape, 1) & (HEAD_DIM - 1)
                half = HEAD_DIM // 2
                partner = jnp.where(lane < half, pltpu.roll(y, tn - half, axis=1), pltpu.roll(y, half, axis=1))
                y = jnp.where(rflag_ref[...] > 0.0, y * c + partner * s, y)
        elif kind == K_SILU:
            y = acc * (0.5 * jnp.tanh(0.5 * acc) + 0.5)
        elif kind == K_SIGMOID:
            y = 0.5 * jnp.tanh(0.5 * acc) + 0.5
        else:
            y = acc
        if r == 1:
            o_ref[rows, :] = y.astype(o_ref.dtype)
        else:
            for s_ in range(tn // LANES):
                slab_ref[s_, rows, :] = y[:, s_ * LANES:(s_ + 1) * LANES]
    if r > 1:
        for c in range(r):
            for s_ in range(tn // LANES):
                o_ref[c, :, s_ * LANES:(s_ + 1) * LANES] = (
                    slab_ref[s_, pl.ds(c, tm // r, stride=r), :].astype(o_ref.dtype))


def _proj_call(h, w, *, kind, col0, step, nj, seq, r=1, gain=None, flag=None, rflag=None, bd=None, cos=None,
               sin=None):
    tm, tn = PROJ_TM, PROJ_TN
    n = h.shape[0]
    ns = seq // tm
    cb = col0 // tn
    col_spec = pl.BlockSpec((1, tn), lambda i, j: (0, cb + j * step))
    in_specs = [pl.BlockSpec((tm, D_MODEL), lambda i, j: (i, 0)),
                pl.BlockSpec((D_MODEL, tn), lambda i, j: (0, cb + j * step))]
    args = [h, w]
    if kind in (K_NORM, K_ROPE):
        in_specs += [col_spec, col_spec, pl.BlockSpec((NORM_W, NORM_W), lambda i, j: (0, 0))]
        args += [gain, flag, bd]
    if kind == K_ROPE:
        in_specs += [col_spec] + [pl.BlockSpec((tm, LANES), lambda i, j: (i % ns, 0))] * 2
        args += [rflag, cos, sin]
    scratch = []
    if r == 1:
        out_shape = jax.ShapeDtypeStruct((n, nj * tn), BF16)
        out_spec = pl.BlockSpec((tm, tn), lambda i, j: (i, j))
    else:
        out_shape = jax.ShapeDtypeStruct((n // seq, r, seq // r, nj * tn), BF16)
        out_spec = pl.BlockSpec((None, r, tm // r, tn), lambda i, j: (i // ns, 0, i % ns, j))
        scratch = [pltpu.VMEM((tn // LANES, tm, LANES), F32)]
    return pl.pallas_call(
        functools.partial(_proj_kernel, kind=kind, r=r, tm=tm, tn=tn),
        grid=(n // tm, nj),
        in_specs=in_specs,
        out_specs=out_spec,
        out_shape=out_shape,
        scratch_shapes=scratch,
        compiler_params=pltpu.CompilerParams(
            dimension_semantics=("parallel", "parallel"), vmem_limit_bytes=VMEM_LIMIT),
    )(*args)


def _bucket(rel):
    nb = REL_BUCKETS // 2
    exact = nb // 2
    n = jnp.abs(rel)
    large = exact + (jnp.log(jnp.maximum(n, exact).astype(F32) / exact)
                     / math.log(REL_MAX_DIST / exact) * (nb - exact)).astype(jnp.int32)
    large = jnp.minimum(large, nb - 1)
    return jnp.where(rel > 0, nb, 0) + jnp.where(n < exact, n, large)


def _lookup(tbl, bucket):
    shape = (tbl.shape[1],) + (1,) * bucket.ndim
    out = jnp.zeros((tbl.shape[1],) + bucket.shape, F32)
    for j in range(REL_BUCKETS):
        out = jnp.where(bucket[None] == j, tbl[j].reshape(shape), out)
    return out


def _band_bias(tbl, stride, half, tq, w, pad, hpc):
    nh = tbl.shape[1]
    tiles = []
    for off in (0, pad, w - tq):
        rel = jnp.arange(w)[:, None] - off - jnp.arange(tq)[None, :]
        b = jnp.where((jnp.abs(rel) <= half)[None], _lookup(tbl, _bucket(rel * stride)) * LOG2E, NEG)
        b = b.reshape(nh // hpc, hpc, w, tq).transpose(0, 2, 1, 3).reshape(nh // hpc, w, hpc * tq)
        tiles.append(b)
    return jnp.stack(tiles, axis=0)


BAND_HPC = 4
BAND_NSUB = 4


def _band_kernel(*refs, qi_axis, shared_kv, tq, nsub, w, pad, m_len, has_sink, lse_out):
    refs = list(refs)
    q_ref, k_ref, v_ref, bias_ref = refs[:4]
    pos = 4
    sink_ref = lse_ref = None
    if has_sink:
        sink_ref = refs[pos]; pos += 1
    o_ref = refs[pos]; pos += 1
    if lse_out:
        lse_ref = refs[pos]

    hpc = BAND_HPC
    kw = HEAD_DIM if shared_kv else hpc * HEAD_DIM
    last = m_len // tq - 1
    qi = pl.program_id(qi_axis)
    qt = q_ref[...].T
    ones = jnp.ones((ONES_ROWS, w), BF16)
    for j in range(nsub):
        qb = qi * nsub + j
        ks = pl.multiple_of(jnp.clip(qb * tq - pad, 0, m_len - w), 64)
        case = jnp.where(qb == 0, 0, jnp.where(qb == last, 2, 1))
        for c in range(q_ref.shape[1] // (hpc * HEAD_DIM)):
            slab = qt[c * hpc * HEAD_DIM:(c + 1) * hpc * HEAD_DIM, j * tq:(j + 1) * tq]
            if shared_kv:
                qc = jnp.concatenate([slab[g * HEAD_DIM:(g + 1) * HEAD_DIM] for g in range(hpc)], axis=1)
            else:
                head = lax.broadcasted_iota(jnp.int32, slab.shape, 0) // HEAD_DIM
                qc = jnp.concatenate([jnp.where(head == g, slab, jnp.zeros_like(slab)) for g in range(hpc)],
                                     axis=1)
            k = k_ref[pl.ds(ks, w), c * kw:(c + 1) * kw]
            v = v_ref[pl.ds(ks, w), c * kw:(c + 1) * kw]
            vt = jnp.concatenate([v.T, ones], axis=0)
            st = jnp.dot(k, qc, preferred_element_type=F32) + bias_ref[case, c]
            m = jnp.max(st, axis=0, keepdims=True)
            if has_sink:
                sk = sink_ref[c]
                m = jnp.maximum(m, sk)
            pt = jnp.exp2(st - m).astype(BF16)
            acc = jnp.dot(vt, pt, preferred_element_type=F32)
            l = acc[kw:kw + 1]
            if has_sink:
                l = l + jnp.exp2(sk - m)
            lse = m + jnp.log2(l)
            for g in range(hpc):
                h = c * hpc + g
                lanes = slice(g * tq, (g + 1) * tq)
                rows = slice(0, HEAD_DIM) if shared_kv else slice(g * HEAD_DIM, (g + 1) * HEAD_DIM)
                out_rows = slice(j * tq, (j + 1) * tq)
                cols = slice(h * HEAD_DIM, (h + 1) * HEAD_DIM)
                ot = acc[rows, lanes] / l[:, lanes]
                o_ref[out_rows, cols] = ot.T.astype(o_ref.dtype)
                if lse_out:
                    lse_ref[out_rows, cols] = jnp.broadcast_to(lse[:, lanes], (HEAD_DIM, tq)).T


def _window_attention(pn, pr, bias, sink_rows, tq, nsub):
    bsz, seq, _ = pn.shape
    pad = A_WIN
    w = tq + 2 * pad
    lanes = BAND_HPC * tq
    kern = functools.partial(_band_kernel, qi_axis=1, shared_kv=True, tq=tq, nsub=nsub, w=w, pad=pad, m_len=seq,
                             has_sink=True, lse_out=False)
    return pl.pallas_call(
        kern,
        grid=(bsz, seq // (nsub * tq)),
        in_specs=[
            pl.BlockSpec((None, nsub * tq, 512), lambda b, qi: (b, qi, 0)),
            pl.BlockSpec((None, seq, 128), lambda b, qi: (b, 0, 6)),
            pl.BlockSpec((None, seq, 128), lambda b, qi: (b, 0, 7)),
            pl.BlockSpec((3, A_KV, w, lanes), lambda b, qi: (0, 0, 0, 0)),
            pl.BlockSpec((A_KV, 1, lanes), lambda b, qi: (0, 0, 0)),
        ],
        out_specs=pl.BlockSpec((None, nsub * tq, BRANCH_W), lambda b, qi: (b, qi, 0)),
        out_shape=jax.ShapeDtypeStruct((bsz, seq, BRANCH_W), BF16),
        compiler_params=pltpu.CompilerParams(
            dimension_semantics=("parallel", "arbitrary"), vmem_limit_bytes=VMEM_LIMIT),
    )(pn, pr, pr, bias, sink_rows)


def _dilated_group(qk, v, bias, tq, nsub):
    bsz, r, m_len, _ = v.shape
    pad = 64
    w = min(tq + 2 * pad, m_len)
    nsub = min(nsub, m_len // tq)
    nchain = D_HEADS_PER_GROUP // BAND_HPC
    kern = functools.partial(_band_kernel, qi_axis=2, shared_kv=False, tq=tq, nsub=nsub, w=w, pad=pad,
                             m_len=m_len, has_sink=False, lse_out=True)
    row_spec = pl.BlockSpec((None, None, nsub * tq, BRANCH_W), lambda b, c, qi: (b, c, qi, 0))
    return pl.pallas_call(
        kern,
        grid=(bsz, r, m_len // (nsub * tq)),
        in_specs=[
            row_spec,
            pl.BlockSpec((None, None, m_len, 512), lambda b, c, qi: (b, c, 0, 1)),
            pl.BlockSpec((None, None, m_len, 512), lambda b, c, qi: (b, c, 0, 0)),
            pl.BlockSpec((3, nchain, w, BAND_HPC * tq), lambda b, c, qi: (0, 0, 0, 0)),
        ],
        out_specs=[row_spec, row_spec],
        out_shape=[jax.ShapeDtypeStruct((bsz, r, m_len, BRANCH_W), BF16),
                   jax.ShapeDtypeStruct((bsz, r, m_len, BRANCH_W), F32)],
        compiler_params=pltpu.CompilerParams(
            dimension_semantics=("parallel", "parallel", "arbitrary"), vmem_limit_bytes=VMEM_LIMIT),
    )(qk, qk, v, bias)


DIFF_UNIT = 512
DIFF_SUB = 1


def _saturation_distance():
    nb = REL_BUCKETS // 2
    exact = nb // 2
    n = np.arange(exact, 4 * REL_MAX_DIST)
    large = exact + (np.log(n.astype(np.float32) / exact) / math.log(REL_MAX_DIST / exact)
                     * (nb - exact)).astype(np.int32)
    return int(n[large < nb - 1].max()) + 1 + 2


def _ceil_div(a, b):
    return -(-a // b)


def _diff_tiles(tq, tk):
    far = _saturation_distance()
    return -_ceil_div(far + tk - 1, DIFF_UNIT), _ceil_div(far + tq - 1, DIFF_UNIT)


def _diff_kernel(lv_ref, sg_ref, q_ref, k_ref, v_ref, bias_ref, o_ref, *, tq, tk, nk, lo, hi, lam_init):
    qi = pl.program_id(2)
    vd = 2 * HEAD_DIM
    ts = tq // DIFF_SUB
    qt = q_ref[...].T
    first = lax.broadcasted_iota(jnp.int32, (vd, ts), 0) < HEAD_DIM
    zero = jnp.zeros((vd, ts), BF16)
    qbd = []
    for s in range(DIFF_SUB):
        slab = qt[:, s * ts:(s + 1) * ts]
        qbd.append(jnp.concatenate([jnp.where(first, slab, zero), jnp.where(first, zero, slab)], axis=1))
    m0 = jnp.full((1, 2 * ts), NEG, F32)
    a0 = jnp.zeros((vd + ONES_ROWS, 2 * ts), F32)
    ones = jnp.ones((ONES_ROWS, tk), BF16)

    def body(ki, carry):
        k0 = pl.multiple_of(ki * tk, tk)
        k = k_ref[pl.ds(k0, tk), :]
        vt = jnp.concatenate([v_ref[pl.ds(k0, tk), :].T, ones], axis=0)
        dd = jnp.clip(ki * (tk // DIFF_UNIT) - qi * (tq // DIFF_UNIT), lo, hi) - lo
        out = []
        for s in range(DIFF_SUB):
            m, acc = carry[s]
            b = bias_ref[dd, :, s * ts:(s + 1) * ts]
            st = jnp.dot(k, qbd[s], preferred_element_type=F32)
            st = jnp.concatenate([st[:, :ts] + b, st[:, ts:] + b], axis=1)
            m_new = jnp.maximum(m, jnp.max(st, axis=0, keepdims=True))
            pt = jnp.exp2(st - m_new).astype(BF16)
            acc = jnp.exp2(m - m_new) * acc + jnp.dot(vt, pt, preferred_element_type=F32)
            out.append((m_new, acc))
        return tuple(out)

    res = lax.fori_loop(0, nk, body, tuple((m0, a0) for _ in range(DIFF_SUB)))
    lv = lv_ref[...]
    lam = (jnp.exp(jnp.sum(lv[0:1] * lv[1:2], axis=-1, keepdims=True))
           - jnp.exp(jnp.sum(lv[2:3] * lv[3:4], axis=-1, keepdims=True)) + lam_init)
    for s in range(DIFF_SUB):
        a = res[s][1]
        ot = a[:vd, :ts] / a[vd:vd + 1, :ts] - lam * (a[:vd, ts:] / a[vd:vd + 1, ts:])
        ms = jnp.mean(ot * ot, axis=0, keepdims=True)
        ot = ot * lax.rsqrt(ms + EPS) * sg_ref[...] * (1.0 - lam_init)
        o_ref[s * ts:(s + 1) * ts, :] = ot.T.astype(o_ref.dtype)


def _diff_bias(tbl, tq, tk):
    lo, hi = _diff_tiles(tq, tk)
    tiles = []
    for d in range(lo, hi + 1):
        rel = d * DIFF_UNIT + jnp.arange(tk)[:, None] - jnp.arange(tq)[None, :]
        tiles.append(_lookup(tbl, _bucket(rel)) * LOG2E)
    return jnp.stack(tiles, axis=1)


def _diff_attention(pn, pbv, bias, lam_vec, sub_gain, lam_init, tq, tk):
    bsz, seq, _ = pn.shape
    lo, hi = _diff_tiles(tq, tk)
    nt = hi - lo + 1
    vd = 2 * HEAD_DIM
    kern = functools.partial(_diff_kernel, tq=tq, tk=tk, nk=seq // tk, lo=lo, hi=hi, lam_init=lam_init)
    return pl.pallas_call(
        kern,
        grid=(B_HEADS, bsz, seq // tq),
        in_specs=[
            pl.BlockSpec((4, HEAD_DIM), lambda h, b, qi: (0, 0)),
            pl.BlockSpec((vd, 1), lambda h, b, qi: (0, 0)),
            pl.BlockSpec((None, tq, vd), lambda h, b, qi: (b, qi, 512 // vd + h)),
            pl.BlockSpec((None, seq, vd), lambda h, b, qi: (b, 0, 1024 // vd + h)),
            pl.BlockSpec((None, seq, vd), lambda h, b, qi: (b, 0, h)),
            pl.BlockSpec((None, nt, tk, tq), lambda h, b, qi: (h, 0, 0, 0), pipeline_mode=pl.Buffered(1)),
        ],
        out_specs=pl.BlockSpec((None, tq, vd), lambda h, b, qi: (b, qi, h)),
        out_shape=jax.ShapeDtypeStruct((bsz, seq, BRANCH_W), BF16),
        compiler_params=pltpu.CompilerParams(
            dimension_semantics=("parallel", "parallel", "arbitrary"), vmem_limit_bytes=VMEM_LIMIT),
    )(lam_vec, sub_gain.reshape(vd, 1), pn, pn, pbv, bias)


def _gqa_kernel(q0_ref, q1_ref, k_ref, v_ref, o_ref, *, tq, tk, nk):
    grp = C_HEADS // C_KV
    qts = []
    for q_ref in (q0_ref, q1_ref):
        qt = q_ref[...].T
        qts.append(jnp.concatenate([qt[g * HEAD_DIM:(g + 1) * HEAD_DIM] for g in range(grp)], axis=1))
    m0 = jnp.full((1, grp * tq), NEG, F32)
    a0 = jnp.zeros((HEAD_DIM + ONES_ROWS, grp * tq), F32)
    ones = jnp.ones((ONES_ROWS, tk), BF16)

    def body(ki, carry):
        k0 = pl.multiple_of(ki * tk, tk)
        out = []
        for kv in range(C_KV):
            m, acc = carry[kv]
            k = k_ref[pl.ds(k0, tk), kv * HEAD_DIM:(kv + 1) * HEAD_DIM]
            v = v_ref[pl.ds(k0, tk), kv * HEAD_DIM:(kv + 1) * HEAD_DIM]
            vt = jnp.concatenate([v.T, ones], axis=0)
            st = jnp.dot(k, qts[kv], preferred_element_type=F32)
            m_new = jnp.maximum(m, jnp.max(st, axis=0, keepdims=True))
            pt = jnp.exp2(st - m_new).astype(BF16)
            acc = jnp.exp2(m - m_new) * acc + jnp.dot(vt, pt, preferred_element_type=F32)
            out.append((m_new, acc))
        return tuple(out)

    res = lax.fori_loop(0, nk, body, ((m0, a0), (m0, a0)))
    for kv in range(C_KV):
        acc = res[kv][1]
        ot = acc[:HEAD_DIM] / acc[HEAD_DIM:HEAD_DIM + 1]
        for g in range(grp):
            h = kv * grp + g
            o_ref[:, h * HEAD_DIM:(h + 1) * HEAD_DIM] = ot[:, g * tq:(g + 1) * tq].T.astype(o_ref.dtype)


def _dense_gqa(pr, tq=512, tk=512):
    bsz, seq, _ = pr.shape
    kern = functools.partial(_gqa_kernel, tq=tq, tk=tk, nk=seq // tk)
    return pl.pallas_call(
        kern,
        grid=(bsz, seq // tq),
        in_specs=[
            pl.BlockSpec((None, tq, 256), lambda b, qi: (b, qi, 0)),
            pl.BlockSpec((None, tq, 256), lambda b, qi: (b, qi, 1)),
            pl.BlockSpec((None, seq, 128), lambda b, qi: (b, 0, 4)),
            pl.BlockSpec((None, seq, 128), lambda b, qi: (b, 0, 5)),
        ],
        out_specs=pl.BlockSpec((None, tq, BRANCH_W), lambda b, qi: (b, qi, 0)),
        out_shape=jax.ShapeDtypeStruct((bsz, seq, BRANCH_W), BF16),
        compiler_params=pltpu.CompilerParams(
            dimension_semantics=("parallel", "arbitrary"), vmem_limit_bytes=VMEM_LIMIT),
    )(pr, pr, pr, pr)


def _merge_kernel(*refs, tm):
    x_ref, oa_ref, ob_ref, oc_ref = refs[:4]
    d_refs = refs[4:10]
    gate_ref, merge_ref, wb_ref, wo_ref, out_ref, so_ref, sl_ref = refs[10:17]
    nslab = BRANCH_W // LANES

    def natural(ref, scr, r):
        if r == 1:
            return ref[0].astype(F32)
        for c in range(r):
            blk = ref[c].astype(F32)
            for s in range(nslab):
                scr[s, pl.ds(c, tm // r, stride=r), :] = blk[:, s * LANES:(s + 1) * LANES]
        return jnp.concatenate([scr[s] for s in range(nslab)], axis=1)

    os_, ls_ = [], []
    for g, (_, r) in enumerate(D_PAIRS):
        os_.append(natural(d_refs[2 * g], so_ref, r))
        ls_.append(natural(d_refs[2 * g + 1], sl_ref, r))
    mx = jnp.maximum(jnp.maximum(ls_[0], ls_[1]), ls_[2])
    es = [jnp.exp2(l - mx) for l in ls_]
    od = (es[0] * os_[0] + es[1] * os_[1] + es[2] * os_[2]) / (es[0] + es[1] + es[2])

    branches = [oa_ref[...].astype(F32), ob_ref[...].astype(F32), oc_ref[...].astype(F32), od]
    gated = [(branches[n] * gate_ref[:, n * BRANCH_W:(n + 1) * BRANCH_W].astype(F32)).astype(BF16)
             for n in range(N_BRANCH)]
    halves = []
    for half in range(2):
        cols = slice(half * 512, (half + 1) * 512)
        merged = None
        for n in range(N_BRANCH):
            y = jnp.dot(gated[n], wb_ref[n, :, cols], preferred_element_type=F32)
            term = merge_ref[:, n * D_MODEL + half * 512:n * D_MODEL + (half + 1) * 512].astype(F32) * y
            merged = term if merged is None else merged + term
        halves.append(merged.astype(BF16))
    merged = jnp.concatenate(halves, axis=1)
    out_ref[...] = x_ref[...] + jnp.dot(merged, wo_ref[...], preferred_element_type=F32)


def _merge(x, oa, ob, oc, d_outs, gate, merge, wb, wo, seq, tm=256):
    n = x.shape[0]
    ns = seq // tm
    row = pl.BlockSpec((tm, BRANCH_W), lambda i: (i, 0))
    in_specs = [pl.BlockSpec((tm, D_MODEL), lambda i: (i, 0)), row, row, row]
    for _, r in D_PAIRS:
        in_specs += [pl.BlockSpec((None, r, tm // r, BRANCH_W), lambda i: (i // ns, 0, i % ns, 0))] * 2
    in_specs += [pl.BlockSpec((tm, N_BRANCH * BRANCH_W), lambda i: (i, 0)),
                 pl.BlockSpec((tm, N_BRANCH * D_MODEL), lambda i: (i, 0)),
                 pl.BlockSpec((N_BRANCH, BRANCH_W, D_MODEL), lambda i: (0, 0, 0)),
                 pl.BlockSpec((D_MODEL, D_MODEL), lambda i: (0, 0))]
    return pl.pallas_call(
        functools.partial(_merge_kernel, tm=tm),
        grid=(n // tm,),
        in_specs=in_specs,
        out_specs=pl.BlockSpec((tm, D_MODEL), lambda i: (i, 0)),
        out_shape=jax.ShapeDtypeStruct((n, D_MODEL), F32),
        scratch_shapes=[pltpu.VMEM((BRANCH_W // LANES, tm, LANES), F32)] * 2,
        compiler_params=pltpu.CompilerParams(
            dimension_semantics=("parallel",), vmem_limit_bytes=VMEM_LIMIT),
    )(x, oa, ob, oc, *d_outs, gate, merge, wb, wo)


def kernel(x, w_in, w_branch, w_out, norm_gain, qk_gain, sink, lambda_vec, sub_norm_gain, rel_bias):
    bsz, seq, _ = x.shape
    n = bsz * seq
    depth = w_in.shape[0]
    rb = rel_bias.astype(F32)

    rows = seq // GRID_W
    row = jnp.repeat(jnp.arange(rows), GRID_W).astype(F32)
    col = jnp.tile(jnp.arange(GRID_W), rows).astype(F32)
    nf = HEAD_DIM // 4
    freqs = ROPE_THETA ** (-jnp.arange(nf, dtype=F32) / nf)
    ang = jnp.concatenate([row[:, None] * freqs, col[:, None] * freqs], axis=-1)
    cos2 = jnp.tile(jnp.cos(ang), (1, 4))
    sin2 = jnp.tile(jnp.concatenate([-jnp.sin(ang), jnp.sin(ang)], axis=-1), (1, 2))

    tq_a = 128
    bias_a = _band_bias(rb[:, :A_HEADS], 1, A_WIN, tq_a, tq_a + 2 * A_WIN, A_WIN, BAND_HPC)
    tq_b, tk_b = 1024, 512
    bias_b = _diff_bias(rb[:, A_HEADS:A_HEADS + B_HEADS], tq_b, tk_b)
    tq_d = 128
    bias_d = []
    for g, (win, r) in enumerate(D_PAIRS):
        lo = A_HEADS + B_HEADS + g * D_HEADS_PER_GROUP
        wd = min(tq_d + 128, seq // r)
        bias_d.append(_band_bias(rb[:, lo:lo + D_HEADS_PER_GROUP], r, win // (2 * r), tq_d, wd, 64, BAND_HPC))

    flag_np, rflag_np = _col_flags()
    flag = _permute_cols(jnp.asarray(flag_np))
    rflag = _permute_cols(jnp.asarray(rflag_np))
    blk = np.arange(NORM_W) // HEAD_DIM
    bd = jnp.asarray((blk[:, None] == blk[None, :]).astype(np.float32), dtype=BF16)

    xf = x.reshape(n, D_MODEL)
    for l in range(depth):
        h = _prenorm(xf, norm_gain[l].reshape(1, D_MODEL))
        w = _permute_cols(w_in[l].astype(BF16))
        gain = _permute_cols(_gain_cols(qk_gain[l]))
        norm_args = dict(gain=gain, flag=flag, bd=bd)
        pr = _proj_call(h, w, kind=K_ROPE, col0=P_ROPE, step=1, nj=2, seq=seq, rflag=rflag, cos=cos2, sin=sin2,
                        **norm_args).reshape(bsz, seq, -1)
        pn = _proj_call(h, w, kind=K_NORM, col0=P_NORM, step=1, nj=3, seq=seq, **norm_args).reshape(bsz, seq, -1)
        pbv = _proj_call(h, w, kind=K_RAW, col0=P_BV, step=1, nj=1, seq=seq).reshape(bsz, seq, -1)
        gate = _proj_call(h, w, kind=K_SILU, col0=OFF_GATE, step=1, nj=4, seq=seq)
        merge = _proj_call(h, w, kind=K_SIGMOID, col0=OFF_MERGE, step=1, nj=8, seq=seq)
        sink_rows = jnp.repeat(sink[l].astype(F32) * LOG2E, tq_a).reshape(A_KV, 1, BAND_HPC * tq_a)
        oa = _window_attention(pn, pr, bias_a, sink_rows, tq=tq_a, nsub=BAND_NSUB)
        lam_init = 0.8 - 0.6 * math.exp(-0.3 * l)
        ob = _diff_attention(pn, pbv, bias_b, lambda_vec[l], sub_norm_gain[l], lam_init, tq=tq_b, tk=tk_b)
        oc = _dense_gqa(pr)
        d_outs = []
        for g, (win, r) in enumerate(D_PAIRS):
            dqk = _proj_call(h, w, kind=K_NORM, col0=OFF_DQ + g * 512, step=(OFF_DK - OFF_DQ) // 512, nj=2,
                             seq=seq, r=r, **norm_args)
            dv = _proj_call(h, w, kind=K_RAW, col0=OFF_DV + g * 512, step=1, nj=1, seq=seq, r=r)
            if r == 1:
                dqk, dv = dqk.reshape(bsz, 1, seq, -1), dv.reshape(bsz, 1, seq, -1)
            d_outs += _dilated_group(dqk, dv, bias_d[g], tq=tq_d, nsub=BAND_NSUB)
        xf = _merge(xf, oa.reshape(n, -1), ob.reshape(n, -1), oc.reshape(n, -1), d_outs, gate, merge,
                    w_branch[l].astype(BF16), w_out[l].astype(BF16), seq)
    return xf.reshape(bsz, seq, D_MODEL)
```

```python
import functools
import math

import numpy as np
import jax
import jax.numpy as jnp
from jax import lax
from jax.experimental import pallas as pl
from jax.experimental.pallas import tpu as pltpu

F32 = jnp.float32
BF16 = jnp.bfloat16

D_MODEL = 1024
HEAD_DIM = 64
GRID_W = 64
EPS = 1e-6
A_HEADS, A_KV, A_WIN = 8, 2, 128
B_HEADS = 4
C_HEADS, C_KV = 8, 2
ROPE_THETA = 10000.0
D_PAIRS = ((128, 1), (512, 4), (2048, 16))
D_HEADS_PER_GROUP = 8
N_BRANCH = 4
BRANCH_W = 512
REL_BUCKETS = 32
REL_MAX_DIST = 1024
IN_WIDTH = 13824
NEG = -1e30
LOG2E = math.log2(math.e)
LANES = 128
ONES_ROWS = 16

OFF_AQ, OFF_AK, OFF_AV = 0, 512, 640
OFF_BQ, OFF_BK, OFF_BV = 768, 1280, 1792
OFF_CQ, OFF_CK, OFF_CV = 2304, 2816, 2944
OFF_DQ, OFF_DK, OFF_DV = 3072, 4608, 6144
OFF_GATE, OFF_MERGE = 7680, 9728
PERM = ((OFF_CQ, 768), (OFF_AK, 256), (OFF_AQ, 512), (OFF_BQ, 1024), (OFF_BV, 512))
P_ROPE, P_NORM, P_BV = 0, 1024, 2560

VMEM_LIMIT = 56 * 1024 * 1024

K_RAW, K_NORM, K_ROPE, K_SILU, K_SIGMOID = range(5)


def _permute_cols(a):
    return jnp.concatenate([a[..., lo:lo + n] for lo, n in PERM] + [a[..., OFF_DQ:]], axis=-1)


def _col_flags():
    norm = np.zeros((IN_WIDTH,), np.float32)
    for lo, hi in ((OFF_AQ, OFF_AV), (OFF_BQ, OFF_BV), (OFF_CQ, OFF_CV), (OFF_DQ, OFF_DV)):
        norm[lo:hi] = 1.0
    rope = np.zeros((IN_WIDTH,), np.float32)
    rope[OFF_CQ:OFF_CV] = 1.0
    return norm.reshape(1, IN_WIDTH), rope.reshape(1, IN_WIDTH)


def _gain_cols(g):
    sc = HEAD_DIM ** -0.5 * LOG2E
    one = lambda n: jnp.ones((n,), F32)
    parts = [jnp.tile(g[0, 0], A_HEADS) * sc, jnp.tile(g[0, 1], A_KV), one(128),
             jnp.tile(g[1, 0], 2 * B_HEADS) * sc, jnp.tile(g[1, 1], 2 * B_HEADS), one(512),
             jnp.tile(g[2, 0], C_HEADS) * sc, jnp.tile(g[2, 1], C_KV), one(128),
             jnp.tile(g[3, 0], 24) * sc, jnp.tile(g[3, 1], 24), one(1536),
             one(IN_WIDTH - OFF_GATE)]
    return jnp.concatenate(parts).reshape(1, IN_WIDTH)


def _prenorm_kernel(x_ref, g_ref, o_ref):
    x = x_ref[...]
    ms = jnp.mean(x * x, axis=-1, keepdims=True)
    o_ref[...] = (x * lax.rsqrt(ms + EPS) * g_ref[...]).astype(o_ref.dtype)


def _prenorm(x, g, tm=1024):
    n = x.shape[0]
    return pl.pallas_call(
        _prenorm_kernel,
        grid=(n // tm,),
        in_specs=[pl.BlockSpec((tm, D_MODEL), lambda i: (i, 0)),
                  pl.BlockSpec((1, D_MODEL), lambda i: (0, 0))],
        out_specs=pl.BlockSpec((tm, D_MODEL), lambda i: (i, 0)),
        out_shape=jax.ShapeDtypeStruct((n, D_MODEL), BF16),
        compiler_params=pltpu.CompilerParams(dimension_semantics=("parallel",), vmem_limit_bytes=VMEM_LIMIT),
    )(x, g)


PROJ_TM = 2048
PROJ_TN = 512
PROJ_CHUNK = 256
NORM_W = 256


def _proj_kernel(*refs, kind, r, tm, tn):
    refs = list(refs)
    h_ref, w_ref = refs[:2]
    pos = 2
    if kind in (K_NORM, K_ROPE):
        gain_ref, flag_ref, bd_ref = refs[pos:pos + 3]; pos += 3
    if kind == K_ROPE:
        rflag_ref, cos_ref, sin_ref = refs[pos:pos + 3]; pos += 3
    o_ref = refs[pos]; pos += 1
    slab_ref = refs[pos] if r > 1 else None

    for rc in range(tm // PROJ_CHUNK):
        rows = slice(rc * PROJ_CHUNK, (rc + 1) * PROJ_CHUNK)
        acc = jnp.dot(h_ref[rows, :], w_ref[...], preferred_element_type=F32)
        if kind in (K_NORM, K_ROPE):
            sq = (acc * acc).astype(BF16)
            bd = bd_ref[...]
            ms = jnp.concatenate([jnp.dot(sq[:, c:c + NORM_W], bd, preferred_element_type=F32)
                                  for c in range(0, tn, NORM_W)], axis=1) * (1.0 / HEAD_DIM)
            y = acc * jnp.where(flag_ref[...] > 0.0, lax.rsqrt(ms + EPS), 1.0) * gain_ref[...]
            if kind == K_ROPE:
                c = jnp.concatenate([cos_ref[rows, :]] * (tn // LANES), axis=1)
                s = jnp.concatenate([sin_ref[rows, :]] * (tn // LANES), axis=1)
                lane = lax.broadcasted_iota(jnp.int32, y.shape, 1) & (HEAD_DIM - 1)
                half = HEAD_DIM // 2
                partner = jnp.where(lane < half, pltpu.roll(y, tn - half, axis=1), pltpu.roll(y, half, axis=1))
                y = jnp.where(rflag_ref[...] > 0.0, y * c + partner * s, y)
        elif kind == K_SILU:
            y = acc * (0.5 * jnp.tanh(0.5 * acc) + 0.5)
        elif kind == K_SIGMOID:
            y = 0.5 * jnp.tanh(0.5 * acc) + 0.5
        else:
            y = acc
        if r == 1:
            o_ref[rows, :] = y.astype(o_ref.dtype)
        else:
            for s_ in range(tn // LANES):
                slab_ref[s_, rows, :] = y[:, s_ * LANES:(s_ + 1) * LANES]
    if r > 1:
        for c in range(r):
            for s_ in range(tn // LANES):
                o_ref[c, :, s_ * LANES:(s_ + 1) * LANES] = (
                    slab_ref[s_, pl.ds(c, tm // r, stride=r), :].astype(o_ref.dtype))


def _proj_call(h, w, *, kind, col0, step, nj, seq, r=1, gain=None, flag=None, rflag=None, bd=None, cos=None,
               sin=None):
    tm, tn = PROJ_TM, PROJ_TN
    n = h.shape[0]
    ns = seq // tm
    cb = col0 // tn
    col_spec = pl.BlockSpec((1, tn), lambda i, j: (0, cb + j * step))
    in_specs = [pl.BlockSpec((tm, D_MODEL), lambda i, j: (i, 0)),
                pl.BlockSpec((D_MODEL, tn), lambda i, j: (0, cb + j * step))]
    args = [h, w]
    if kind in (K_NORM, K_ROPE):
        in_specs += [col_spec, col_spec, pl.BlockSpec((NORM_W, NORM_W), lambda i, j: (0, 0))]
        args += [gain, flag, bd]
    if kind == K_ROPE:
        in_specs += [col_spec] + [pl.BlockSpec((tm, LANES), lambda i, j: (i % ns, 0))] * 2
        args += [rflag, cos, sin]
    scratch = []
    if r == 1:
        out_shape = jax.ShapeDtypeStruct((n, nj * tn), BF16)
        out_spec = pl.BlockSpec((tm, tn), lambda i, j: (i, j))
    else:
        out_shape = jax.ShapeDtypeStruct((n // seq, r, seq // r, nj * tn), BF16)
        out_spec = pl.BlockSpec((None, r, tm // r, tn), lambda i, j: (i // ns, 0, i % ns, j))
        scratch = [pltpu.VMEM((tn // LANES, tm, LANES), F32)]
    return pl.pallas_call(
        functools.partial(_proj_kernel, kind=kind, r=r, tm=tm, tn=tn),
        grid=(n // tm, nj),
        in_specs=in_specs,
        out_specs=out_spec,
        out_shape=out_shape,
        scratch_shapes=scratch,
        compiler_params=pltpu.CompilerParams(
            dimension_semantics=("parallel", "parallel"), vmem_limit_bytes=VMEM_LIMIT),
    )(*args)


def _bucket(rel):
    nb = REL_BUCKETS // 2
    exact = nb // 2
    n = jnp.abs(rel)
    large = exact + (jnp.log(jnp.maximum(n, exact).astype(F32) / exact)
                     / math.log(REL_MAX_DIST / exact) * (nb - exact)).astype(jnp.int32)
    large = jnp.minimum(large, nb - 1)
    return jnp.where(rel > 0, nb, 0) + jnp.where(n < exact, n, large)


def _lookup(tbl, bucket):
    shape = (tbl.shape[1],) + (1,) * bucket.ndim
    out = jnp.zeros((tbl.shape[1],) + bucket.shape, F32)
    for j in range(REL_BUCKETS):
        out = jnp.where(bucket[None] == j, tbl[j].reshape(shape), out)
    return out


def _band_bias(tbl, stride, half, tq, w, pad, hpc):
    nh = tbl.shape[1]
    tiles = []
    for off in (0, pad, w - tq):
        rel = jnp.arange(w)[:, None] - off - jnp.arange(tq)[None, :]
        b = jnp.where((jnp.abs(rel) <= half)[None], _lookup(tbl, _bucket(rel * stride)) * LOG2E, NEG)
        b = b.reshape(nh // hpc, hpc, w, tq).transpose(0, 2, 1, 3).reshape(nh // hpc, w, hpc * tq)
        tiles.append(b)
    return jnp.stack(tiles, axis=0)


BAND_HPC = 4
BAND_NSUB = 4


def _band_kernel(*refs, qi_axis, shared_kv, tq, nsub, w, pad, m_len, has_sink, lse_out):
    refs = list(refs)
    q_ref, k_ref, v_ref, bias_ref = refs[:4]
    pos = 4
    sink_ref = lse_ref = None
    if has_sink:
        sink_ref = refs[pos]; pos += 1
    o_ref = refs[pos]; pos += 1
    if lse_out:
        lse_ref = refs[pos]

    hpc = BAND_HPC
    kw = HEAD_DIM if shared_kv else hpc * HEAD_DIM
    last = m_len // tq - 1
    qi = pl.program_id(qi_axis)
    qt = q_ref[...].T
    ones = jnp.ones((ONES_ROWS, w), BF16)
    for j in range(nsub):
        qb = qi * nsub + j
        ks = pl.multiple_of(jnp.clip(qb * tq - pad, 0, m_len - w), 64)
        case = jnp.where(qb == 0, 0, jnp.where(qb == last, 2, 1))
        for c in range(q_ref.shape[1] // (hpc * HEAD_DIM)):
            slab = qt[c * hpc * HEAD_DIM:(c + 1) * hpc * HEAD_DIM, j * tq:(j + 1) * tq]
            if shared_kv:
                qc = jnp.concatenate([slab[g * HEAD_DIM:(g + 1) * HEAD_DIM] for g in range(hpc)], axis=1)
            else:
                head = lax.broadcasted_iota(jnp.int32, slab.shape, 0) // HEAD_DIM
                qc = jnp.concatenate([jnp.where(head == g, slab, jnp.zeros_like(slab)) for g in range(hpc)],
                                     axis=1)
            k = k_ref[pl.ds(ks, w), c * kw:(c + 1) * kw]
            v = v_ref[pl.ds(ks, w), c * kw:(c + 1) * kw]
            vt = jnp.concatenate([v.T, ones], axis=0)
            st = jnp.dot(k, qc, preferred_element_type=F32) + bias_ref[case, c]
            m = jnp.max(st, axis=0, keepdims=True)
            if has_sink:
                sk = sink_ref[c]
                m = jnp.maximum(m, sk)
            pt = jnp.exp2(st - m).astype(BF16)
            acc = jnp.dot(vt, pt, preferred_element_type=F32)
            l = acc[kw:kw + 1]
            if has_sink:
                l = l + jnp.exp2(sk - m)
            lse = m + jnp.log2(l)
            for g in range(hpc):
                h = c * hpc + g
                lanes = slice(g * tq, (g + 1) * tq)
                rows = slice(0, HEAD_DIM) if shared_kv else slice(g * HEAD_DIM, (g + 1) * HEAD_DIM)
                out_rows = slice(j * tq, (j + 1) * tq)
                cols = slice(h * HEAD_DIM, (h + 1) * HEAD_DIM)
                ot = acc[rows, lanes] / l[:, lanes]
                o_ref[out_rows, cols] = ot.T.astype(o_ref.dtype)
                if lse_out:
                    lse_ref[out_rows, cols] = jnp.broadcast_to(lse[:, lanes], (HEAD_DIM, tq)).T


def _window_attention(pn, pr, bias, sink_rows, tq, nsub):
    bsz, seq, _ = pn.shape
    pad = A_WIN
    w = tq + 2 * pad
    lanes = BAND_HPC * tq
    kern = functools.partial(_band_kernel, qi_axis=1, shared_kv=True, tq=tq, nsub=nsub, w=w, pad=pad, m_len=seq,
                             has_sink=True, lse_out=False)
    return pl.pallas_call(
        kern,
        grid=(bsz, seq // (nsub * tq)),
        in_specs=[
            pl.BlockSpec((None, nsub * tq, 512), lambda b, qi: (b, qi, 0)),
            pl.BlockSpec((None, seq, 128), lambda b, qi: (b, 0, 6)),
            pl.BlockSpec((None, seq, 128), lambda b, qi: (b, 0, 7)),
            pl.BlockSpec((3, A_KV, w, lanes), lambda b, qi: (0, 0, 0, 0)),
            pl.BlockSpec((A_KV, 1, lanes), lambda b, qi: (0, 0, 0)),
        ],
        out_specs=pl.BlockSpec((None, nsub * tq, BRANCH_W), lambda b, qi: (b, qi, 0)),
        out_shape=jax.ShapeDtypeStruct((bsz, seq, BRANCH_W), BF16),
        compiler_params=pltpu.CompilerParams(
            dimension_semantics=("parallel", "arbitrary"), vmem_limit_bytes=VMEM_LIMIT),
    )(pn, pr, pr, bias, sink_rows)


def _dilated_group(qk, v, bias, tq, nsub):
    bsz, r, m_len, _ = v.shape
    pad = 64
    w = min(tq + 2 * pad, m_len)
    nsub = min(nsub, m_len // tq)
    nchain = D_HEADS_PER_GROUP // BAND_HPC
    kern = functools.partial(_band_kernel, qi_axis=2, shared_kv=False, tq=tq, nsub=nsub, w=w, pad=pad,
                             m_len=m_len, has_sink=False, lse_out=True)
    row_spec = pl.BlockSpec((None, None, nsub * tq, BRANCH_W), lambda b, c, qi: (b, c, qi, 0))
    return pl.pallas_call(
        kern,
        grid=(bsz, r, m_len // (nsub * tq)),
        in_specs=[
            row_spec,
            pl.BlockSpec((None, None, m_len, 512), lambda b, c, qi: (b, c, 0, 1)),
            pl.BlockSpec((None, None, m_len, 512), lambda b, c, qi: (b, c, 0, 0)),
            pl.BlockSpec((3, nchain, w, BAND_HPC * tq), lambda b, c, qi: (0, 0, 0, 0)),
        ],
        out_specs=[row_spec, row_spec],
        out_shape=[jax.ShapeDtypeStruct((bsz, r, m_len, BRANCH_W), BF16),
                   jax.ShapeDtypeStruct((bsz, r, m_len, BRANCH_W), F32)],
        compiler_params=pltpu.CompilerParams(
            dimension_semantics=("parallel", "parallel", "arbitrary"), vmem_limit_bytes=VMEM_LIMIT),
    )(qk, qk, v, bias)


DIFF_UNIT = 512


def _saturation_distance():
    nb = REL_BUCKETS // 2
    exact = nb // 2
    n = np.arange(exact, 4 * REL_MAX_DIST)
    large = exact + (np.log(n.astype(np.float32) / exact) / math.log(REL_MAX_DIST / exact)
                     * (nb - exact)).astype(np.int32)
    return int(n[large < nb - 1].max()) + 1 + 2


def _ceil_div(a, b):
    return -(-a // b)


def _diff_tiles(tq, tk):
    far = _saturation_distance()
    return -_ceil_div(far + tk - 1, DIFF_UNIT), _ceil_div(far + tq - 1, DIFF_UNIT)


def _diff_kernel(lv_ref, sg_ref, q_ref, k_ref, v_ref, bias_ref, o_ref, st_ref, *, tq, tk, nk, lo, hi, lam_init):
    qi = pl.program_id(2)
    vd = 2 * HEAD_DIM
    qt = q_ref[...].T
    first = lax.broadcasted_iota(jnp.int32, (vd, tq), 0) < HEAD_DIM
    zero = jnp.zeros((vd, tq), BF16)
    qbd = jnp.concatenate([jnp.where(first, qt, zero), jnp.where(first, zero, qt)], axis=1)
    m = jnp.full((1, 2 * tq), NEG, F32)
    acc = jnp.zeros((vd + ONES_ROWS, 2 * tq), F32)
    ones = jnp.ones((ONES_ROWS, tk), BF16)

    def scores(t):
        dd = jnp.clip(t * (tk // DIFF_UNIT) - qi * (tq // DIFF_UNIT), lo, hi) - lo
        b = bias_ref[dd]
        st = jnp.dot(k_ref[t * tk:(t + 1) * tk, :], qbd, preferred_element_type=F32)
        return jnp.concatenate([st[:, :tq] + b, st[:, tq:] + b], axis=1)

    st_ref[0] = scores(0)
    for t in range(nk):
        if t + 1 < nk:
            st_ref[(t + 1) % 2] = scores(t + 1)
        st = st_ref[t % 2]
        vt = jnp.concatenate([v_ref[t * tk:(t + 1) * tk, :].T, ones], axis=0)
        m_new = jnp.maximum(m, jnp.max(st, axis=0, keepdims=True))
        pt = jnp.exp2(st - m_new).astype(BF16)
        acc = jnp.exp2(m - m_new) * acc + jnp.dot(vt, pt, preferred_element_type=F32)
        m = m_new

    lv = lv_ref[...]
    lam = (jnp.exp(jnp.sum(lv[0:1] * lv[1:2], axis=-1, keepdims=True))
           - jnp.exp(jnp.sum(lv[2:3] * lv[3:4], axis=-1, keepdims=True)) + lam_init)
    ot = acc[:vd, :tq] / acc[vd:vd + 1, :tq] - lam * (acc[:vd, tq:] / acc[vd:vd + 1, tq:])
    ms = jnp.mean(ot * ot, axis=0, keepdims=True)
    ot = ot * lax.rsqrt(ms + EPS) * sg_ref[...] * (1.0 - lam_init)
    o_ref[...] = ot.T.astype(o_ref.dtype)


def _diff_bias(tbl, tq, tk):
    lo, hi = _diff_tiles(tq, tk)
    tiles = []
    for d in range(lo, hi + 1):
        rel = d * DIFF_UNIT + jnp.arange(tk)[:, None] - jnp.arange(tq)[None, :]
        tiles.append(_lookup(tbl, _bucket(rel)) * LOG2E)
    return jnp.stack(tiles, axis=1)


def _diff_attention(pn, pbv, bias, lam_vec, sub_gain, lam_init, tq, tk):
    bsz, seq, _ = pn.shape
    lo, hi = _diff_tiles(tq, tk)
    nt = hi - lo + 1
    vd = 2 * HEAD_DIM
    kern = functools.partial(_diff_kernel, tq=tq, tk=tk, nk=seq // tk, lo=lo, hi=hi, lam_init=lam_init)
    return pl.pallas_call(
        kern,
        grid=(B_HEADS, bsz, seq // tq),
        in_specs=[
            pl.BlockSpec((4, HEAD_DIM), lambda h, b, qi: (0, 0)),
            pl.BlockSpec((vd, 1), lambda h, b, qi: (0, 0)),
            pl.BlockSpec((None, tq, vd), lambda h, b, qi: (b, qi, 512 // vd + h)),
            pl.BlockSpec((None, seq, vd), lambda h, b, qi: (b, 0, 1024 // vd + h)),
            pl.BlockSpec((None, seq, vd), lambda h, b, qi: (b, 0, h)),
            pl.BlockSpec((None, nt, tk, tq), lambda h, b, qi: (h, 0, 0, 0), pipeline_mode=pl.Buffered(1)),
        ],
        out_specs=pl.BlockSpec((None, tq, vd), lambda h, b, qi: (b, qi, h)),
        out_shape=jax.ShapeDtypeStruct((bsz, seq, BRANCH_W), BF16),
        scratch_shapes=[pltpu.VMEM((2, tk, 2 * tq), F32)],
        compiler_params=pltpu.CompilerParams(
            dimension_semantics=("parallel", "parallel", "arbitrary"), vmem_limit_bytes=VMEM_LIMIT),
    )(lam_vec, sub_gain.reshape(vd, 1), pn, pn, pbv, bias)


def _gqa_kernel(q0_ref, q1_ref, k_ref, v_ref, o_ref, st_ref, *, tq, tk, nk):
    grp = C_HEADS // C_KV
    qts = []
    for q_ref in (q0_ref, q1_ref):
        qt = q_ref[...].T
        qts.append(jnp.concatenate([qt[g * HEAD_DIM:(g + 1) * HEAD_DIM] for g in range(grp)], axis=1))
    ones = jnp.ones((ONES_ROWS, tk), BF16)
    state = [(jnp.full((1, grp * tq), NEG, F32), jnp.zeros((HEAD_DIM + ONES_ROWS, grp * tq), F32))
             for _ in range(C_KV)]

    def scores(t, slot):
        for kv in range(C_KV):
            k = k_ref[t * tk:(t + 1) * tk, kv * HEAD_DIM:(kv + 1) * HEAD_DIM]
            st_ref[slot, kv] = jnp.dot(k, qts[kv], preferred_element_type=F32)

    scores(0, 0)
    for t in range(nk):
        if t + 1 < nk:
            scores(t + 1, (t + 1) % 2)
        for kv in range(C_KV):
            m, acc = state[kv]
            st = st_ref[t % 2, kv]
            v = v_ref[t * tk:(t + 1) * tk, kv * HEAD_DIM:(kv + 1) * HEAD_DIM]
            vt = jnp.concatenate([v.T, ones], axis=0)
            m_new = jnp.maximum(m, jnp.max(st, axis=0, keepdims=True))
            pt = jnp.exp2(st - m_new).astype(BF16)
            acc = jnp.exp2(m - m_new) * acc + jnp.dot(vt, pt, preferred_element_type=F32)
            state[kv] = (m_new, acc)

    for kv in range(C_KV):
        acc = state[kv][1]
        ot = acc[:HEAD_DIM] / acc[HEAD_DIM:HEAD_DIM + 1]
        for g in range(grp):
            h = kv * grp + g
            o_ref[:, h * HEAD_DIM:(h + 1) * HEAD_DIM] = ot[:, g * tq:(g + 1) * tq].T.astype(o_ref.dtype)


def _dense_gqa(pr, tq=512, tk=512):
    bsz, seq, _ = pr.shape
    kern = functools.partial(_gqa_kernel, tq=tq, tk=tk, nk=seq // tk)
    return pl.pallas_call(
        kern,
        grid=(bsz, seq // tq),
        in_specs=[
            pl.BlockSpec((None, tq, 256), lambda b, qi: (b, qi, 0)),
            pl.BlockSpec((None, tq, 256), lambda b, qi: (b, qi, 1)),
            pl.BlockSpec((None, seq, 128), lambda b, qi: (b, 0, 4)),
            pl.BlockSpec((None, seq, 128), lambda b, qi: (b, 0, 5)),
        ],
        out_specs=pl.BlockSpec((None, tq, BRANCH_W), lambda b, qi: (b, qi, 0)),
        out_shape=jax.ShapeDtypeStruct((bsz, seq, BRANCH_W), BF16),
        scratch_shapes=[pltpu.VMEM((2, C_KV, tk, (C_HEADS // C_KV) * tq), F32)],
        compiler_params=pltpu.CompilerParams(
            dimension_semantics=("parallel", "arbitrary"), vmem_limit_bytes=VMEM_LIMIT),
    )(pr, pr, pr, pr)


def _merge_kernel(*refs, tm):
    x_ref, oa_ref, ob_ref, oc_ref = refs[:4]
    d_refs = refs[4:10]
    gate_ref, merge_ref, wb_ref, wo_ref, out_ref, so_ref, sl_ref = refs[10:17]
    nslab = BRANCH_W // LANES

    def natural(ref, scr, r):
        if r == 1:
            return ref[0].astype(F32)
        for c in range(r):
            blk = ref[c].astype(F32)
            for s in range(nslab):
                scr[s, pl.ds(c, tm // r, stride=r), :] = blk[:, s * LANES:(s + 1) * LANES]
        return jnp.concatenate([scr[s] for s in range(nslab)], axis=1)

    os_, ls_ = [], []
    for g, (_, r) in enumerate(D_PAIRS):
        os_.append(natural(d_refs[2 * g], so_ref, r))
        ls_.append(natural(d_refs[2 * g + 1], sl_ref, r))
    mx = jnp.maximum(jnp.maximum(ls_[0], ls_[1]), ls_[2])
    es = [jnp.exp2(l - mx) for l in ls_]
    od = (es[0] * os_[0] + es[1] * os_[1] + es[2] * os_[2]) / (es[0] + es[1] + es[2])

    branches = [oa_ref[...].astype(F32), ob_ref[...].astype(F32), oc_ref[...].astype(F32), od]
    gated = [(branches[n] * gate_ref[:, n * BRANCH_W:(n + 1) * BRANCH_W].astype(F32)).astype(BF16)
             for n in range(N_BRANCH)]
    halves = []
    for half in range(2):
        cols = slice(half * 512, (half + 1) * 512)
        merged = None
        for n in range(N_BRANCH):
            y = jnp.dot(gated[n], wb_ref[n, :, cols], preferred_element_type=F32)
            term = merge_ref[:, n * D_MODEL + half * 512:n * D_MODEL + (half + 1) * 512].astype(F32) * y
            merged = term if merged is None else merged + term
        halves.append(merged.astype(BF16))
    merged = jnp.concatenate(halves, axis=1)
    out_ref[...] = x_ref[...] + jnp.dot(merged, wo_ref[...], preferred_element_type=F32)


def _merge(x, oa, ob, oc, d_outs, gate, merge, wb, wo, seq, tm=256):
    n = x.shape[0]
    ns = seq // tm
    row = pl.BlockSpec((tm, BRANCH_W), lambda i: (i, 0))
    in_specs = [pl.BlockSpec((tm, D_MODEL), lambda i: (i, 0)), row, row, row]
    for _, r in D_PAIRS:
        in_specs += [pl.BlockSpec((None, r, tm // r, BRANCH_W), lambda i: (i // ns, 0, i % ns, 0))] * 2
    in_specs += [pl.BlockSpec((tm, N_BRANCH * BRANCH_W), lambda i: (i, 0)),
                 pl.BlockSpec((tm, N_BRANCH * D_MODEL), lambda i: (i, 0)),
                 pl.BlockSpec((N_BRANCH, BRANCH_W, D_MODEL), lambda i: (0, 0, 0)),
                 pl.BlockSpec((D_MODEL, D_MODEL), lambda i: (0, 0))]
    return pl.pallas_call(
        functools.partial(_merge_kernel, tm=tm),
        grid=(n // tm,),
        in_specs=in_specs,
        out_specs=pl.BlockSpec((tm, D_MODEL), lambda i: (i, 0)),
        out_shape=jax.ShapeDtypeStruct((n, D_MODEL), F32),
        scratch_shapes=[pltpu.VMEM((BRANCH_W // LANES, tm, LANES), F32)] * 2,
        compiler_params=pltpu.CompilerParams(
            dimension_semantics=("parallel",), vmem_limit_bytes=VMEM_LIMIT),
    )(x, oa, ob, oc, *d_outs, gate, merge, wb, wo)


def kernel(x, w_in, w_branch, w_out, norm_gain, qk_gain, sink, lambda_vec, sub_norm_gain, rel_bias):
    bsz, seq, _ = x.shape
    n = bsz * seq
    depth = w_in.shape[0]
    rb = rel_bias.astype(F32)

    rows = seq // GRID_W
    row = jnp.repeat(jnp.arange(rows), GRID_W).astype(F32)
    col = jnp.tile(jnp.arange(GRID_W), rows).astype(F32)
    nf = HEAD_DIM // 4
    freqs = ROPE_THETA ** (-jnp.arange(nf, dtype=F32) / nf)
    ang = jnp.concatenate([row[:, None] * freqs, col[:, None] * freqs], axis=-1)
    cos2 = jnp.tile(jnp.cos(ang), (1, 4))
    sin2 = jnp.tile(jnp.concatenate([-jnp.sin(ang), jnp.sin(ang)], axis=-1), (1, 2))

    tq_a = 128
    bias_a = _band_bias(rb[:, :A_HEADS], 1, A_WIN, tq_a, tq_a + 2 * A_WIN, A_WIN, BAND_HPC)
    tq_b, tk_b = 1024, 512
    bias_b = _diff_bias(rb[:, A_HEADS:A_HEADS + B_HEADS], tq_b, tk_b)
    tq_d = 128
    bias_d = []
    for g, (win, r) in enumerate(D_PAIRS):
        lo = A_HEADS + B_HEADS + g * D_HEADS_PER_GROUP
        wd = min(tq_d + 128, seq // r)
        bias_d.append(_band_bias(rb[:, lo:lo + D_HEADS_PER_GROUP], r, win // (2 * r), tq_d, wd, 64, BAND_HPC))

    flag_np, rflag_np = _col_flags()
    flag = _permute_cols(jnp.asarray(flag_np))
    rflag = _permute_cols(jnp.asarray(rflag_np))
    blk = np.arange(NORM_W) // HEAD_DIM
    bd = jnp.asarray((blk[:, None] == blk[None, :]).astype(np.float32), dtype=BF16)

    xf = x.reshape(n, D_MODEL)
    for l in range(depth):
        h = _prenorm(xf, norm_gain[l].reshape(1, D_MODEL))
        w = _permute_cols(w_in[l].astype(BF16))
        gain = _permute_cols(_gain_cols(qk_gain[l]))
        norm_args = dict(gain=gain, flag=flag, bd=bd)
        pr = _proj_call(h, w, kind=K_ROPE, col0=P_ROPE, step=1, nj=2, seq=seq, rflag=rflag, cos=cos2, sin=sin2,
                        **norm_args).reshape(bsz, seq, -1)
        pn = _proj_call(h, w, kind=K_NORM, col0=P_NORM, step=1, nj=3, seq=seq, **norm_args).reshape(bsz, seq, -1)
        pbv = _proj_call(h, w, kind=K_RAW, col0=P_BV, step=1, nj=1, seq=seq).reshape(bsz, seq, -1)
        gate = _proj_call(h, w, kind=K_SILU, col0=OFF_GATE, step=1, nj=4, seq=seq)
        merge = _proj_call(h, w, kind=K_SIGMOID, col0=OFF_MERGE, step=1, nj=8, seq=seq)
        sink_rows = jnp.repeat(sink[l].astype(F32) * LOG2E, tq_a).reshape(A_KV, 1, BAND_HPC * tq_a)
        oa = _window_attention(pn, pr, bias_a, sink_rows, tq=tq_a, nsub=BAND_NSUB)
        lam_init = 0.8 - 0.6 * math.exp(-0.3 * l)
        ob = _diff_attention(pn, pbv, bias_b, lambda_vec[l], sub_norm_gain[l], lam_init, tq=tq_b, tk=tk_b)
        oc = _dense_gqa(pr)
        d_outs = []
        for g, (win, r) in enumerate(D_PAIRS):
            dqk = _proj_call(h, w, kind=K_NORM, col0=OFF_DQ + g * 512, step=(OFF_DK - OFF_DQ) // 512, nj=2,
                             seq=seq, r=r, **norm_args)
            dv = _proj_call(h, w, kind=K_RAW, col0=OFF_DV + g * 512, step=1, nj=1, seq=seq, r=r)
            if r == 1:
                dqk, dv = dqk.reshape(bsz, 1, seq, -1), dv.reshape(bsz, 1, seq, -1)
            d_outs += _dilated_group(dqk, dv, bias_d[g], tq=tq_d, nsub=BAND_NSUB)
        xf = _merge(xf, oa.reshape(n, -1), ob.reshape(n, -1), oc.reshape(n, -1), d_outs, gate, merge,
                    w_branch[l].astype(BF16), w_out[l].astype(BF16), seq)
    return xf.reshape(bsz, seq, D_MODEL)
```

```python
import functools
import math

import numpy as np
import jax
import jax.numpy as jnp
from jax import lax
from jax.experimental import pallas as pl
from jax.experimental.pallas import tpu as pltpu

F32 = jnp.float32
BF16 = jnp.bfloat16

D_MODEL = 1024
HEAD_DIM = 64
GRID_W = 64
EPS = 1e-6
A_HEADS, A_KV, A_WIN = 8, 2, 128
B_HEADS = 4
C_HEADS, C_KV = 8, 2
ROPE_THETA = 10000.0
D_PAIRS = ((128, 1), (512, 4), (2048, 16))
D_HEADS_PER_GROUP = 8
N_BRANCH = 4
BRANCH_W = 512
REL_BUCKETS = 32
REL_MAX_DIST = 1024
IN_WIDTH = 13824
NEG = -1e30
LOG2E = math.log2(math.e)
LANES = 128
ONES_ROWS = 16

OFF_AQ, OFF_AK, OFF_AV = 0, 512, 640
OFF_BQ, OFF_BK, OFF_BV = 768, 1280, 1792
OFF_CQ, OFF_CK, OFF_CV = 2304, 2816, 2944
OFF_DQ, OFF_DK, OFF_DV = 3072, 4608, 6144
OFF_GATE, OFF_MERGE = 7680, 9728
PERM = ((OFF_CQ, 768), (OFF_AK, 256), (OFF_AQ, 512), (OFF_BQ, 1024), (OFF_BV, 512))
P_ROPE, P_NORM, P_BV = 0, 1024, 2560

VMEM_LIMIT = 56 * 1024 * 1024

K_RAW, K_NORM, K_ROPE, K_SILU, K_SIGMOID = range(5)


def _permute_cols(a):
    return jnp.concatenate([a[..., lo:lo + n] for lo, n in PERM] + [a[..., OFF_DQ:]], axis=-1)


def _col_flags():
    norm = np.zeros((IN_WIDTH,), np.float32)
    for lo, hi in ((OFF_AQ, OFF_AV), (OFF_BQ, OFF_BV), (OFF_CQ, OFF_CV), (OFF_DQ, OFF_DV)):
        norm[lo:hi] = 1.0
    rope = np.zeros((IN_WIDTH,), np.float32)
    rope[OFF_CQ:OFF_CV] = 1.0
    return norm.reshape(1, IN_WIDTH), rope.reshape(1, IN_WIDTH)


def _gain_cols(g):
    sc = HEAD_DIM ** -0.5 * LOG2E
    one = lambda n: jnp.ones((n,), F32)
    parts = [jnp.tile(g[0, 0], A_HEADS) * sc, jnp.tile(g[0, 1], A_KV), one(128),
             jnp.tile(g[1, 0], 2 * B_HEADS) * sc, jnp.tile(g[1, 1], 2 * B_HEADS), one(512),
             jnp.tile(g[2, 0], C_HEADS) * sc, jnp.tile(g[2, 1], C_KV), one(128),
             jnp.tile(g[3, 0], 24) * sc, jnp.tile(g[3, 1], 24), one(1536),
             one(IN_WIDTH - OFF_GATE)]
    return jnp.concatenate(parts).reshape(1, IN_WIDTH)


def _prenorm_kernel(x_ref, g_ref, o_ref):
    x = x_ref[...]
    ms = jnp.mean(x * x, axis=-1, keepdims=True)
    o_ref[...] = (x * lax.rsqrt(ms + EPS) * g_ref[...]).astype(o_ref.dtype)


def _prenorm(x, g, tm=1024):
    n = x.shape[0]
    return pl.pallas_call(
        _prenorm_kernel,
        grid=(n // tm,),
        in_specs=[pl.BlockSpec((tm, D_MODEL), lambda i: (i, 0)),
                  pl.BlockSpec((1, D_MODEL), lambda i: (0, 0))],
        out_specs=pl.BlockSpec((tm, D_MODEL), lambda i: (i, 0)),
        out_shape=jax.ShapeDtypeStruct((n, D_MODEL), BF16),
        compiler_params=pltpu.CompilerParams(dimension_semantics=("parallel",), vmem_limit_bytes=VMEM_LIMIT),
    )(x, g)


PROJ_TM = 2048
PROJ_TN = 512
PROJ_CHUNK = 256
NORM_W = 256
DEINT_STRIDE = 4


def _proj_kernel(*refs, kind, r, tm, tn):
    refs = list(refs)
    h_ref, w_ref = refs[:2]
    pos = 2
    if kind in (K_NORM, K_ROPE):
        gain_ref, flag_ref, bd_ref = refs[pos:pos + 3]; pos += 3
    if kind == K_ROPE:
        rflag_ref, cos_ref, sin_ref = refs[pos:pos + 3]; pos += 3
    o_ref = refs[pos]; pos += 1
    slab_ref = refs[pos] if r > 1 else None
    slab2_ref = refs[pos + 1] if r > DEINT_STRIDE else None

    nchunk = tm // PROJ_CHUNK

    def main(rc):
        return jnp.dot(h_ref[rc * PROJ_CHUNK:(rc + 1) * PROJ_CHUNK, :], w_ref[...], preferred_element_type=F32)

    nxt = main(0)
    for rc in range(nchunk):
        rows = slice(rc * PROJ_CHUNK, (rc + 1) * PROJ_CHUNK)
        acc = nxt
        if rc + 1 < nchunk:
            nxt = main(rc + 1)
        if kind in (K_NORM, K_ROPE):
            sq = (acc * acc).astype(BF16)
            bd = bd_ref[...]
            ms = jnp.concatenate([jnp.dot(sq[:, c:c + NORM_W], bd, preferred_element_type=F32)
                                  for c in range(0, tn, NORM_W)], axis=1) * (1.0 / HEAD_DIM)
            y = acc * jnp.where(flag_ref[...] > 0.0, lax.rsqrt(ms + EPS), 1.0) * gain_ref[...]
            if kind == K_ROPE:
                c = jnp.concatenate([cos_ref[rows, :]] * (tn // LANES), axis=1)
                s = jnp.concatenate([sin_ref[rows, :]] * (tn // LANES), axis=1)
                lane = lax.broadcasted_iota(jnp.int32, y.shape, 1) & (HEAD_DIM - 1)
                half = HEAD_DIM // 2
                partner = jnp.where(lane < half, pltpu.roll(y, tn - half, axis=1), pltpu.roll(y, half, axis=1))
                y = jnp.where(rflag_ref[...] > 0.0, y * c + partner * s, y)
        elif kind == K_SILU:
            y = acc * (0.5 * jnp.tanh(0.5 * acc) + 0.5)
        elif kind == K_SIGMOID:
            y = 0.5 * jnp.tanh(0.5 * acc) + 0.5
        else:
            y = acc
        if r == 1:
            o_ref[rows, :] = y.astype(o_ref.dtype)
        else:
            for s_ in range(tn // LANES):
                slab_ref[s_, rows, :] = y[:, s_ * LANES:(s_ + 1) * LANES]
    if r > 1:
        r1 = min(r, DEINT_STRIDE)
        r2 = r // r1
        for s_ in range(tn // LANES):
            if r2 > 1:
                for c1 in range(r1):
                    slab2_ref[s_, c1 * (tm // r1):(c1 + 1) * (tm // r1), :] = (
                        slab_ref[s_, pl.ds(c1, tm // r1, stride=r1), :])
            for c1 in range(r1):
                for c2 in range(r2):
                    if r2 > 1:
                        rows_c = slab2_ref[s_, pl.ds(c1 * (tm // r1) + c2, tm // r, stride=r2), :]
                    else:
                        rows_c = slab_ref[s_, pl.ds(c1, tm // r, stride=r), :]
                    o_ref[c1 + r1 * c2, :, s_ * LANES:(s_ + 1) * LANES] = rows_c.astype(o_ref.dtype)


def _proj_call(h, w, *, kind, col0, step, nj, seq, r=1, gain=None, flag=None, rflag=None, bd=None, cos=None,
               sin=None):
    tm, tn = PROJ_TM, PROJ_TN
    n = h.shape[0]
    ns = seq // tm
    cb = col0 // tn
    col_spec = pl.BlockSpec((1, tn), lambda i, j: (0, cb + j * step))
    in_specs = [pl.BlockSpec((tm, D_MODEL), lambda i, j: (i, 0)),
                pl.BlockSpec((D_MODEL, tn), lambda i, j: (0, cb + j * step))]
    args = [h, w]
    if kind in (K_NORM, K_ROPE):
        in_specs += [col_spec, col_spec, pl.BlockSpec((NORM_W, NORM_W), lambda i, j: (0, 0))]
        args += [gain, flag, bd]
    if kind == K_ROPE:
        in_specs += [col_spec] + [pl.BlockSpec((tm, LANES), lambda i, j: (i % ns, 0))] * 2
        args += [rflag, cos, sin]
    scratch = []
    if r == 1:
        out_shape = jax.ShapeDtypeStruct((n, nj * tn), BF16)
        out_spec = pl.BlockSpec((tm, tn), lambda i, j: (i, j))
    else:
        out_shape = jax.ShapeDtypeStruct((n // seq, r, seq // r, nj * tn), BF16)
        out_spec = pl.BlockSpec((None, r, tm // r, tn), lambda i, j: (i // ns, 0, i % ns, j))
        scratch = [pltpu.VMEM((tn // LANES, tm, LANES), F32)] * (2 if r > DEINT_STRIDE else 1)
    return pl.pallas_call(
        functools.partial(_proj_kernel, kind=kind, r=r, tm=tm, tn=tn),
        grid=(n // tm, nj),
        in_specs=in_specs,
        out_specs=out_spec,
        out_shape=out_shape,
        scratch_shapes=scratch,
        compiler_params=pltpu.CompilerParams(
            dimension_semantics=("parallel", "parallel"), vmem_limit_bytes=VMEM_LIMIT),
    )(*args)


def _bucket(rel):
    nb = REL_BUCKETS // 2
    exact = nb // 2
    n = jnp.abs(rel)
    large = exact + (jnp.log(jnp.maximum(n, exact).astype(F32) / exact)
                     / math.log(REL_MAX_DIST / exact) * (nb - exact)).astype(jnp.int32)
    large = jnp.minimum(large, nb - 1)
    return jnp.where(rel > 0, nb, 0) + jnp.where(n < exact, n, large)


def _lookup(tbl, bucket):
    shape = (tbl.shape[1],) + (1,) * bucket.ndim
    out = jnp.zeros((tbl.shape[1],) + bucket.shape, F32)
    for j in range(REL_BUCKETS):
        out = jnp.where(bucket[None] == j, tbl[j].reshape(shape), out)
    return out


def _toeplitz(vec, n_rows, n_cols, start):
    nh = vec.shape[0]
    span = n_rows + n_cols - 1
    u = vec[:, start - (n_cols - 1):start + n_rows]
    ur = jnp.concatenate([u[:, ::-1], jnp.zeros((nh, 1), vec.dtype)], axis=1)
    g = jnp.tile(ur, (1, n_rows))[:, :n_rows * span].reshape(nh, n_rows, span)
    return g[:, :, n_rows - 1:]


def _band_bias(tbl, stride, half, tq, w, pad, hpc):
    nh = tbl.shape[1]
    r0 = w + tq
    rel = jnp.arange(-r0, r0 + 1)
    vec = jnp.where((jnp.abs(rel) <= half)[None], _lookup(tbl, _bucket(rel * stride)) * LOG2E, NEG)
    tiles = []
    for off in (0, pad, w - tq):
        b = _toeplitz(vec, w, tq, r0 - off)
        b = b.reshape(nh // hpc, hpc, w, tq).transpose(0, 2, 1, 3).reshape(nh // hpc, w, hpc * tq)
        tiles.append(b)
    return jnp.stack(tiles, axis=0)


BAND_HPC = 4
BAND_NSUB = 4


def _band_kernel(*refs, qi_axis, shared_kv, tq, nsub, w, pad, m_len, has_sink, lse_out):
    refs = list(refs)
    q_ref, k_ref, v_ref, bias_ref = refs[:4]
    pos = 4
    sink_ref = lse_ref = None
    if has_sink:
        sink_ref = refs[pos]; pos += 1
    o_ref = refs[pos]; pos += 1
    if lse_out:
        lse_ref = refs[pos]

    hpc = BAND_HPC
    kw = HEAD_DIM if shared_kv else hpc * HEAD_DIM
    last = m_len // tq - 1
    qi = pl.program_id(qi_axis)
    qt = q_ref[...].T
    ones = jnp.ones((ONES_ROWS, w), BF16)
    for j in range(nsub):
        qb = qi * nsub + j
        ks = pl.multiple_of(jnp.clip(qb * tq - pad, 0, m_len - w), 64)
        case = jnp.where(qb == 0, 0, jnp.where(qb == last, 2, 1))
        for c in range(q_ref.shape[1] // (hpc * HEAD_DIM)):
            slab = qt[c * hpc * HEAD_DIM:(c + 1) * hpc * HEAD_DIM, j * tq:(j + 1) * tq]
            if shared_kv:
                qc = jnp.concatenate([slab[g * HEAD_DIM:(g + 1) * HEAD_DIM] for g in range(hpc)], axis=1)
            else:
                head = lax.broadcasted_iota(jnp.int32, slab.shape, 0) // HEAD_DIM
                qc = jnp.concatenate([jnp.where(head == g, slab, jnp.zeros_like(slab)) for g in range(hpc)],
                                     axis=1)
            k = k_ref[pl.ds(ks, w), c * kw:(c + 1) * kw]
            v = v_ref[pl.ds(ks, w), c * kw:(c + 1) * kw]
            vt = jnp.concatenate([v.T, ones], axis=0)
            st = jnp.dot(k, qc, preferred_element_type=F32) + bias_ref[case, c]
            m = jnp.max(st, axis=0, keepdims=True)
            if has_sink:
                sk = sink_ref[c]
                m = jnp.maximum(m, sk)
            pt = jnp.exp2(st - m).astype(BF16)
            acc = jnp.dot(vt, pt, preferred_element_type=F32)
            l = acc[kw:kw + 1]
            if has_sink:
                l = l + jnp.exp2(sk - m)
            lse = m + jnp.log2(l)
            for g in range(hpc):
                h = c * hpc + g
                lanes = slice(g * tq, (g + 1) * tq)
                rows = slice(0, HEAD_DIM) if shared_kv else slice(g * HEAD_DIM, (g + 1) * HEAD_DIM)
                out_rows = slice(j * tq, (j + 1) * tq)
                cols = slice(h * HEAD_DIM, (h + 1) * HEAD_DIM)
                ot = acc[rows, lanes] / l[:, lanes]
                o_ref[out_rows, cols] = ot.T.astype(o_ref.dtype)
                if lse_out:
                    lse_ref[out_rows, cols] = jnp.broadcast_to(lse[:, lanes], (HEAD_DIM, tq)).T


def _window_attention(pn, pr, bias, sink_rows, tq, nsub):
    bsz, seq, _ = pn.shape
    pad = A_WIN
    w = tq + 2 * pad
    lanes = BAND_HPC * tq
    kern = functools.partial(_band_kernel, qi_axis=1, shared_kv=True, tq=tq, nsub=nsub, w=w, pad=pad, m_len=seq,
                             has_sink=True, lse_out=False)
    return pl.pallas_call(
        kern,
        grid=(bsz, seq // (nsub * tq)),
        in_specs=[
            pl.BlockSpec((None, nsub * tq, 512), lambda b, qi: (b, qi, 0)),
            pl.BlockSpec((None, seq, 128), lambda b, qi: (b, 0, 6)),
            pl.BlockSpec((None, seq, 128), lambda b, qi: (b, 0, 7)),
            pl.BlockSpec((3, A_KV, w, lanes), lambda b, qi: (0, 0, 0, 0)),
            pl.BlockSpec((A_KV, 1, lanes), lambda b, qi: (0, 0, 0)),
        ],
        out_specs=pl.BlockSpec((None, nsub * tq, BRANCH_W), lambda b, qi: (b, qi, 0)),
        out_shape=jax.ShapeDtypeStruct((bsz, seq, BRANCH_W), BF16),
        compiler_params=pltpu.CompilerParams(
            dimension_semantics=("parallel", "arbitrary"), vmem_limit_bytes=VMEM_LIMIT),
    )(pn, pr, pr, bias, sink_rows)


def _dilated_group(qk, v, bias, tq, nsub):
    bsz, r, m_len, _ = v.shape
    pad = 64
    w = min(tq + 2 * pad, m_len)
    nsub = min(nsub, m_len // tq)
    nchain = D_HEADS_PER_GROUP // BAND_HPC
    kern = functools.partial(_band_kernel, qi_axis=2, shared_kv=False, tq=tq, nsub=nsub, w=w, pad=pad,
                             m_len=m_len, has_sink=False, lse_out=True)
    row_spec = pl.BlockSpec((None, None, nsub * tq, BRANCH_W), lambda b, c, qi: (b, c, qi, 0))
    return pl.pallas_call(
        kern,
        grid=(bsz, r, m_len // (nsub * tq)),
        in_specs=[
            row_spec,
            pl.BlockSpec((None, None, m_len, 512), lambda b, c, qi: (b, c, 0, 1)),
            pl.BlockSpec((None, None, m_len, 512), lambda b, c, qi: (b, c, 0, 0)),
            pl.BlockSpec((3, nchain, w, BAND_HPC * tq), lambda b, c, qi: (0, 0, 0, 0)),
        ],
        out_specs=[row_spec, row_spec],
        out_shape=[jax.ShapeDtypeStruct((bsz, r, m_len, BRANCH_W), BF16),
                   jax.ShapeDtypeStruct((bsz, r, m_len, BRANCH_W), F32)],
        compiler_params=pltpu.CompilerParams(
            dimension_semantics=("parallel", "parallel", "arbitrary"), vmem_limit_bytes=VMEM_LIMIT),
    )(qk, qk, v, bias)


DIFF_UNIT = 512


def _saturation_distance():
    nb = REL_BUCKETS // 2
    exact = nb // 2
    n = np.arange(exact, 4 * REL_MAX_DIST)
    large = exact + (np.log(n.astype(np.float32) / exact) / math.log(REL_MAX_DIST / exact)
                     * (nb - exact)).astype(np.int32)
    return int(n[large < nb - 1].max()) + 1 + 2


def _ceil_div(a, b):
    return -(-a // b)


def _diff_tiles(tq, tk):
    far = _saturation_distance()
    return -_ceil_div(far + tk - 1, DIFF_UNIT), _ceil_div(far + tq - 1, DIFF_UNIT)


def _diff_kernel(lv_ref, sg_ref, q_ref, k_ref, v_ref, bias_ref, o_ref, st_ref, *, tq, tk, nk, lo, hi, lam_init):
    qi = pl.program_id(2)
    vd = 2 * HEAD_DIM
    qt = q_ref[...].T
    first = lax.broadcasted_iota(jnp.int32, (vd, tq), 0) < HEAD_DIM
    zero = jnp.zeros((vd, tq), BF16)
    qbd = jnp.concatenate([jnp.where(first, qt, zero), jnp.where(first, zero, qt)], axis=1)
    m = jnp.full((1, 2 * tq), NEG, F32)
    acc = jnp.zeros((vd + ONES_ROWS, 2 * tq), F32)
    ones = jnp.ones((ONES_ROWS, tk), BF16)

    def scores(t):
        dd = jnp.clip(t * (tk // DIFF_UNIT) - qi * (tq // DIFF_UNIT), lo, hi) - lo
        b = bias_ref[dd]
        st = jnp.dot(k_ref[t * tk:(t + 1) * tk, :], qbd, preferred_element_type=F32)
        return jnp.concatenate([st[:, :tq] + b, st[:, tq:] + b], axis=1)

    st_ref[0] = scores(0)
    for t in range(nk):
        if t + 1 < nk:
            st_ref[(t + 1) % 2] = scores(t + 1)
        st = st_ref[t % 2]
        vt = jnp.concatenate([v_ref[t * tk:(t + 1) * tk, :].T, ones], axis=0)
        m_new = jnp.maximum(m, jnp.max(st, axis=0, keepdims=True))
        pt = jnp.exp2(st - m_new).astype(BF16)
        acc = jnp.exp2(m - m_new) * acc + jnp.dot(vt, pt, preferred_element_type=F32)
        m = m_new

    lv = lv_ref[...]
    lam = (jnp.exp(jnp.sum(lv[0:1] * lv[1:2], axis=-1, keepdims=True))
           - jnp.exp(jnp.sum(lv[2:3] * lv[3:4], axis=-1, keepdims=True)) + lam_init)
    ot = acc[:vd, :tq] / acc[vd:vd + 1, :tq] - lam * (acc[:vd, tq:] / acc[vd:vd + 1, tq:])
    ms = jnp.mean(ot * ot, axis=0, keepdims=True)
    ot = ot * lax.rsqrt(ms + EPS) * sg_ref[...] * (1.0 - lam_init)
    o_ref[...] = ot.T.astype(o_ref.dtype)


def _diff_bias(tbl, tq, tk):
    lo, hi = _diff_tiles(tq, tk)
    r0 = max(-lo, hi) * DIFF_UNIT + tk + tq
    vec = _lookup(tbl, _bucket(jnp.arange(-r0, r0 + 1))) * LOG2E
    return jnp.stack([_toeplitz(vec, tk, tq, r0 + d * DIFF_UNIT) for d in range(lo, hi + 1)], axis=1)


def _diff_attention(pn, pbv, bias, lam_vec, sub_gain, lam_init, tq, tk):
    bsz, seq, _ = pn.shape
    lo, hi = _diff_tiles(tq, tk)
    nt = hi - lo + 1
    vd = 2 * HEAD_DIM
    kern = functools.partial(_diff_kernel, tq=tq, tk=tk, nk=seq // tk, lo=lo, hi=hi, lam_init=lam_init)
    return pl.pallas_call(
        kern,
        grid=(B_HEADS, bsz, seq // tq),
        in_specs=[
            pl.BlockSpec((4, HEAD_DIM), lambda h, b, qi: (0, 0)),
            pl.BlockSpec((vd, 1), lambda h, b, qi: (0, 0)),
            pl.BlockSpec((None, tq, vd), lambda h, b, qi: (b, qi, 512 // vd + h)),
            pl.BlockSpec((None, seq, vd), lambda h, b, qi: (b, 0, 1024 // vd + h)),
            pl.BlockSpec((None, seq, vd), lambda h, b, qi: (b, 0, h)),
            pl.BlockSpec((None, nt, tk, tq), lambda h, b, qi: (h, 0, 0, 0), pipeline_mode=pl.Buffered(1)),
        ],
        out_specs=pl.BlockSpec((None, tq, vd), lambda h, b, qi: (b, qi, h)),
        out_shape=jax.ShapeDtypeStruct((bsz, seq, BRANCH_W), BF16),
        scratch_shapes=[pltpu.VMEM((2, tk, 2 * tq), F32)],
        compiler_params=pltpu.CompilerParams(
            dimension_semantics=("parallel", "parallel", "arbitrary"), vmem_limit_bytes=VMEM_LIMIT),
    )(lam_vec, sub_gain.reshape(vd, 1), pn, pn, pbv, bias)


def _gqa_kernel(q0_ref, q1_ref, k_ref, v_ref, o_ref, st_ref, *, tq, tk, nk):
    grp = C_HEADS // C_KV
    qts = []
    for q_ref in (q0_ref, q1_ref):
        qt = q_ref[...].T
        qts.append(jnp.concatenate([qt[g * HEAD_DIM:(g + 1) * HEAD_DIM] for g in range(grp)], axis=1))
    ones = jnp.ones((ONES_ROWS, tk), BF16)
    state = [(jnp.full((1, grp * tq), NEG, F32), jnp.zeros((HEAD_DIM + ONES_ROWS, grp * tq), F32))
             for _ in range(C_KV)]

    def scores(t, slot):
        for kv in range(C_KV):
            k = k_ref[t * tk:(t + 1) * tk, kv * HEAD_DIM:(kv + 1) * HEAD_DIM]
            st_ref[slot, kv] = jnp.dot(k, qts[kv], preferred_element_type=F32)

    scores(0, 0)
    for t in range(nk):
        if t + 1 < nk:
            scores(t + 1, (t + 1) % 2)
        for kv in range(C_KV):
            m, acc = state[kv]
            st = st_ref[t % 2, kv]
            v = v_ref[t * tk:(t + 1) * tk, kv * HEAD_DIM:(kv + 1) * HEAD_DIM]
            vt = jnp.concatenate([v.T, ones], axis=0)
            m_new = jnp.maximum(m, jnp.max(st, axis=0, keepdims=True))
            pt = jnp.exp2(st - m_new).astype(BF16)
            acc = jnp.exp2(m - m_new) * acc + jnp.dot(vt, pt, preferred_element_type=F32)
            state[kv] = (m_new, acc)

    for kv in range(C_KV):
        acc = state[kv][1]
        ot = acc[:HEAD_DIM] / acc[HEAD_DIM:HEAD_DIM + 1]
        for g in range(grp):
            h = kv * grp + g
            o_ref[:, h * HEAD_DIM:(h + 1) * HEAD_DIM] = ot[:, g * tq:(g + 1) * tq].T.astype(o_ref.dtype)


def _dense_gqa(pr, tq=512, tk=512):
    bsz, seq, _ = pr.shape
    kern = functools.partial(_gqa_kernel, tq=tq, tk=tk, nk=seq // tk)
    return pl.pallas_call(
        kern,
        grid=(bsz, seq // tq),
        in_specs=[
            pl.BlockSpec((None, tq, 256), lambda b, qi: (b, qi, 0)),
            pl.BlockSpec((None, tq, 256), lambda b, qi: (b, qi, 1)),
            pl.BlockSpec((None, seq, 128), lambda b, qi: (b, 0, 4)),
            pl.BlockSpec((None, seq, 128), lambda b, qi: (b, 0, 5)),
        ],
        out_specs=pl.BlockSpec((None, tq, BRANCH_W), lambda b, qi: (b, qi, 0)),
        out_shape=jax.ShapeDtypeStruct((bsz, seq, BRANCH_W), BF16),
        scratch_shapes=[pltpu.VMEM((2, C_KV, tk, (C_HEADS // C_KV) * tq), F32)],
        compiler_params=pltpu.CompilerParams(
            dimension_semantics=("parallel", "arbitrary"), vmem_limit_bytes=VMEM_LIMIT),
    )(pr, pr, pr, pr)


def _merge_kernel(*refs, tm):
    x_ref, oa_ref, ob_ref, oc_ref = refs[:4]
    d_refs = refs[4:10]
    gate_ref, merge_ref, wb_ref, wo_ref, out_ref, so_ref, sl_ref = refs[10:17]
    nslab = BRANCH_W // LANES

    def natural(ref, scr, r):
        if r == 1:
            return ref[0].astype(F32)
        for c in range(r):
            blk = ref[c].astype(F32)
            for s in range(nslab):
                scr[s, pl.ds(c, tm // r, stride=r), :] = blk[:, s * LANES:(s + 1) * LANES]
        return jnp.concatenate([scr[s] for s in range(nslab)], axis=1)

    os_, ls_ = [], []
    for g, (_, r) in enumerate(D_PAIRS):
        os_.append(natural(d_refs[2 * g], so_ref, r))
        ls_.append(natural(d_refs[2 * g + 1], sl_ref, r))
    mx = jnp.maximum(jnp.maximum(ls_[0], ls_[1]), ls_[2])
    es = [jnp.exp2(l - mx) for l in ls_]
    od = (es[0] * os_[0] + es[1] * os_[1] + es[2] * os_[2]) / (es[0] + es[1] + es[2])

    branches = [oa_ref[...].astype(F32), ob_ref[...].astype(F32), oc_ref[...].astype(F32), od]
    gated = [(branches[n] * gate_ref[:, n * BRANCH_W:(n + 1) * BRANCH_W].astype(F32)).astype(BF16)
             for n in range(N_BRANCH)]
    halves = []
    for half in range(2):
        cols = slice(half * 512, (half + 1) * 512)
        merged = None
        for n in range(N_BRANCH):
            y = jnp.dot(gated[n], wb_ref[n, :, cols], preferred_element_type=F32)
            term = merge_ref[:, n * D_MODEL + half * 512:n * D_MODEL + (half + 1) * 512].astype(F32) * y
            merged = term if merged is None else merged + term
        halves.append(merged.astype(BF16))
    merged = jnp.concatenate(halves, axis=1)
    out_ref[...] = x_ref[...] + jnp.dot(merged, wo_ref[...], preferred_element_type=F32)


def _merge(x, oa, ob, oc, d_outs, gate, merge, wb, wo, seq, tm=256):
    n = x.shape[0]
    ns = seq // tm
    row = pl.BlockSpec((tm, BRANCH_W), lambda i: (i, 0))
    in_specs = [pl.BlockSpec((tm, D_MODEL), lambda i: (i, 0)), row, row, row]
    for _, r in D_PAIRS:
        in_specs += [pl.BlockSpec((None, r, tm // r, BRANCH_W), lambda i: (i // ns, 0, i % ns, 0))] * 2
    in_specs += [pl.BlockSpec((tm, N_BRANCH * BRANCH_W), lambda i: (i, 0)),
                 pl.BlockSpec((tm, N_BRANCH * D_MODEL), lambda i: (i, 0)),
                 pl.BlockSpec((N_BRANCH, BRANCH_W, D_MODEL), lambda i: (0, 0, 0)),
                 pl.BlockSpec((D_MODEL, D_MODEL), lambda i: (0, 0))]
    return pl.pallas_call(
        functools.partial(_merge_kernel, tm=tm),
        grid=(n // tm,),
        in_specs=in_specs,
        out_specs=pl.BlockSpec((tm, D_MODEL), lambda i: (i, 0)),
        out_shape=jax.ShapeDtypeStruct((n, D_MODEL), F32),
        scratch_shapes=[pltpu.VMEM((BRANCH_W // LANES, tm, LANES), F32)] * 2,
        compiler_params=pltpu.CompilerParams(
            dimension_semantics=("parallel",), vmem_limit_bytes=VMEM_LIMIT),
    )(x, oa, ob, oc, *d_outs, gate, merge, wb, wo)


def kernel(x, w_in, w_branch, w_out, norm_gain, qk_gain, sink, lambda_vec, sub_norm_gain, rel_bias):
    bsz, seq, _ = x.shape
    n = bsz * seq
    depth = w_in.shape[0]
    rb = rel_bias.astype(F32)

    rows = seq // GRID_W
    row = jnp.repeat(jnp.arange(rows), GRID_W).astype(F32)
    col = jnp.tile(jnp.arange(GRID_W), rows).astype(F32)
    nf = HEAD_DIM // 4
    freqs = ROPE_THETA ** (-jnp.arange(nf, dtype=F32) / nf)
    ang = jnp.concatenate([row[:, None] * freqs, col[:, None] * freqs], axis=-1)
    cos2 = jnp.tile(jnp.cos(ang), (1, 4))
    sin2 = jnp.tile(jnp.concatenate([-jnp.sin(ang), jnp.sin(ang)], axis=-1), (1, 2))

    tq_a = 128
    bias_a = _band_bias(rb[:, :A_HEADS], 1, A_WIN, tq_a, tq_a + 2 * A_WIN, A_WIN, BAND_HPC)
    tq_b, tk_b = 1024, 512
    bias_b = _diff_bias(rb[:, A_HEADS:A_HEADS + B_HEADS], tq_b, tk_b)
    tq_d = 128
    bias_d = []
    for g, (win, r) in enumerate(D_PAIRS):
        lo = A_HEADS + B_HEADS + g * D_HEADS_PER_GROUP
        wd = min(tq_d + 128, seq // r)
        bias_d.append(_band_bias(rb[:, lo:lo + D_HEADS_PER_GROUP], r, win // (2 * r), tq_d, wd, 64, BAND_HPC))

    flag_np, rflag_np = _col_flags()
    flag = _permute_cols(jnp.asarray(flag_np))
    rflag = _permute_cols(jnp.asarray(rflag_np))
    blk = np.arange(NORM_W) // HEAD_DIM
    bd = jnp.asarray((blk[:, None] == blk[None, :]).astype(np.float32), dtype=BF16)

    xf = x.reshape(n, D_MODEL)
    for l in range(depth):
        h = _prenorm(xf, norm_gain[l].reshape(1, D_MODEL))
        w = _permute_cols(w_in[l].astype(BF16))
        gain = _permute_cols(_gain_cols(qk_gain[l]))
        norm_args = dict(gain=gain, flag=flag, bd=bd)
        pr = _proj_call(h, w, kind=K_ROPE, col0=P_ROPE, step=1, nj=2, seq=seq, rflag=rflag, cos=cos2, sin=sin2,
                        **norm_args).reshape(bsz, seq, -1)
        pn = _proj_call(h, w, kind=K_NORM, col0=P_NORM, step=1, nj=3, seq=seq, **norm_args).reshape(bsz, seq, -1)
        pbv = _proj_call(h, w, kind=K_RAW, col0=P_BV, step=1, nj=1, seq=seq).reshape(bsz, seq, -1)
        gate = _proj_call(h, w, kind=K_SILU, col0=OFF_GATE, step=1, nj=4, seq=seq)
        merge = _proj_call(h, w, kind=K_SIGMOID, col0=OFF_MERGE, step=1, nj=8, seq=seq)
        sink_rows = jnp.repeat(sink[l].astype(F32) * LOG2E, tq_a).reshape(A_KV, 1, BAND_HPC * tq_a)
        oa = _window_attention(pn, pr, bias_a, sink_rows, tq=tq_a, nsub=BAND_NSUB)
        lam_init = 0.8 - 0.6 * math.exp(-0.3 * l)
        ob = _diff_attention(pn, pbv, bias_b, lambda_vec[l], sub_norm_gain[l], lam_init, tq=tq_b, tk=tk_b)
        oc = _dense_gqa(pr)
        d_outs = []
        for g, (win, r) in enumerate(D_PAIRS):
            dqk = _proj_call(h, w, kind=K_NORM, col0=OFF_DQ + g * 512, step=(OFF_DK - OFF_DQ) // 512, nj=2,
                             seq=seq, r=r, **norm_args)
            dv = _proj_call(h, w, kind=K_RAW, col0=OFF_DV + g * 512, step=1, nj=1, seq=seq, r=r)
            if r == 1:
                dqk, dv = dqk.reshape(bsz, 1, seq, -1), dv.reshape(bsz, 1, seq, -1)
            d_outs += _dilated_group(dqk, dv, bias_d[g], tq=tq_d, nsub=BAND_NSUB)
        xf = _merge(xf, oa.reshape(n, -1), ob.reshape(n, -1), oc.reshape(n, -1), d_outs, gate, merge,
                    w_branch[l].astype(BF16), w_out[l].astype(BF16), seq)
    return xf.reshape(bsz, seq, D_MODEL)
```

```python
import functools
import math

import numpy as np
import jax
import jax.numpy as jnp
from jax import lax
from jax.experimental import pallas as pl
from jax.experimental.pallas import tpu as pltpu

F32 = jnp.float32
BF16 = jnp.bfloat16

D_MODEL = 1024
HEAD_DIM = 64
GRID_W = 64
EPS = 1e-6
A_HEADS, A_KV, A_WIN = 8, 2, 128
B_HEADS = 4
C_HEADS, C_KV = 8, 2
ROPE_THETA = 10000.0
D_PAIRS = ((128, 1), (512, 4), (2048, 16))
D_HEADS_PER_GROUP = 8
N_BRANCH = 4
BRANCH_W = 512
REL_BUCKETS = 32
REL_MAX_DIST = 1024
IN_WIDTH = 13824
NEG = -1e30
LOG2E = math.log2(math.e)
LANES = 128
ONES_ROWS = 16

OFF_AQ, OFF_AK, OFF_AV = 0, 512, 640
OFF_BQ, OFF_BK, OFF_BV = 768, 1280, 1792
OFF_CQ, OFF_CK, OFF_CV = 2304, 2816, 2944
OFF_DQ, OFF_DK, OFF_DV = 3072, 4608, 6144
OFF_GATE, OFF_MERGE = 7680, 9728
PERM = ((OFF_CQ, 768), (OFF_AK, 256), (OFF_AQ, 512), (OFF_BQ, 1024), (OFF_BV, 512))
P_ROPE, P_NORM, P_BV = 0, 1024, 2560

VMEM_LIMIT = 56 * 1024 * 1024

K_RAW, K_NORM, K_ROPE, K_SILU, K_SIGMOID = range(5)


def _permute_cols(a):
    return jnp.concatenate([a[..., lo:lo + n] for lo, n in PERM] + [a[..., OFF_DQ:]], axis=-1)


def _col_flags():
    norm = np.zeros((IN_WIDTH,), np.float32)
    for lo, hi in ((OFF_AQ, OFF_AV), (OFF_BQ, OFF_BV), (OFF_CQ, OFF_CV), (OFF_DQ, OFF_DV)):
        norm[lo:hi] = 1.0
    rope = np.zeros((IN_WIDTH,), np.float32)
    rope[OFF_CQ:OFF_CV] = 1.0
    return norm.reshape(1, IN_WIDTH), rope.reshape(1, IN_WIDTH)


def _gain_cols(g):
    sc = HEAD_DIM ** -0.5 * LOG2E
    one = lambda n: jnp.ones((n,), F32)
    parts = [jnp.tile(g[0, 0], A_HEADS) * sc, jnp.tile(g[0, 1], A_KV), one(128),
             jnp.tile(g[1, 0], 2 * B_HEADS) * sc, jnp.tile(g[1, 1], 2 * B_HEADS), one(512),
             jnp.tile(g[2, 0], C_HEADS) * sc, jnp.tile(g[2, 1], C_KV), one(128),
             jnp.tile(g[3, 0], 24) * sc, jnp.tile(g[3, 1], 24), one(1536),
             one(IN_WIDTH - OFF_GATE)]
    return jnp.concatenate(parts).reshape(1, IN_WIDTH)


def _prenorm_kernel(x_ref, g_ref, o_ref):
    x = x_ref[...]
    ms = jnp.mean(x * x, axis=-1, keepdims=True)
    o_ref[...] = (x * lax.rsqrt(ms + EPS) * g_ref[...]).astype(o_ref.dtype)


def _prenorm(x, g, tm=1024):
    n = x.shape[0]
    return pl.pallas_call(
        _prenorm_kernel,
        grid=(n // tm,),
        in_specs=[pl.BlockSpec((tm, D_MODEL), lambda i: (i, 0)),
                  pl.BlockSpec((1, D_MODEL), lambda i: (0, 0))],
        out_specs=pl.BlockSpec((tm, D_MODEL), lambda i: (i, 0)),
        out_shape=jax.ShapeDtypeStruct((n, D_MODEL), BF16),
        compiler_params=pltpu.CompilerParams(dimension_semantics=("parallel",), vmem_limit_bytes=VMEM_LIMIT),
    )(x, g)


PROJ_TM = 2048
PROJ_TN = 512
PROJ_CHUNK = 256
NORM_W = 256
DEINT_STRIDE = 4


def _proj_kernel(*refs, kind, r, tm, tn):
    refs = list(refs)
    h_ref, w_ref = refs[:2]
    pos = 2
    if kind in (K_NORM, K_ROPE):
        gain_ref, flag_ref, bd_ref = refs[pos:pos + 3]; pos += 3
    if kind == K_ROPE:
        rflag_ref, cos_ref, sin_ref = refs[pos:pos + 3]; pos += 3
    o_ref = refs[pos]; pos += 1
    slab_ref = refs[pos] if r > 1 else None
    slab2_ref = refs[pos + 1] if r > DEINT_STRIDE else None

    nchunk = tm // PROJ_CHUNK

    def main(rc):
        return jnp.dot(h_ref[rc * PROJ_CHUNK:(rc + 1) * PROJ_CHUNK, :], w_ref[...], preferred_element_type=F32)

    nxt = main(0)
    for rc in range(nchunk):
        rows = slice(rc * PROJ_CHUNK, (rc + 1) * PROJ_CHUNK)
        acc = nxt
        if rc + 1 < nchunk:
            nxt = main(rc + 1)
        if kind in (K_NORM, K_ROPE):
            sq = (acc * acc).astype(BF16)
            bd = bd_ref[...]
            ms = jnp.concatenate([jnp.dot(sq[:, c:c + NORM_W], bd, preferred_element_type=F32)
                                  for c in range(0, tn, NORM_W)], axis=1) * (1.0 / HEAD_DIM)
            y = acc * jnp.where(flag_ref[...] > 0.0, lax.rsqrt(ms + EPS), 1.0) * gain_ref[...]
            if kind == K_ROPE:
                c = jnp.concatenate([cos_ref[rows, :]] * (tn // LANES), axis=1)
                s = jnp.concatenate([sin_ref[rows, :]] * (tn // LANES), axis=1)
                lane = lax.broadcasted_iota(jnp.int32, y.shape, 1) & (HEAD_DIM - 1)
                half = HEAD_DIM // 2
                partner = jnp.where(lane < half, pltpu.roll(y, tn - half, axis=1), pltpu.roll(y, half, axis=1))
                y = jnp.where(rflag_ref[...] > 0.0, y * c + partner * s, y)
        elif kind == K_SILU:
            y = acc * (0.5 * jnp.tanh(0.5 * acc) + 0.5)
        elif kind == K_SIGMOID:
            y = 0.5 * jnp.tanh(0.5 * acc) + 0.5
        else:
            y = acc
        if r == 1:
            o_ref[rows, :] = y.astype(o_ref.dtype)
        else:
            for s_ in range(tn // LANES):
                slab_ref[s_, rows, :] = y[:, s_ * LANES:(s_ + 1) * LANES]
    if r > 1:
        r1 = min(r, DEINT_STRIDE)
        r2 = r // r1
        for s_ in range(tn // LANES):
            if r2 > 1:
                for c1 in range(r1):
                    slab2_ref[s_, c1 * (tm // r1):(c1 + 1) * (tm // r1), :] = (
                        slab_ref[s_, pl.ds(c1, tm // r1, stride=r1), :])
            for c1 in range(r1):
                for c2 in range(r2):
                    if r2 > 1:
                        rows_c = slab2_ref[s_, pl.ds(c1 * (tm // r1) + c2, tm // r, stride=r2), :]
                    else:
                        rows_c = slab_ref[s_, pl.ds(c1, tm // r, stride=r), :]
                    o_ref[c1 + r1 * c2, :, s_ * LANES:(s_ + 1) * LANES] = rows_c.astype(o_ref.dtype)


def _proj_call(h, w, *, kind, col0, step, nj, seq, r=1, gain=None, flag=None, rflag=None, bd=None, cos=None,
               sin=None):
    tm, tn = PROJ_TM, PROJ_TN
    n = h.shape[0]
    ns = seq // tm
    cb = col0 // tn
    col_spec = pl.BlockSpec((1, tn), lambda i, j: (0, cb + j * step))
    in_specs = [pl.BlockSpec((tm, D_MODEL), lambda i, j: (i, 0)),
                pl.BlockSpec((D_MODEL, tn), lambda i, j: (0, cb + j * step))]
    args = [h, w]
    if kind in (K_NORM, K_ROPE):
        in_specs += [col_spec, col_spec, pl.BlockSpec((NORM_W, NORM_W), lambda i, j: (0, 0))]
        args += [gain, flag, bd]
    if kind == K_ROPE:
        in_specs += [col_spec] + [pl.BlockSpec((tm, LANES), lambda i, j: (i % ns, 0))] * 2
        args += [rflag, cos, sin]
    scratch = []
    if r == 1:
        out_shape = jax.ShapeDtypeStruct((n, nj * tn), BF16)
        out_spec = pl.BlockSpec((tm, tn), lambda i, j: (i, j))
    else:
        out_shape = jax.ShapeDtypeStruct((n // seq, r, seq // r, nj * tn), BF16)
        out_spec = pl.BlockSpec((None, r, tm // r, tn), lambda i, j: (i // ns, 0, i % ns, j))
        scratch = [pltpu.VMEM((tn // LANES, tm, LANES), F32)] * (2 if r > DEINT_STRIDE else 1)
    return pl.pallas_call(
        functools.partial(_proj_kernel, kind=kind, r=r, tm=tm, tn=tn),
        grid=(n // tm, nj),
        in_specs=in_specs,
        out_specs=out_spec,
        out_shape=out_shape,
        scratch_shapes=scratch,
        compiler_params=pltpu.CompilerParams(
            dimension_semantics=("parallel", "parallel"), vmem_limit_bytes=VMEM_LIMIT),
    )(*args)


def _bucket(rel):
    nb = REL_BUCKETS // 2
    exact = nb // 2
    n = jnp.abs(rel)
    large = exact + (jnp.log(jnp.maximum(n, exact).astype(F32) / exact)
                     / math.log(REL_MAX_DIST / exact) * (nb - exact)).astype(jnp.int32)
    large = jnp.minimum(large, nb - 1)
    return jnp.where(rel > 0, nb, 0) + jnp.where(n < exact, n, large)


def _lookup(tbl, bucket):
    shape = (tbl.shape[1],) + (1,) * bucket.ndim
    out = jnp.zeros((tbl.shape[1],) + bucket.shape, F32)
    for j in range(REL_BUCKETS):
        out = jnp.where(bucket[None] == j, tbl[j].reshape(shape), out)
    return out


def _toeplitz(vec, n_rows, n_cols, start):
    nh = vec.shape[0]
    span = n_rows + n_cols - 1
    u = vec[:, start - (n_cols - 1):start + n_rows]
    ur = jnp.concatenate([u[:, ::-1], jnp.zeros((nh, 1), vec.dtype)], axis=1)
    g = jnp.tile(ur, (1, n_rows))[:, :n_rows * span].reshape(nh, n_rows, span)
    return g[:, :, n_rows - 1:]


def _band_bias(tbl, stride, half, tq, w, pad, hpc):
    nh = tbl.shape[1]
    r0 = w + tq
    rel = jnp.arange(-r0, r0 + 1)
    vec = jnp.where((jnp.abs(rel) <= half)[None], _lookup(tbl, _bucket(rel * stride)) * LOG2E, NEG)
    offs = (0, pad, w - tq)
    omax = max(offs)
    big = _toeplitz(vec, w + omax, tq, r0 - omax)
    tiles = []
    for off in offs:
        b = big[:, omax - off:omax - off + w]
        tiles.append(b.reshape(nh // hpc, hpc, w, tq).transpose(0, 2, 1, 3).reshape(nh // hpc, w, hpc * tq))
    return jnp.stack(tiles, axis=0)


BAND_HPC = 4
BAND_NSUB = 4


def _band_kernel(*refs, qi_axis, shared_kv, tq, nsub, w, pad, m_len, has_sink, lse_out):
    refs = list(refs)
    q_ref, k_ref, v_ref, bias_ref = refs[:4]
    pos = 4
    sink_ref = lse_ref = None
    if has_sink:
        sink_ref = refs[pos]; pos += 1
    o_ref = refs[pos]; pos += 1
    if lse_out:
        lse_ref = refs[pos]

    hpc = BAND_HPC
    kw = HEAD_DIM if shared_kv else hpc * HEAD_DIM
    last = m_len // tq - 1
    qi = pl.program_id(qi_axis)
    qt = q_ref[...].T
    ones = jnp.ones((ONES_ROWS, w), BF16)
    for j in range(nsub):
        qb = qi * nsub + j
        ks = pl.multiple_of(jnp.clip(qb * tq - pad, 0, m_len - w), 64)
        case = jnp.where(qb == 0, 0, jnp.where(qb == last, 2, 1))
        for c in range(q_ref.shape[1] // (hpc * HEAD_DIM)):
            slab = qt[c * hpc * HEAD_DIM:(c + 1) * hpc * HEAD_DIM, j * tq:(j + 1) * tq]
            if shared_kv:
                qc = jnp.concatenate([slab[g * HEAD_DIM:(g + 1) * HEAD_DIM] for g in range(hpc)], axis=1)
            else:
                head = lax.broadcasted_iota(jnp.int32, slab.shape, 0) // HEAD_DIM
                qc = jnp.concatenate([jnp.where(head == g, slab, jnp.zeros_like(slab)) for g in range(hpc)],
                                     axis=1)
            k = k_ref[pl.ds(ks, w), c * kw:(c + 1) * kw]
            v = v_ref[pl.ds(ks, w), c * kw:(c + 1) * kw]
            vt = jnp.concatenate([v.T, ones], axis=0)
            st = jnp.dot(k, qc, preferred_element_type=F32) + bias_ref[case, c]
            m = jnp.max(st, axis=0, keepdims=True)
            if has_sink:
                sk = sink_ref[c]
                m = jnp.maximum(m, sk)
            pt = jnp.exp2(st - m).astype(BF16)
            acc = jnp.dot(vt, pt, preferred_element_type=F32)
            l = acc[kw:kw + 1]
            if has_sink:
                l = l + jnp.exp2(sk - m)
            lse = m + jnp.log2(l)
            for g in range(0, hpc, 2):
                ots, lses = [], []
                for gg in (g, g + 1):
                    lanes = slice(gg * tq, (gg + 1) * tq)
                    rows = slice(0, HEAD_DIM) if shared_kv else slice(gg * HEAD_DIM, (gg + 1) * HEAD_DIM)
                    ots.append(acc[rows, lanes] / l[:, lanes])
                    lses.append(jnp.broadcast_to(lse[:, lanes], (HEAD_DIM, tq)))
                out_rows = slice(j * tq, (j + 1) * tq)
                cols = slice((c * hpc + g) * HEAD_DIM, (c * hpc + g + 2) * HEAD_DIM)
                o_ref[out_rows, cols] = jnp.concatenate(ots, axis=0).T.astype(o_ref.dtype)
                if lse_out:
                    lse_ref[out_rows, cols] = jnp.concatenate(lses, axis=0).T


def _window_attention(pn, pr, bias, sink_rows, tq, nsub):
    bsz, seq, _ = pn.shape
    pad = A_WIN
    w = tq + 2 * pad
    lanes = BAND_HPC * tq
    kern = functools.partial(_band_kernel, qi_axis=1, shared_kv=True, tq=tq, nsub=nsub, w=w, pad=pad, m_len=seq,
                             has_sink=True, lse_out=False)
    return pl.pallas_call(
        kern,
        grid=(bsz, seq // (nsub * tq)),
        in_specs=[
            pl.BlockSpec((None, nsub * tq, 512), lambda b, qi: (b, qi, 0)),
            pl.BlockSpec((None, seq, 128), lambda b, qi: (b, 0, 6)),
            pl.BlockSpec((None, seq, 128), lambda b, qi: (b, 0, 7)),
            pl.BlockSpec((3, A_KV, w, lanes), lambda b, qi: (0, 0, 0, 0)),
            pl.BlockSpec((A_KV, 1, lanes), lambda b, qi: (0, 0, 0)),
        ],
        out_specs=pl.BlockSpec((None, nsub * tq, BRANCH_W), lambda b, qi: (b, qi, 0)),
        out_shape=jax.ShapeDtypeStruct((bsz, seq, BRANCH_W), BF16),
        compiler_params=pltpu.CompilerParams(
            dimension_semantics=("parallel", "arbitrary"), vmem_limit_bytes=VMEM_LIMIT),
    )(pn, pr, pr, bias, sink_rows)


def _dilated_group(qk, v, bias, tq, nsub):
    bsz, r, m_len, _ = v.shape
    pad = 64
    w = min(tq + 2 * pad, m_len)
    nsub = min(nsub, m_len // tq)
    nchain = D_HEADS_PER_GROUP // BAND_HPC
    kern = functools.partial(_band_kernel, qi_axis=2, shared_kv=False, tq=tq, nsub=nsub, w=w, pad=pad,
                             m_len=m_len, has_sink=False, lse_out=True)
    row_spec = pl.BlockSpec((None, None, nsub * tq, BRANCH_W), lambda b, c, qi: (b, c, qi, 0))
    return pl.pallas_call(
        kern,
        grid=(bsz, r, m_len // (nsub * tq)),
        in_specs=[
            row_spec,
            pl.BlockSpec((None, None, m_len, 512), lambda b, c, qi: (b, c, 0, 1)),
            pl.BlockSpec((None, None, m_len, 512), lambda b, c, qi: (b, c, 0, 0)),
            pl.BlockSpec((3, nchain, w, BAND_HPC * tq), lambda b, c, qi: (0, 0, 0, 0)),
        ],
        out_specs=[row_spec, row_spec],
        out_shape=[jax.ShapeDtypeStruct((bsz, r, m_len, BRANCH_W), BF16),
                   jax.ShapeDtypeStruct((bsz, r, m_len, BRANCH_W), F32)],
        compiler_params=pltpu.CompilerParams(
            dimension_semantics=("parallel", "parallel", "arbitrary"), vmem_limit_bytes=VMEM_LIMIT),
    )(qk, qk, v, bias)


DIFF_UNIT = 512


def _saturation_distance():
    nb = REL_BUCKETS // 2
    exact = nb // 2
    n = np.arange(exact, 4 * REL_MAX_DIST)
    large = exact + (np.log(n.astype(np.float32) / exact) / math.log(REL_MAX_DIST / exact)
                     * (nb - exact)).astype(np.int32)
    return int(n[large < nb - 1].max()) + 1 + 2


def _ceil_div(a, b):
    return -(-a // b)


def _diff_tiles(tq, tk):
    far = _saturation_distance()
    return -_ceil_div(far + tk - 1, DIFF_UNIT), _ceil_div(far + tq - 1, DIFF_UNIT)


def _diff_kernel(lv_ref, sg_ref, q_ref, k_ref, v_ref, bias_ref, o_ref, st_ref, *, tq, tk, nk, lo, hi, lam_init):
    qi = pl.program_id(2)
    vd = 2 * HEAD_DIM
    qt = q_ref[...].T
    first = lax.broadcasted_iota(jnp.int32, (vd, tq), 0) < HEAD_DIM
    zero = jnp.zeros((vd, tq), BF16)
    qbd = jnp.concatenate([jnp.where(first, qt, zero), jnp.where(first, zero, qt)], axis=1)
    m = jnp.full((1, 2 * tq), NEG, F32)
    acc = jnp.zeros((vd + ONES_ROWS, 2 * tq), F32)
    ones = jnp.ones((ONES_ROWS, tk), BF16)

    def scores(t):
        dd = jnp.clip(t * (tk // DIFF_UNIT) - qi * (tq // DIFF_UNIT), lo, hi) - lo
        b = bias_ref[dd]
        st = jnp.dot(k_ref[t * tk:(t + 1) * tk, :], qbd, preferred_element_type=F32)
        return jnp.concatenate([st[:, :tq] + b, st[:, tq:] + b], axis=1)

    st_ref[0] = scores(0)
    for t in range(nk):
        if t + 1 < nk:
            st_ref[(t + 1) % 2] = scores(t + 1)
        st = st_ref[t % 2]
        vt = jnp.concatenate([v_ref[t * tk:(t + 1) * tk, :].T, ones], axis=0)
        m_new = jnp.maximum(m, jnp.max(st, axis=0, keepdims=True))
        pt = jnp.exp2(st - m_new).astype(BF16)
        acc = jnp.exp2(m - m_new) * acc + jnp.dot(vt, pt, preferred_element_type=F32)
        m = m_new

    lv = lv_ref[...]
    lam = (jnp.exp(jnp.sum(lv[0:1] * lv[1:2], axis=-1, keepdims=True))
           - jnp.exp(jnp.sum(lv[2:3] * lv[3:4], axis=-1, keepdims=True)) + lam_init)
    ot = acc[:vd, :tq] / acc[vd:vd + 1, :tq] - lam * (acc[:vd, tq:] / acc[vd:vd + 1, tq:])
    ms = jnp.mean(ot * ot, axis=0, keepdims=True)
    ot = ot * lax.rsqrt(ms + EPS) * sg_ref[...] * (1.0 - lam_init)
    o_ref[...] = ot.T.astype(o_ref.dtype)


def _diff_bias(tbl, tq, tk):
    lo, hi = _diff_tiles(tq, tk)
    r0 = max(-lo, hi) * DIFF_UNIT + tk + tq
    vec = _lookup(tbl, _bucket(jnp.arange(-r0, r0 + 1))) * LOG2E
    assert DIFF_UNIT == tk
    nt = hi - lo + 1
    return _toeplitz(vec, nt * tk, tq, r0 + lo * DIFF_UNIT).reshape(tbl.shape[1], nt, tk, tq)


def _diff_attention(pn, pbv, bias, lam_vec, sub_gain, lam_init, tq, tk):
    bsz, seq, _ = pn.shape
    lo, hi = _diff_tiles(tq, tk)
    nt = hi - lo + 1
    vd = 2 * HEAD_DIM
    kern = functools.partial(_diff_kernel, tq=tq, tk=tk, nk=seq // tk, lo=lo, hi=hi, lam_init=lam_init)
    return pl.pallas_call(
        kern,
        grid=(B_HEADS, bsz, seq // tq),
        in_specs=[
            pl.BlockSpec((4, HEAD_DIM), lambda h, b, qi: (0, 0)),
            pl.BlockSpec((vd, 1), lambda h, b, qi: (0, 0)),
            pl.BlockSpec((None, tq, vd), lambda h, b, qi: (b, qi, 512 // vd + h)),
            pl.BlockSpec((None, seq, vd), lambda h, b, qi: (b, 0, 1024 // vd + h)),
            pl.BlockSpec((None, seq, vd), lambda h, b, qi: (b, 0, h)),
            pl.BlockSpec((None, nt, tk, tq), lambda h, b, qi: (h, 0, 0, 0), pipeline_mode=pl.Buffered(1)),
        ],
        out_specs=pl.BlockSpec((None, tq, vd), lambda h, b, qi: (b, qi, h)),
        out_shape=jax.ShapeDtypeStruct((bsz, seq, BRANCH_W), BF16),
        scratch_shapes=[pltpu.VMEM((2, tk, 2 * tq), F32)],
        compiler_params=pltpu.CompilerParams(
            dimension_semantics=("parallel", "parallel", "arbitrary"), vmem_limit_bytes=VMEM_LIMIT),
    )(lam_vec, sub_gain.reshape(vd, 1), pn, pn, pbv, bias)


def _gqa_kernel(q0_ref, q1_ref, k_ref, v_ref, o_ref, st_ref, *, tq, tk, nk):
    grp = C_HEADS // C_KV
    qts = []
    for q_ref in (q0_ref, q1_ref):
        qt = q_ref[...].T
        qts.append(jnp.concatenate([qt[g * HEAD_DIM:(g + 1) * HEAD_DIM] for g in range(grp)], axis=1))
    ones = jnp.ones((ONES_ROWS, tk), BF16)
    state = [(jnp.full((1, grp * tq), NEG, F32), jnp.zeros((HEAD_DIM + ONES_ROWS, grp * tq), F32))
             for _ in range(C_KV)]

    def scores(t, slot):
        for kv in range(C_KV):
            k = k_ref[t * tk:(t + 1) * tk, kv * HEAD_DIM:(kv + 1) * HEAD_DIM]
            st_ref[slot, kv] = jnp.dot(k, qts[kv], preferred_element_type=F32)

    scores(0, 0)
    for t in range(nk):
        if t + 1 < nk:
            scores(t + 1, (t + 1) % 2)
        for kv in range(C_KV):
            m, acc = state[kv]
            st = st_ref[t % 2, kv]
            v = v_ref[t * tk:(t + 1) * tk, kv * HEAD_DIM:(kv + 1) * HEAD_DIM]
            vt = jnp.concatenate([v.T, ones], axis=0)
            m_new = jnp.maximum(m, jnp.max(st, axis=0, keepdims=True))
            pt = jnp.exp2(st - m_new).astype(BF16)
            acc = jnp.exp2(m - m_new) * acc + jnp.dot(vt, pt, preferred_element_type=F32)
            state[kv] = (m_new, acc)

    for kv in range(C_KV):
        acc = state[kv][1]
        ot = acc[:HEAD_DIM] / acc[HEAD_DIM:HEAD_DIM + 1]
        for g in range(grp):
            h = kv * grp + g
            o_ref[:, h * HEAD_DIM:(h + 1) * HEAD_DIM] = ot[:, g * tq:(g + 1) * tq].T.astype(o_ref.dtype)


def _dense_gqa(pr, tq=512, tk=512):
    bsz, seq, _ = pr.shape
    kern = functools.partial(_gqa_kernel, tq=tq, tk=tk, nk=seq // tk)
    return pl.pallas_call(
        kern,
        grid=(bsz, seq // tq),
        in_specs=[
            pl.BlockSpec((None, tq, 256), lambda b, qi: (b, qi, 0)),
            pl.BlockSpec((None, tq, 256), lambda b, qi: (b, qi, 1)),
            pl.BlockSpec((None, seq, 128), lambda b, qi: (b, 0, 4)),
            pl.BlockSpec((None, seq, 128), lambda b, qi: (b, 0, 5)),
        ],
        out_specs=pl.BlockSpec((None, tq, BRANCH_W), lambda b, qi: (b, qi, 0)),
        out_shape=jax.ShapeDtypeStruct((bsz, seq, BRANCH_W), BF16),
        scratch_shapes=[pltpu.VMEM((2, C_KV, tk, (C_HEADS // C_KV) * tq), F32)],
        compiler_params=pltpu.CompilerParams(
            dimension_semantics=("parallel", "arbitrary"), vmem_limit_bytes=VMEM_LIMIT),
    )(pr, pr, pr, pr)


def _merge_kernel(*refs, tm):
    x_ref, oa_ref, ob_ref, oc_ref = refs[:4]
    d_refs = refs[4:10]
    gate_ref, merge_ref, wb_ref, wo_ref, out_ref, so_ref, sl_ref = refs[10:17]
    nslab = BRANCH_W // LANES

    def natural(ref, scr, r):
        if r == 1:
            return ref[0].astype(F32)
        for c in range(r):
            blk = ref[c].astype(F32)
            for s in range(nslab):
                scr[s, pl.ds(c, tm // r, stride=r), :] = blk[:, s * LANES:(s + 1) * LANES]
        return jnp.concatenate([scr[s] for s in range(nslab)], axis=1)

    os_, ls_ = [], []
    for g, (_, r) in enumerate(D_PAIRS):
        os_.append(natural(d_refs[2 * g], so_ref, r))
        ls_.append(natural(d_refs[2 * g + 1], sl_ref, r))
    mx = jnp.maximum(jnp.maximum(ls_[0], ls_[1]), ls_[2])
    es = [jnp.exp2(l - mx) for l in ls_]
    od = (es[0] * os_[0] + es[1] * os_[1] + es[2] * os_[2]) / (es[0] + es[1] + es[2])

    branches = [oa_ref[...].astype(F32), ob_ref[...].astype(F32), oc_ref[...].astype(F32), od]
    gated = [(branches[n] * gate_ref[:, n * BRANCH_W:(n + 1) * BRANCH_W].astype(F32)).astype(BF16)
             for n in range(N_BRANCH)]
    halves = []
    for half in range(2):
        cols = slice(half * 512, (half + 1) * 512)
        merged = None
        for n in range(N_BRANCH):
            y = jnp.dot(gated[n], wb_ref[n, :, cols], preferred_element_type=F32)
            term = merge_ref[:, n * D_MODEL + half * 512:n * D_MODEL + (half + 1) * 512].astype(F32) * y
            merged = term if merged is None else merged + term
        halves.append(merged.astype(BF16))
    merged = jnp.concatenate(halves, axis=1)
    out_ref[...] = x_ref[...] + jnp.dot(merged, wo_ref[...], preferred_element_type=F32)


def _merge(x, oa, ob, oc, d_outs, gate, merge, wb, wo, seq, tm=256):
    n = x.shape[0]
    ns = seq // tm
    row = pl.BlockSpec((tm, BRANCH_W), lambda i: (i, 0))
    in_specs = [pl.BlockSpec((tm, D_MODEL), lambda i: (i, 0)), row, row, row]
    for _, r in D_PAIRS:
        in_specs += [pl.BlockSpec((None, r, tm // r, BRANCH_W), lambda i: (i // ns, 0, i % ns, 0))] * 2
    in_specs += [pl.BlockSpec((tm, N_BRANCH * BRANCH_W), lambda i: (i, 0)),
                 pl.BlockSpec((tm, N_BRANCH * D_MODEL), lambda i: (i, 0)),
                 pl.BlockSpec((N_BRANCH, BRANCH_W, D_MODEL), lambda i: (0, 0, 0)),
                 pl.BlockSpec((D_MODEL, D_MODEL), lambda i: (0, 0))]
    return pl.pallas_call(
        functools.partial(_merge_kernel, tm=tm),
        grid=(n // tm,),
        in_specs=in_specs,
        out_specs=pl.BlockSpec((tm, D_MODEL), lambda i: (i, 0)),
        out_shape=jax.ShapeDtypeStruct((n, D_MODEL), F32),
        scratch_shapes=[pltpu.VMEM((BRANCH_W // LANES, tm, LANES), F32)] * 2,
        compiler_params=pltpu.CompilerParams(
            dimension_semantics=("parallel",), vmem_limit_bytes=VMEM_LIMIT),
    )(x, oa, ob, oc, *d_outs, gate, merge, wb, wo)


def kernel(x, w_in, w_branch, w_out, norm_gain, qk_gain, sink, lambda_vec, sub_norm_gain, rel_bias):
    bsz, seq, _ = x.shape
    n = bsz * seq
    depth = w_in.shape[0]
    rb = rel_bias.astype(F32)

    rows = seq // GRID_W
    row = jnp.repeat(jnp.arange(rows), GRID_W).astype(F32)
    col = jnp.tile(jnp.arange(GRID_W), rows).astype(F32)
    nf = HEAD_DIM // 4
    freqs = ROPE_THETA ** (-jnp.arange(nf, dtype=F32) / nf)
    ang = jnp.concatenate([row[:, None] * freqs, col[:, None] * freqs], axis=-1)
    cos2 = jnp.tile(jnp.cos(ang), (1, 4))
    sin2 = jnp.tile(jnp.concatenate([-jnp.sin(ang), jnp.sin(ang)], axis=-1), (1, 2))

    tq_a = 128
    bias_a = _band_bias(rb[:, :A_HEADS], 1, A_WIN, tq_a, tq_a + 2 * A_WIN, A_WIN, BAND_HPC)
    tq_b, tk_b = 1024, 512
    bias_b = _diff_bias(rb[:, A_HEADS:A_HEADS + B_HEADS], tq_b, tk_b)
    tq_d = 128
    bias_d = []
    for g, (win, r) in enumerate(D_PAIRS):
        lo = A_HEADS + B_HEADS + g * D_HEADS_PER_GROUP
        wd = min(tq_d + 128, seq // r)
        bias_d.append(_band_bias(rb[:, lo:lo + D_HEADS_PER_GROUP], r, win // (2 * r), tq_d, wd, 64, BAND_HPC))

    flag_np, rflag_np = _col_flags()
    flag = _permute_cols(jnp.asarray(flag_np))
    rflag = _permute_cols(jnp.asarray(rflag_np))
    blk = np.arange(NORM_W) // HEAD_DIM
    bd = jnp.asarray((blk[:, None] == blk[None, :]).astype(np.float32), dtype=BF16)

    xf = x.reshape(n, D_MODEL)
    for l in range(depth):
        h = _prenorm(xf, norm_gain[l].reshape(1, D_MODEL))
        w = _permute_cols(w_in[l].astype(BF16))
        gain = _permute_cols(_gain_cols(qk_gain[l]))
        norm_args = dict(gain=gain, flag=flag, bd=bd)
        pr = _proj_call(h, w, kind=K_ROPE, col0=P_ROPE, step=1, nj=2, seq=seq, rflag=rflag, cos=cos2, sin=sin2,
                        **norm_args).reshape(bsz, seq, -1)
        pn = _proj_call(h, w, kind=K_NORM, col0=P_NORM, step=1, nj=3, seq=seq, **norm_args).reshape(bsz, seq, -1)
        pbv = _proj_call(h, w, kind=K_RAW, col0=P_BV, step=1, nj=1, seq=seq).reshape(bsz, seq, -1)
        gate = _proj_call(h, w, kind=K_SILU, col0=OFF_GATE, step=1, nj=4, seq=seq)
        merge = _proj_call(h, w, kind=K_SIGMOID, col0=OFF_MERGE, step=1, nj=8, seq=seq)
        sink_rows = jnp.repeat(sink[l].astype(F32) * LOG2E, tq_a).reshape(A_KV, 1, BAND_HPC * tq_a)
        oa = _window_attention(pn, pr, bias_a, sink_rows, tq=tq_a, nsub=BAND_NSUB)
        lam_init = 0.8 - 0.6 * math.exp(-0.3 * l)
        ob = _diff_attention(pn, pbv, bias_b, lambda_vec[l], sub_norm_gain[l], lam_init, tq=tq_b, tk=tk_b)
        oc = _dense_gqa(pr)
        d_outs = []
        for g, (win, r) in enumerate(D_PAIRS):
            dqk = _proj_call(h, w, kind=K_NORM, col0=OFF_DQ + g * 512, step=(OFF_DK - OFF_DQ) // 512, nj=2,
                             seq=seq, r=r, **norm_args)
            dv = _proj_call(h, w, kind=K_RAW, col0=OFF_DV + g * 512, step=1, nj=1, seq=seq, r=r)
            if r == 1:
                dqk, dv = dqk.reshape(bsz, 1, seq, -1), dv.reshape(bsz, 1, seq, -1)
            d_outs += _dilated_group(dqk, dv, bias_d[g], tq=tq_d, nsub=BAND_NSUB)
        xf = _merge(xf, oa.reshape(n, -1), ob.reshape(n, -1), oc.reshape(n, -1), d_outs, gate, merge,
                    w_branch[l].astype(BF16), w_out[l].astype(BF16), seq)
    return xf.reshape(bsz, seq, D_MODEL)
```

```python
import functools
import math

import numpy as np
import jax
import jax.numpy as jnp
from jax import lax
from jax.experimental import pallas as pl
from jax.experimental.pallas import tpu as pltpu

F32 = jnp.float32
BF16 = jnp.bfloat16

D_MODEL = 1024
HEAD_DIM = 64
GRID_W = 64
EPS = 1e-6
A_HEADS, A_KV, A_WIN = 8, 2, 128
B_HEADS = 4
C_HEADS, C_KV = 8, 2
ROPE_THETA = 10000.0
D_PAIRS = ((128, 1), (512, 4), (2048, 16))
D_HEADS_PER_GROUP = 8
N_BRANCH = 4
BRANCH_W = 512
REL_BUCKETS = 32
REL_MAX_DIST = 1024
IN_WIDTH = 13824
NEG = -1e30
LOG2E = math.log2(math.e)
LANES = 128
ONES_ROWS = 16

OFF_AQ, OFF_AK, OFF_AV = 0, 512, 640
OFF_BQ, OFF_BK, OFF_BV = 768, 1280, 1792
OFF_CQ, OFF_CK, OFF_CV = 2304, 2816, 2944
OFF_DQ, OFF_DK, OFF_DV = 3072, 4608, 6144
OFF_GATE, OFF_MERGE = 7680, 9728
PERM = ((OFF_CQ, 768), (OFF_AK, 256), (OFF_AQ, 512), (OFF_BQ, 1024), (OFF_BV, 512))
P_ROPE, P_NORM, P_BV = 0, 1024, 2560

VMEM_LIMIT = 56 * 1024 * 1024

K_RAW, K_NORM, K_ROPE, K_SILU, K_SIGMOID = range(5)


def _permute_cols(a):
    return jnp.concatenate([a[..., lo:lo + n] for lo, n in PERM] + [a[..., OFF_DQ:]], axis=-1)


def _col_flags():
    norm = np.zeros((IN_WIDTH,), np.float32)
    for lo, hi in ((OFF_AQ, OFF_AV), (OFF_BQ, OFF_BV), (OFF_CQ, OFF_CV), (OFF_DQ, OFF_DV)):
        norm[lo:hi] = 1.0
    rope = np.zeros((IN_WIDTH,), np.float32)
    rope[OFF_CQ:OFF_CV] = 1.0
    return norm.reshape(1, IN_WIDTH), rope.reshape(1, IN_WIDTH)


def _gain_cols(g):
    sc = HEAD_DIM ** -0.5 * LOG2E
    one = lambda n: jnp.ones((n,), F32)
    parts = [jnp.tile(g[0, 0], A_HEADS) * sc, jnp.tile(g[0, 1], A_KV), one(128),
             jnp.tile(g[1, 0], 2 * B_HEADS) * sc, jnp.tile(g[1, 1], 2 * B_HEADS), one(512),
             jnp.tile(g[2, 0], C_HEADS) * sc, jnp.tile(g[2, 1], C_KV), one(128),
             jnp.tile(g[3, 0], 24) * sc, jnp.tile(g[3, 1], 24), one(1536),
             one(IN_WIDTH - OFF_GATE)]
    return jnp.concatenate(parts).reshape(1, IN_WIDTH)


def _prenorm_kernel(x_ref, g_ref, o_ref):
    x = x_ref[...]
    ms = jnp.mean(x * x, axis=-1, keepdims=True)
    o_ref[...] = (x * lax.rsqrt(ms + EPS) * g_ref[...]).astype(o_ref.dtype)


def _prenorm(x, g, tm=1024):
    n = x.shape[0]
    return pl.pallas_call(
        _prenorm_kernel,
        grid=(n // tm,),
        in_specs=[pl.BlockSpec((tm, D_MODEL), lambda i: (i, 0)),
                  pl.BlockSpec((1, D_MODEL), lambda i: (0, 0))],
        out_specs=pl.BlockSpec((tm, D_MODEL), lambda i: (i, 0)),
        out_shape=jax.ShapeDtypeStruct((n, D_MODEL), BF16),
        compiler_params=pltpu.CompilerParams(dimension_semantics=("parallel",), vmem_limit_bytes=VMEM_LIMIT),
    )(x, g)


PROJ_TM = 2048
PROJ_TN = 512
PROJ_CHUNK = 256
NORM_W = 256
DEINT_STRIDE = 4


def _proj_kernel(*refs, kind, r, tm, tn):
    refs = list(refs)
    h_ref, w_ref = refs[:2]
    pos = 2
    if kind in (K_NORM, K_ROPE):
        gain_ref, flag_ref, bd_ref = refs[pos:pos + 3]; pos += 3
    if kind == K_ROPE:
        rflag_ref, cos_ref, sin_ref = refs[pos:pos + 3]; pos += 3
    o_ref = refs[pos]; pos += 1
    slab_ref = refs[pos] if r > 1 else None
    slab2_ref = refs[pos + 1] if r > DEINT_STRIDE else None

    nchunk = tm // PROJ_CHUNK

    def main(rc):
        return jnp.dot(h_ref[rc * PROJ_CHUNK:(rc + 1) * PROJ_CHUNK, :], w_ref[...], preferred_element_type=F32)

    nxt = main(0)
    for rc in range(nchunk):
        rows = slice(rc * PROJ_CHUNK, (rc + 1) * PROJ_CHUNK)
        acc = nxt
        if rc + 1 < nchunk:
            nxt = main(rc + 1)
        if kind in (K_NORM, K_ROPE):
            sq = (acc * acc).astype(BF16)
            bd = bd_ref[...]
            ms = jnp.concatenate([jnp.dot(sq[:, c:c + NORM_W], bd, preferred_element_type=F32)
                                  for c in range(0, tn, NORM_W)], axis=1) * (1.0 / HEAD_DIM)
            y = acc * jnp.where(flag_ref[...] > 0.0, lax.rsqrt(ms + EPS), 1.0) * gain_ref[...]
            if kind == K_ROPE:
                c = jnp.concatenate([cos_ref[rows, :]] * (tn // LANES), axis=1)
                s = jnp.concatenate([sin_ref[rows, :]] * (tn // LANES), axis=1)
                lane = lax.broadcasted_iota(jnp.int32, y.shape, 1) & (HEAD_DIM - 1)
                half = HEAD_DIM // 2
                partner = jnp.where(lane < half, pltpu.roll(y, tn - half, axis=1), pltpu.roll(y, half, axis=1))
                y = jnp.where(rflag_ref[...] > 0.0, y * c + partner * s, y)
        elif kind == K_SILU:
            y = acc * (0.5 * jnp.tanh(0.5 * acc) + 0.5)
        elif kind == K_SIGMOID:
            y = 0.5 * jnp.tanh(0.5 * acc) + 0.5
        else:
            y = acc
        if r == 1:
            o_ref[rows, :] = y.astype(o_ref.dtype)
        else:
            for s_ in range(tn // LANES):
                slab_ref[s_, rows, :] = y[:, s_ * LANES:(s_ + 1) * LANES]
    if r > 1:
        r1 = min(r, DEINT_STRIDE)
        r2 = r // r1
        for s_ in range(tn // LANES):
            if r2 > 1:
                for c1 in range(r1):
                    slab2_ref[s_, c1 * (tm // r1):(c1 + 1) * (tm // r1), :] = (
                        slab_ref[s_, pl.ds(c1, tm // r1, stride=r1), :])
            for c1 in range(r1):
                for c2 in range(r2):
                    if r2 > 1:
                        rows_c = slab2_ref[s_, pl.ds(c1 * (tm // r1) + c2, tm // r, stride=r2), :]
                    else:
                        rows_c = slab_ref[s_, pl.ds(c1, tm // r, stride=r), :]
                    o_ref[c1 + r1 * c2, :, s_ * LANES:(s_ + 1) * LANES] = rows_c.astype(o_ref.dtype)


def _proj_call(h, w, *, kind, col0, step, nj, seq, r=1, gain=None, flag=None, rflag=None, bd=None, cos=None,
               sin=None):
    tm, tn = PROJ_TM, PROJ_TN
    n = h.shape[0]
    ns = seq // tm
    cb = col0 // tn
    col_spec = pl.BlockSpec((1, tn), lambda i, j: (0, cb + j * step))
    in_specs = [pl.BlockSpec((tm, D_MODEL), lambda i, j: (i, 0)),
                pl.BlockSpec((D_MODEL, tn), lambda i, j: (0, cb + j * step))]
    args = [h, w]
    if kind in (K_NORM, K_ROPE):
        in_specs += [col_spec, col_spec, pl.BlockSpec((NORM_W, NORM_W), lambda i, j: (0, 0))]
        args += [gain, flag, bd]
    if kind == K_ROPE:
        in_specs += [col_spec] + [pl.BlockSpec((tm, LANES), lambda i, j: (i % ns, 0))] * 2
        args += [rflag, cos, sin]
    scratch = []
    if r == 1:
        out_shape = jax.ShapeDtypeStruct((n, nj * tn), BF16)
        out_spec = pl.BlockSpec((tm, tn), lambda i, j: (i, j))
    else:
        out_shape = jax.ShapeDtypeStruct((n // seq, r, seq // r, nj * tn), BF16)
        out_spec = pl.BlockSpec((None, r, tm // r, tn), lambda i, j: (i // ns, 0, i % ns, j))
        scratch = [pltpu.VMEM((tn // LANES, tm, LANES), F32)] * (2 if r > DEINT_STRIDE else 1)
    return pl.pallas_call(
        functools.partial(_proj_kernel, kind=kind, r=r, tm=tm, tn=tn),
        grid=(n // tm, nj),
        in_specs=in_specs,
        out_specs=out_spec,
        out_shape=out_shape,
        scratch_shapes=scratch,
        compiler_params=pltpu.CompilerParams(
            dimension_semantics=("parallel", "parallel"), vmem_limit_bytes=VMEM_LIMIT),
    )(*args)


def _bucket(rel):
    nb = REL_BUCKETS // 2
    exact = nb // 2
    n = jnp.abs(rel)
    large = exact + (jnp.log(jnp.maximum(n, exact).astype(F32) / exact)
                     / math.log(REL_MAX_DIST / exact) * (nb - exact)).astype(jnp.int32)
    large = jnp.minimum(large, nb - 1)
    return jnp.where(rel > 0, nb, 0) + jnp.where(n < exact, n, large)


def _lookup(tbl, bucket):
    shape = (tbl.shape[1],) + (1,) * bucket.ndim
    out = jnp.zeros((tbl.shape[1],) + bucket.shape, F32)
    for j in range(REL_BUCKETS):
        out = jnp.where(bucket[None] == j, tbl[j].reshape(shape), out)
    return out


def _toeplitz_kernel(u_ref, o_ref, *, n_rows, n_cols):
    x = jnp.broadcast_to(u_ref[...], (n_rows, u_ref.shape[-1]))
    o_ref[...] = pltpu.roll(x, 0, 1, stride=1, stride_axis=0)[:, n_rows:n_rows + n_cols]


def _toeplitz_tiles(vec, starts, n_rows, n_cols, out_shape, out_index):
    assert n_rows % LANES == 0 and n_cols % LANES == 0
    nh = vec.shape[0]
    span = n_rows + n_cols - 1
    lp = _ceil_div(1 + span, LANES) * LANES
    rows = [jnp.pad(vec[:, s - (n_cols - 1):s + n_rows][:, ::-1], ((0, 0), (1, lp - 1 - span))) for s in starts]
    u = jnp.stack(rows, axis=1)[:, :, None, :]
    return pl.pallas_call(
        functools.partial(_toeplitz_kernel, n_rows=n_rows, n_cols=n_cols),
        grid=(nh, len(starts)),
        in_specs=[pl.BlockSpec((None, None, 1, lp), lambda h, t: (h, t, 0, 0))],
        out_specs=pl.BlockSpec((None, None, n_rows, n_cols), out_index),
        out_shape=jax.ShapeDtypeStruct(out_shape, F32),
        compiler_params=pltpu.CompilerParams(dimension_semantics=("parallel", "parallel")),
    )(u)


def _band_bias(tbl, stride, half, tq, w, pad, hpc):
    nh = tbl.shape[1]
    r0 = w + tq
    rel = jnp.arange(-r0, r0 + 1)
    vec = jnp.where((jnp.abs(rel) <= half)[None], _lookup(tbl, _bucket(rel * stride)) * LOG2E, NEG)
    starts = [r0 - off for off in (0, pad, w - tq)]
    return _toeplitz_tiles(vec, starts, w, tq, (3, nh // hpc, w, hpc * tq),
                           lambda h, t: (t, h // hpc, 0, h % hpc))


BAND_HPC = 4
BAND_NSUB = 4


def _band_kernel(*refs, qi_axis, shared_kv, tq, nsub, w, pad, m_len, has_sink, lse_out):
    refs = list(refs)
    q_ref, k_ref, v_ref, bias_ref = refs[:4]
    pos = 4
    sink_ref = lse_ref = None
    if has_sink:
        sink_ref = refs[pos]; pos += 1
    o_ref = refs[pos]; pos += 1
    if lse_out:
        lse_ref = refs[pos]

    hpc = BAND_HPC
    kw = HEAD_DIM if shared_kv else hpc * HEAD_DIM
    last = m_len // tq - 1
    qi = pl.program_id(qi_axis)
    qt = q_ref[...].T
    ones = jnp.ones((ONES_ROWS, w), BF16)
    for j in range(nsub):
        qb = qi * nsub + j
        ks = pl.multiple_of(jnp.clip(qb * tq - pad, 0, m_len - w), 64)
        case = jnp.where(qb == 0, 0, jnp.where(qb == last, 2, 1))
        for c in range(q_ref.shape[1] // (hpc * HEAD_DIM)):
            slab = qt[c * hpc * HEAD_DIM:(c + 1) * hpc * HEAD_DIM, j * tq:(j + 1) * tq]
            if shared_kv:
                qc = jnp.concatenate([slab[g * HEAD_DIM:(g + 1) * HEAD_DIM] for g in range(hpc)], axis=1)
            else:
                head = lax.broadcasted_iota(jnp.int32, slab.shape, 0) // HEAD_DIM
                qc = jnp.concatenate([jnp.where(head == g, slab, jnp.zeros_like(slab)) for g in range(hpc)],
                                     axis=1)
            k = k_ref[pl.ds(ks, w), c * kw:(c + 1) * kw]
            v = v_ref[pl.ds(ks, w), c * kw:(c + 1) * kw]
            vt = jnp.concatenate([v.T, ones], axis=0)
            st = jnp.dot(k, qc, preferred_element_type=F32) + bias_ref[case, c]
            m = jnp.max(st, axis=0, keepdims=True)
            if has_sink:
                sk = sink_ref[c]
                m = jnp.maximum(m, sk)
            pt = jnp.exp2(st - m).astype(BF16)
            acc = jnp.dot(vt, pt, preferred_element_type=F32)
            l = acc[kw:kw + 1]
            if has_sink:
                l = l + jnp.exp2(sk - m)
            lse = m + jnp.log2(l)
            for g in range(0, hpc, 2):
                ots, lses = [], []
                for gg in (g, g + 1):
                    lanes = slice(gg * tq, (gg + 1) * tq)
                    rows = slice(0, HEAD_DIM) if shared_kv else slice(gg * HEAD_DIM, (gg + 1) * HEAD_DIM)
                    ots.append(acc[rows, lanes] / l[:, lanes])
                    lses.append(jnp.broadcast_to(lse[:, lanes], (HEAD_DIM, tq)))
                out_rows = slice(j * tq, (j + 1) * tq)
                cols = slice((c * hpc + g) * HEAD_DIM, (c * hpc + g + 2) * HEAD_DIM)
                o_ref[out_rows, cols] = jnp.concatenate(ots, axis=0).T.astype(o_ref.dtype)
                if lse_out:
                    lse_ref[out_rows, cols] = jnp.concatenate(lses, axis=0).T


def _window_attention(pn, pr, bias, sink_rows, tq, nsub):
    bsz, seq, _ = pn.shape
    pad = A_WIN
    w = tq + 2 * pad
    lanes = BAND_HPC * tq
    kern = functools.partial(_band_kernel, qi_axis=1, shared_kv=True, tq=tq, nsub=nsub, w=w, pad=pad, m_len=seq,
                             has_sink=True, lse_out=False)
    return pl.pallas_call(
        kern,
        grid=(bsz, seq // (nsub * tq)),
        in_specs=[
            pl.BlockSpec((None, nsub * tq, 512), lambda b, qi: (b, qi, 0)),
            pl.BlockSpec((None, seq, 128), lambda b, qi: (b, 0, 6)),
            pl.BlockSpec((None, seq, 128), lambda b, qi: (b, 0, 7)),
            pl.BlockSpec((3, A_KV, w, lanes), lambda b, qi: (0, 0, 0, 0)),
            pl.BlockSpec((A_KV, 1, lanes), lambda b, qi: (0, 0, 0)),
        ],
        out_specs=pl.BlockSpec((None, nsub * tq, BRANCH_W), lambda b, qi: (b, qi, 0)),
        out_shape=jax.ShapeDtypeStruct((bsz, seq, BRANCH_W), BF16),
        compiler_params=pltpu.CompilerParams(
            dimension_semantics=("parallel", "arbitrary"), vmem_limit_bytes=VMEM_LIMIT),
    )(pn, pr, pr, bias, sink_rows)


def _dilated_group(qk, v, bias, tq, nsub):
    bsz, r, m_len, _ = v.shape
    pad = 64
    w = min(tq + 2 * pad, m_len)
    nsub = min(nsub, m_len // tq)
    nchain = D_HEADS_PER_GROUP // BAND_HPC
    kern = functools.partial(_band_kernel, qi_axis=2, shared_kv=False, tq=tq, nsub=nsub, w=w, pad=pad,
                             m_len=m_len, has_sink=False, lse_out=True)
    row_spec = pl.BlockSpec((None, None, nsub * tq, BRANCH_W), lambda b, c, qi: (b, c, qi, 0))
    return pl.pallas_call(
        kern,
        grid=(bsz, r, m_len // (nsub * tq)),
        in_specs=[
            row_spec,
            pl.BlockSpec((None, None, m_len, 512), lambda b, c, qi: (b, c, 0, 1)),
            pl.BlockSpec((None, None, m_len, 512), lambda b, c, qi: (b, c, 0, 0)),
            pl.BlockSpec((3, nchain, w, BAND_HPC * tq), lambda b, c, qi: (0, 0, 0, 0)),
        ],
        out_specs=[row_spec, row_spec],
        out_shape=[jax.ShapeDtypeStruct((bsz, r, m_len, BRANCH_W), BF16),
                   jax.ShapeDtypeStruct((bsz, r, m_len, BRANCH_W), F32)],
        compiler_params=pltpu.CompilerParams(
            dimension_semantics=("parallel", "parallel", "arbitrary"), vmem_limit_bytes=VMEM_LIMIT),
    )(qk, qk, v, bias)


DIFF_UNIT = 512


def _saturation_distance():
    nb = REL_BUCKETS // 2
    exact = nb // 2
    n = np.arange(exact, 4 * REL_MAX_DIST)
    large = exact + (np.log(n.astype(np.float32) / exact) / math.log(REL_MAX_DIST / exact)
                     * (nb - exact)).astype(np.int32)
    return int(n[large < nb - 1].max()) + 1 + 2


def _ceil_div(a, b):
    return -(-a // b)


def _diff_tiles(tq, tk):
    far = _saturation_distance()
    return -_ceil_div(far + tk - 1, DIFF_UNIT), _ceil_div(far + tq - 1, DIFF_UNIT)


def _diff_kernel(lv_ref, sg_ref, q_ref, k_ref, v_ref, bias_ref, o_ref, st_ref, *, tq, tk, nk, lo, hi, lam_init):
    qi = pl.program_id(2)
    vd = 2 * HEAD_DIM
    qt = q_ref[...].T
    first = lax.broadcasted_iota(jnp.int32, (vd, tq), 0) < HEAD_DIM
    zero = jnp.zeros((vd, tq), BF16)
    qbd = jnp.concatenate([jnp.where(first, qt, zero), jnp.where(first, zero, qt)], axis=1)
    m = jnp.full((1, 2 * tq), NEG, F32)
    acc = jnp.zeros((vd + ONES_ROWS, 2 * tq), F32)
    ones = jnp.ones((ONES_ROWS, tk), BF16)

    def scores(t):
        dd = jnp.clip(t * (tk // DIFF_UNIT) - qi * (tq // DIFF_UNIT), lo, hi) - lo
        b = bias_ref[dd]
        st = jnp.dot(k_ref[t * tk:(t + 1) * tk, :], qbd, preferred_element_type=F32)
        return jnp.concatenate([st[:, :tq] + b, st[:, tq:] + b], axis=1)

    st_ref[0] = scores(0)
    for t in range(nk):
        if t + 1 < nk:
            st_ref[(t + 1) % 2] = scores(t + 1)
        st = st_ref[t % 2]
        vt = jnp.concatenate([v_ref[t * tk:(t + 1) * tk, :].T, ones], axis=0)
        m_new = jnp.maximum(m, jnp.max(st, axis=0, keepdims=True))
        pt = jnp.exp2(st - m_new).astype(BF16)
        acc = jnp.exp2(m - m_new) * acc + jnp.dot(vt, pt, preferred_element_type=F32)
        m = m_new

    lv = lv_ref[...]
    lam = (jnp.exp(jnp.sum(lv[0:1] * lv[1:2], axis=-1, keepdims=True))
           - jnp.exp(jnp.sum(lv[2:3] * lv[3:4], axis=-1, keepdims=True)) + lam_init)
    ot = acc[:vd, :tq] / acc[vd:vd + 1, :tq] - lam * (acc[:vd, tq:] / acc[vd:vd + 1, tq:])
    ms = jnp.mean(ot * ot, axis=0, keepdims=True)
    ot = ot * lax.rsqrt(ms + EPS) * sg_ref[...] * (1.0 - lam_init)
    o_ref[...] = ot.T.astype(o_ref.dtype)


def _diff_bias(tbl, tq, tk):
    lo, hi = _diff_tiles(tq, tk)
    r0 = max(-lo, hi) * DIFF_UNIT + tk + tq
    vec = _lookup(tbl, _bucket(jnp.arange(-r0, r0 + 1))) * LOG2E
    starts = [r0 + d * DIFF_UNIT for d in range(lo, hi + 1)]
    return _toeplitz_tiles(vec, starts, tk, tq, (tbl.shape[1], len(starts), tk, tq), lambda h, t: (h, t, 0, 0))


def _diff_attention(pn, pbv, bias, lam_vec, sub_gain, lam_init, tq, tk):
    bsz, seq, _ = pn.shape
    lo, hi = _diff_tiles(tq, tk)
    nt = hi - lo + 1
    vd = 2 * HEAD_DIM
    kern = functools.partial(_diff_kernel, tq=tq, tk=tk, nk=seq // tk, lo=lo, hi=hi, lam_init=lam_init)
    return pl.pallas_call(
        kern,
        grid=(B_HEADS, bsz, seq // tq),
        in_specs=[
            pl.BlockSpec((4, HEAD_DIM), lambda h, b, qi: (0, 0)),
            pl.BlockSpec((vd, 1), lambda h, b, qi: (0, 0)),
            pl.BlockSpec((None, tq, vd), lambda h, b, qi: (b, qi, 512 // vd + h)),
            pl.BlockSpec((None, seq, vd), lambda h, b, qi: (b, 0, 1024 // vd + h)),
            pl.BlockSpec((None, seq, vd), lambda h, b, qi: (b, 0, h)),
            pl.BlockSpec((None, nt, tk, tq), lambda h, b, qi: (h, 0, 0, 0), pipeline_mode=pl.Buffered(1)),
        ],
        out_specs=pl.BlockSpec((None, tq, vd), lambda h, b, qi: (b, qi, h)),
        out_shape=jax.ShapeDtypeStruct((bsz, seq, BRANCH_W), BF16),
        scratch_shapes=[pltpu.VMEM((2, tk, 2 * tq), F32)],
        compiler_params=pltpu.CompilerParams(
            dimension_semantics=("parallel", "parallel", "arbitrary"), vmem_limit_bytes=VMEM_LIMIT),
    )(lam_vec, sub_gain.reshape(vd, 1), pn, pn, pbv, bias)


def _gqa_kernel(q0_ref, q1_ref, k_ref, v_ref, o_ref, st_ref, *, tq, tk, nk):
    grp = C_HEADS // C_KV
    qts = []
    for q_ref in (q0_ref, q1_ref):
        qt = q_ref[...].T
        qts.append(jnp.concatenate([qt[g * HEAD_DIM:(g + 1) * HEAD_DIM] for g in range(grp)], axis=1))
    ones = jnp.ones((ONES_ROWS, tk), BF16)
    state = [(jnp.full((1, grp * tq), NEG, F32), jnp.zeros((HEAD_DIM + ONES_ROWS, grp * tq), F32))
             for _ in range(C_KV)]

    def scores(t, slot):
        for kv in range(C_KV):
            k = k_ref[t * tk:(t + 1) * tk, kv * HEAD_DIM:(kv + 1) * HEAD_DIM]
            st_ref[slot, kv] = jnp.dot(k, qts[kv], preferred_element_type=F32)

    scores(0, 0)
    for t in range(nk):
        if t + 1 < nk:
            scores(t + 1, (t + 1) % 2)
        for kv in range(C_KV):
            m, acc = state[kv]
            st = st_ref[t % 2, kv]
            v = v_ref[t * tk:(t + 1) * tk, kv * HEAD_DIM:(kv + 1) * HEAD_DIM]
            vt = jnp.concatenate([v.T, ones], axis=0)
            m_new = jnp.maximum(m, jnp.max(st, axis=0, keepdims=True))
            pt = jnp.exp2(st - m_new).astype(BF16)
            acc = jnp.exp2(m - m_new) * acc + jnp.dot(vt, pt, preferred_element_type=F32)
            state[kv] = (m_new, acc)

    for kv in range(C_KV):
        acc = state[kv][1]
        ot = acc[:HEAD_DIM] / acc[HEAD_DIM:HEAD_DIM + 1]
        for g in range(grp):
            h = kv * grp + g
            o_ref[:, h * HEAD_DIM:(h + 1) * HEAD_DIM] = ot[:, g * tq:(g + 1) * tq].T.astype(o_ref.dtype)


def _dense_gqa(pr, tq=512, tk=512):
    bsz, seq, _ = pr.shape
    kern = functools.partial(_gqa_kernel, tq=tq, tk=tk, nk=seq // tk)
    return pl.pallas_call(
        kern,
        grid=(bsz, seq // tq),
        in_specs=[
            pl.BlockSpec((None, tq, 256), lambda b, qi: (b, qi, 0)),
            pl.BlockSpec((None, tq, 256), lambda b, qi: (b, qi, 1)),
            pl.BlockSpec((None, seq, 128), lambda b, qi: (b, 0, 4)),
            pl.BlockSpec((None, seq, 128), lambda b, qi: (b, 0, 5)),
        ],
        out_specs=pl.BlockSpec((None, tq, BRANCH_W), lambda b, qi: (b, qi, 0)),
        out_shape=jax.ShapeDtypeStruct((bsz, seq, BRANCH_W), BF16),
        scratch_shapes=[pltpu.VMEM((2, C_KV, tk, (C_HEADS // C_KV) * tq), F32)],
        compiler_params=pltpu.CompilerParams(
            dimension_semantics=("parallel", "arbitrary"), vmem_limit_bytes=VMEM_LIMIT),
    )(pr, pr, pr, pr)


def _merge_kernel(*refs, tm):
    x_ref, oa_ref, ob_ref, oc_ref = refs[:4]
    d_refs = refs[4:10]
    gate_ref, merge_ref, wb_ref, wo_ref, out_ref, so_ref, sl_ref = refs[10:17]
    nslab = BRANCH_W // LANES

    def natural(ref, scr, r):
        if r == 1:
            return ref[0].astype(F32)
        for c in range(r):
            blk = ref[c].astype(F32)
            for s in range(nslab):
                scr[s, pl.ds(c, tm // r, stride=r), :] = blk[:, s * LANES:(s + 1) * LANES]
        return jnp.concatenate([scr[s] for s in range(nslab)], axis=1)

    os_, ls_ = [], []
    for g, (_, r) in enumerate(D_PAIRS):
        os_.append(natural(d_refs[2 * g], so_ref, r))
        ls_.append(natural(d_refs[2 * g + 1], sl_ref, r))
    mx = jnp.maximum(jnp.maximum(ls_[0], ls_[1]), ls_[2])
    es = [jnp.exp2(l - mx) for l in ls_]
    od = (es[0] * os_[0] + es[1] * os_[1] + es[2] * os_[2]) / (es[0] + es[1] + es[2])

    branches = [oa_ref[...].astype(F32), ob_ref[...].astype(F32), oc_ref[...].astype(F32), od]
    gated = [(branches[n] * gate_ref[:, n * BRANCH_W:(n + 1) * BRANCH_W].astype(F32)).astype(BF16)
             for n in range(N_BRANCH)]
    halves = []
    for half in range(2):
        cols = slice(half * 512, (half + 1) * 512)
        merged = None
        for n in range(N_BRANCH):
            y = jnp.dot(gated[n], wb_ref[n, :, cols], preferred_element_type=F32)
            term = merge_ref[:, n * D_MODEL + half * 512:n * D_MODEL + (half + 1) * 512].astype(F32) * y
            merged = term if merged is None else merged + term
        halves.append(merged.astype(BF16))
    merged = jnp.concatenate(halves, axis=1)
    out_ref[...] = x_ref[...] + jnp.dot(merged, wo_ref[...], preferred_element_type=F32)


def _merge(x, oa, ob, oc, d_outs, gate, merge, wb, wo, seq, tm=256):
    n = x.shape[0]
    ns = seq // tm
    row = pl.BlockSpec((tm, BRANCH_W), lambda i: (i, 0))
    in_specs = [pl.BlockSpec((tm, D_MODEL), lambda i: (i, 0)), row, row, row]
    for _, r in D_PAIRS:
        in_specs += [pl.BlockSpec((None, r, tm // r, BRANCH_W), lambda i: (i // ns, 0, i % ns, 0))] * 2
    in_specs += [pl.BlockSpec((tm, N_BRANCH * BRANCH_W), lambda i: (i, 0)),
                 pl.BlockSpec((tm, N_BRANCH * D_MODEL), lambda i: (i, 0)),
                 pl.BlockSpec((N_BRANCH, BRANCH_W, D_MODEL), lambda i: (0, 0, 0)),
                 pl.BlockSpec((D_MODEL, D_MODEL), lambda i: (0, 0))]
    return pl.pallas_call(
        functools.partial(_merge_kernel, tm=tm),
        grid=(n // tm,),
        in_specs=in_specs,
        out_specs=pl.BlockSpec((tm, D_MODEL), lambda i: (i, 0)),
        out_shape=jax.ShapeDtypeStruct((n, D_MODEL), F32),
        scratch_shapes=[pltpu.VMEM((BRANCH_W // LANES, tm, LANES), F32)] * 2,
        compiler_params=pltpu.CompilerParams(
            dimension_semantics=("parallel",), vmem_limit_bytes=VMEM_LIMIT),
    )(x, oa, ob, oc, *d_outs, gate, merge, wb, wo)


def kernel(x, w_in, w_branch, w_out, norm_gain, qk_gain, sink, lambda_vec, sub_norm_gain, rel_bias):
    bsz, seq, _ = x.shape
    n = bsz * seq
    depth = w_in.shape[0]
    rb = rel_bias.astype(F32)

    rows = seq // GRID_W
    row = jnp.repeat(jnp.arange(rows), GRID_W).astype(F32)
    col = jnp.tile(jnp.arange(GRID_W), rows).astype(F32)
    nf = HEAD_DIM // 4
    freqs = ROPE_THETA ** (-jnp.arange(nf, dtype=F32) / nf)
    ang = jnp.concatenate([row[:, None] * freqs, col[:, None] * freqs], axis=-1)
    cos2 = jnp.tile(jnp.cos(ang), (1, 4))
    sin2 = jnp.tile(jnp.concatenate([-jnp.sin(ang), jnp.sin(ang)], axis=-1), (1, 2))

    tq_a = 128
    bias_a = _band_bias(rb[:, :A_HEADS], 1, A_WIN, tq_a, tq_a + 2 * A_WIN, A_WIN, BAND_HPC)
    tq_b, tk_b = 1024, 512
    bias_b = _diff_bias(rb[:, A_HEADS:A_HEADS + B_HEADS], tq_b, tk_b)
    tq_d = 128
    bias_d = []
    for g, (win, r) in enumerate(D_PAIRS):
        lo = A_HEADS + B_HEADS + g * D_HEADS_PER_GROUP
        wd = min(tq_d + 128, seq // r)
        bias_d.append(_band_bias(rb[:, lo:lo + D_HEADS_PER_GROUP], r, win // (2 * r), tq_d, wd, 64, BAND_HPC))

    flag_np, rflag_np = _col_flags()
    flag = _permute_cols(jnp.asarray(flag_np))
    rflag = _permute_cols(jnp.asarray(rflag_np))
    blk = np.arange(NORM_W) // HEAD_DIM
    bd = jnp.asarray((blk[:, None] == blk[None, :]).astype(np.float32), dtype=BF16)

    xf = x.reshape(n, D_MODEL)
    for l in range(depth):
        h = _prenorm(xf, norm_gain[l].reshape(1, D_MODEL))
        w = _permute_cols(w_in[l].astype(BF16))
        gain = _permute_cols(_gain_cols(qk_gain[l]))
        norm_args = dict(gain=gain, flag=flag, bd=bd)
        pr = _proj_call(h, w, kind=K_ROPE, col0=P_ROPE, step=1, nj=2, seq=seq, rflag=rflag, cos=cos2, sin=sin2,
                        **norm_args).reshape(bsz, seq, -1)
        pn = _proj_call(h, w, kind=K_NORM, col0=P_NORM, step=1, nj=3, seq=seq, **norm_args).reshape(bsz, seq, -1)
        pbv = _proj_call(h, w, kind=K_RAW, col0=P_BV, step=1, nj=1, seq=seq).reshape(bsz, seq, -1)
        gate = _proj_call(h, w, kind=K_SILU, col0=OFF_GATE, step=1, nj=4, seq=seq)
        merge = _proj_call(h, w, kind=K_SIGMOID, col0=OFF_MERGE, step=1, nj=8, seq=seq)
        sink_rows = jnp.repeat(sink[l].astype(F32) * LOG2E, tq_a).reshape(A_KV, 1, BAND_HPC * tq_a)
        oa = _window_attention(pn, pr, bias_a, sink_rows, tq=tq_a, nsub=BAND_NSUB)
        lam_init = 0.8 - 0.6 * math.exp(-0.3 * l)
        ob = _diff_attention(pn, pbv, bias_b, lambda_vec[l], sub_norm_gain[l], lam_init, tq=tq_b, tk=tk_b)
        oc = _dense_gqa(pr)
        d_outs = []
        for g, (win, r) in enumerate(D_PAIRS):
            dqk = _proj_call(h, w, kind=K_NORM, col0=OFF_DQ + g * 512, step=(OFF_DK - OFF_DQ) // 512, nj=2,
                             seq=seq, r=r, **norm_args)
            dv = _proj_call(h, w, kind=K_RAW, col0=OFF_DV + g * 512, step=1, nj=1, seq=seq, r=r)
            if r == 1:
                dqk, dv = dqk.reshape(bsz, 1, seq, -1), dv.reshape(bsz, 1, seq, -1)
            d_outs += _dilated_group(dqk, dv, bias_d[g], tq=tq_d, nsub=BAND_NSUB)
        xf = _merge(xf, oa.reshape(n, -1), ob.reshape(n, -1), oc.reshape(n, -1), d_outs, gate, merge,
                    w_branch[l].astype(BF16), w_out[l].astype(BF16), seq)
    return xf.reshape(bsz, seq, D_MODEL)
```

```python
import functools
import math

import numpy as np
import jax
import jax.numpy as jnp
from jax import lax
from jax.experimental import pallas as pl
from jax.experimental.pallas import tpu as pltpu

F32 = jnp.float32
BF16 = jnp.bfloat16

D_MODEL = 1024
HEAD_DIM = 64
GRID_W = 64
EPS = 1e-6
A_HEADS, A_KV, A_WIN = 8, 2, 128
B_HEADS = 4
C_HEADS, C_KV = 8, 2
ROPE_THETA = 10000.0
D_PAIRS = ((128, 1), (512, 4), (2048, 16))
D_HEADS_PER_GROUP = 8
N_BRANCH = 4
BRANCH_W = 512
REL_BUCKETS = 32
REL_MAX_DIST = 1024
IN_WIDTH = 13824
NEG = -1e30
LOG2E = math.log2(math.e)
LANES = 128
ONES_ROWS = 16

OFF_AQ, OFF_AK, OFF_AV = 0, 512, 640
OFF_BQ, OFF_BK, OFF_BV = 768, 1280, 1792
OFF_CQ, OFF_CK, OFF_CV = 2304, 2816, 2944
OFF_DQ, OFF_DK, OFF_DV = 3072, 4608, 6144
OFF_GATE, OFF_MERGE = 7680, 9728
PERM = ((OFF_CQ, 768), (OFF_AK, 256), (OFF_AQ, 512), (OFF_BQ, 1024), (OFF_BV, 512))
P_ROPE, P_NORM, P_BV = 0, 1024, 2560

VMEM_LIMIT = 56 * 1024 * 1024

K_RAW, K_NORM, K_ROPE, K_SILU, K_SIGMOID = range(5)


def _permute_cols(a):
    return jnp.concatenate([a[..., lo:lo + n] for lo, n in PERM] + [a[..., OFF_DQ:]], axis=-1)


def _col_flags():
    norm = np.zeros((IN_WIDTH,), np.float32)
    for lo, hi in ((OFF_AQ, OFF_AV), (OFF_BQ, OFF_BV), (OFF_CQ, OFF_CV), (OFF_DQ, OFF_DV)):
        norm[lo:hi] = 1.0
    rope = np.zeros((IN_WIDTH,), np.float32)
    rope[OFF_CQ:OFF_CV] = 1.0
    return norm.reshape(1, IN_WIDTH), rope.reshape(1, IN_WIDTH)


def _gain_cols(g):
    sc = HEAD_DIM ** -0.5 * LOG2E
    one = lambda n: jnp.ones((n,), F32)
    parts = [jnp.tile(g[0, 0], A_HEADS) * sc, jnp.tile(g[0, 1], A_KV), one(128),
             jnp.tile(g[1, 0], 2 * B_HEADS) * sc, jnp.tile(g[1, 1], 2 * B_HEADS), one(512),
             jnp.tile(g[2, 0], C_HEADS) * sc, jnp.tile(g[2, 1], C_KV), one(128),
             jnp.tile(g[3, 0], 24) * sc, jnp.tile(g[3, 1], 24), one(1536),
             one(IN_WIDTH - OFF_GATE)]
    return jnp.concatenate(parts).reshape(1, IN_WIDTH)


def _prenorm_kernel(x_ref, g_ref, o_ref):
    x = x_ref[...]
    ms = jnp.mean(x * x, axis=-1, keepdims=True)
    o_ref[...] = (x * lax.rsqrt(ms + EPS) * g_ref[...]).astype(o_ref.dtype)


def _prenorm(x, g, tm=1024):
    n = x.shape[0]
    return pl.pallas_call(
        _prenorm_kernel,
        grid=(n // tm,),
        in_specs=[pl.BlockSpec((tm, D_MODEL), lambda i: (i, 0)),
                  pl.BlockSpec((1, D_MODEL), lambda i: (0, 0))],
        out_specs=pl.BlockSpec((tm, D_MODEL), lambda i: (i, 0)),
        out_shape=jax.ShapeDtypeStruct((n, D_MODEL), BF16),
        compiler_params=pltpu.CompilerParams(dimension_semantics=("parallel",), vmem_limit_bytes=VMEM_LIMIT),
    )(x, g)


PROJ_TM = 2048
PROJ_TN = 512
PROJ_CHUNK = 256
NORM_W = 256
DEINT_STRIDE = 4


def _proj_kernel(*refs, kind, r, tm, tn):
    refs = list(refs)
    h_ref, w_ref = refs[:2]
    pos = 2
    if kind in (K_NORM, K_ROPE):
        gain_ref, flag_ref, bd_ref = refs[pos:pos + 3]; pos += 3
    if kind == K_ROPE:
        rflag_ref, cos_ref, sin_ref = refs[pos:pos + 3]; pos += 3
    o_ref = refs[pos]; pos += 1
    slab_ref = refs[pos] if r > 1 else None
    slab2_ref = refs[pos + 1] if r > DEINT_STRIDE else None

    nchunk = tm // PROJ_CHUNK

    def main(rc):
        return jnp.dot(h_ref[rc * PROJ_CHUNK:(rc + 1) * PROJ_CHUNK, :], w_ref[...], preferred_element_type=F32)

    nxt = main(0)
    for rc in range(nchunk):
        rows = slice(rc * PROJ_CHUNK, (rc + 1) * PROJ_CHUNK)
        acc = nxt
        if rc + 1 < nchunk:
            nxt = main(rc + 1)
        if kind in (K_NORM, K_ROPE):
            sq = (acc * acc).astype(BF16)
            bd = bd_ref[...]
            ms = jnp.concatenate([jnp.dot(sq[:, c:c + NORM_W], bd, preferred_element_type=F32)
                                  for c in range(0, tn, NORM_W)], axis=1) * (1.0 / HEAD_DIM)
            y = acc * jnp.where(flag_ref[...] > 0.0, lax.rsqrt(ms + EPS), 1.0) * gain_ref[...]
            if kind == K_ROPE:
                c = jnp.concatenate([cos_ref[rows, :]] * (tn // LANES), axis=1)
                s = jnp.concatenate([sin_ref[rows, :]] * (tn // LANES), axis=1)
                lane = lax.broadcasted_iota(jnp.int32, y.shape, 1) & (HEAD_DIM - 1)
                half = HEAD_DIM // 2
                partner = jnp.where(lane < half, pltpu.roll(y, tn - half, axis=1), pltpu.roll(y, half, axis=1))
                y = jnp.where(rflag_ref[...] > 0.0, y * c + partner * s, y)
        elif kind == K_SILU:
            y = acc * (0.5 * jnp.tanh(0.5 * acc) + 0.5)
        elif kind == K_SIGMOID:
            y = 0.5 * jnp.tanh(0.5 * acc) + 0.5
        else:
            y = acc
        if r == 1:
            o_ref[rows, :] = y.astype(o_ref.dtype)
        else:
            for s_ in range(tn // LANES):
                slab_ref[s_, rows, :] = y[:, s_ * LANES:(s_ + 1) * LANES]
    if r > 1:
        r1 = min(r, DEINT_STRIDE)
        r2 = r // r1
        for s_ in range(tn // LANES):
            if r2 > 1:
                for c1 in range(r1):
                    slab2_ref[s_, c1 * (tm // r1):(c1 + 1) * (tm // r1), :] = (
                        slab_ref[s_, pl.ds(c1, tm // r1, stride=r1), :])
            for c1 in range(r1):
                for c2 in range(r2):
                    if r2 > 1:
                        rows_c = slab2_ref[s_, pl.ds(c1 * (tm // r1) + c2, tm // r, stride=r2), :]
                    else:
                        rows_c = slab_ref[s_, pl.ds(c1, tm // r, stride=r), :]
                    o_ref[c1 + r1 * c2, :, s_ * LANES:(s_ + 1) * LANES] = rows_c.astype(o_ref.dtype)


def _proj_call(h, w, *, kind, col0, step, nj, seq, r=1, gain=None, flag=None, rflag=None, bd=None, cos=None,
               sin=None):
    tm, tn = PROJ_TM, PROJ_TN
    n = h.shape[0]
    ns = seq // tm
    cb = col0 // tn
    col_spec = pl.BlockSpec((1, tn), lambda i, j: (0, cb + j * step))
    in_specs = [pl.BlockSpec((tm, D_MODEL), lambda i, j: (i, 0)),
                pl.BlockSpec((D_MODEL, tn), lambda i, j: (0, cb + j * step))]
    args = [h, w]
    if kind in (K_NORM, K_ROPE):
        in_specs += [col_spec, col_spec, pl.BlockSpec((NORM_W, NORM_W), lambda i, j: (0, 0))]
        args += [gain, flag, bd]
    if kind == K_ROPE:
        in_specs += [col_spec] + [pl.BlockSpec((tm, LANES), lambda i, j: (i % ns, 0))] * 2
        args += [rflag, cos, sin]
    scratch = []
    if r == 1:
        out_shape = jax.ShapeDtypeStruct((n, nj * tn), BF16)
        out_spec = pl.BlockSpec((tm, tn), lambda i, j: (i, j))
    else:
        out_shape = jax.ShapeDtypeStruct((n // seq, r, seq // r, nj * tn), BF16)
        out_spec = pl.BlockSpec((None, r, tm // r, tn), lambda i, j: (i // ns, 0, i % ns, j))
        scratch = [pltpu.VMEM((tn // LANES, tm, LANES), F32)] * (2 if r > DEINT_STRIDE else 1)
    return pl.pallas_call(
        functools.partial(_proj_kernel, kind=kind, r=r, tm=tm, tn=tn),
        grid=(n // tm, nj),
        in_specs=in_specs,
        out_specs=out_spec,
        out_shape=out_shape,
        scratch_shapes=scratch,
        compiler_params=pltpu.CompilerParams(
            dimension_semantics=("parallel", "parallel"), vmem_limit_bytes=VMEM_LIMIT),
    )(*args)


def _bucket(rel):
    nb = REL_BUCKETS // 2
    exact = nb // 2
    n = jnp.abs(rel)
    large = exact + (jnp.log(jnp.maximum(n, exact).astype(F32) / exact)
                     / math.log(REL_MAX_DIST / exact) * (nb - exact)).astype(jnp.int32)
    large = jnp.minimum(large, nb - 1)
    return jnp.where(rel > 0, nb, 0) + jnp.where(n < exact, n, large)


def _lookup(tbl, bucket):
    shape = (tbl.shape[1],) + (1,) * bucket.ndim
    out = jnp.zeros((tbl.shape[1],) + bucket.shape, F32)
    for j in range(REL_BUCKETS):
        out = jnp.where(bucket[None] == j, tbl[j].reshape(shape), out)
    return out


def _toeplitz_kernel(u_ref, o_ref, *, n_rows, n_cols):
    x = jnp.broadcast_to(u_ref[...], (n_rows, u_ref.shape[-1]))
    o_ref[...] = pltpu.roll(x, 0, 1, stride=1, stride_axis=0)[:, n_rows:n_rows + n_cols]


def _toeplitz_tiles(vec, starts, n_rows, n_cols, out_shape, out_index):
    assert n_rows % LANES == 0 and n_cols % LANES == 0
    nh = vec.shape[0]
    span = n_rows + n_cols - 1
    lp = _ceil_div(1 + span, LANES) * LANES
    rows = [jnp.pad(vec[:, s - (n_cols - 1):s + n_rows][:, ::-1], ((0, 0), (1, lp - 1 - span))) for s in starts]
    u = jnp.stack(rows, axis=1)[:, :, None, :]
    return pl.pallas_call(
        functools.partial(_toeplitz_kernel, n_rows=n_rows, n_cols=n_cols),
        grid=(nh, len(starts)),
        in_specs=[pl.BlockSpec((None, None, 1, lp), lambda h, t: (h, t, 0, 0))],
        out_specs=pl.BlockSpec((None, None, n_rows, n_cols), out_index),
        out_shape=jax.ShapeDtypeStruct(out_shape, F32),
        compiler_params=pltpu.CompilerParams(dimension_semantics=("parallel", "parallel")),
    )(u)


def _band_bias(tbl, stride, half, tq, w, pad, hpc):
    nh = tbl.shape[1]
    r0 = w + tq
    rel = jnp.arange(-r0, r0 + 1)
    vec = jnp.where((jnp.abs(rel) <= half)[None], _lookup(tbl, _bucket(rel * stride)) * LOG2E, NEG)
    starts = [r0 - off for off in (0, pad, w - tq)]
    return _toeplitz_tiles(vec, starts, w, tq, (3, nh // hpc, w, hpc * tq),
                           lambda h, t: (t, h // hpc, 0, h % hpc))


BAND_HPC = 4
BAND_NSUB = 4


def _band_kernel(*refs, qi_axis, shared_kv, tq, nsub, w, pad, m_len, has_sink, lse_out):
    refs = list(refs)
    q_ref, k_ref, v_ref, bias_ref = refs[:4]
    pos = 4
    sink_ref = lse_ref = None
    if has_sink:
        sink_ref = refs[pos]; pos += 1
    o_ref = refs[pos]; pos += 1
    if lse_out:
        lse_ref = refs[pos]

    hpc = BAND_HPC
    kw = HEAD_DIM if shared_kv else hpc * HEAD_DIM
    last = m_len // tq - 1
    qi = pl.program_id(qi_axis)
    qt = q_ref[...].T
    ones = jnp.ones((ONES_ROWS, w), BF16)
    for j in range(nsub):
        qb = qi * nsub + j
        ks = pl.multiple_of(jnp.clip(qb * tq - pad, 0, m_len - w), 64)
        case = jnp.where(qb == 0, 0, jnp.where(qb == last, 2, 1))
        for c in range(q_ref.shape[1] // (hpc * HEAD_DIM)):
            slab = qt[c * hpc * HEAD_DIM:(c + 1) * hpc * HEAD_DIM, j * tq:(j + 1) * tq]
            if shared_kv:
                qc = jnp.concatenate([slab[g * HEAD_DIM:(g + 1) * HEAD_DIM] for g in range(hpc)], axis=1)
            else:
                head = lax.broadcasted_iota(jnp.int32, slab.shape, 0) // HEAD_DIM
                qc = jnp.concatenate([jnp.where(head == g, slab, jnp.zeros_like(slab)) for g in range(hpc)],
                                     axis=1)
            k = k_ref[pl.ds(ks, w), c * kw:(c + 1) * kw]
            v = v_ref[pl.ds(ks, w), c * kw:(c + 1) * kw]
            vt = jnp.concatenate([v.T, ones], axis=0)
            st = jnp.dot(k, qc, preferred_element_type=F32) + bias_ref[case, c]
            m = jnp.max(st, axis=0, keepdims=True)
            if has_sink:
                sk = sink_ref[c]
                m = jnp.maximum(m, sk)
            pt = jnp.exp2(st - m).astype(BF16)
            acc = jnp.dot(vt, pt, preferred_element_type=F32)
            l = acc[kw:kw + 1]
            if has_sink:
                l = l + jnp.exp2(sk - m)
            lse = m + jnp.log2(l)
            for g in range(0, hpc, 2):
                ots, lses = [], []
                for gg in (g, g + 1):
                    lanes = slice(gg * tq, (gg + 1) * tq)
                    rows = slice(0, HEAD_DIM) if shared_kv else slice(gg * HEAD_DIM, (gg + 1) * HEAD_DIM)
                    ots.append(acc[rows, lanes] / l[:, lanes])
                    lses.append(jnp.broadcast_to(lse[:, lanes], (HEAD_DIM, tq)))
                out_rows = slice(j * tq, (j + 1) * tq)
                cols = slice((c * hpc + g) * HEAD_DIM, (c * hpc + g + 2) * HEAD_DIM)
                o_ref[out_rows, cols] = jnp.concatenate(ots, axis=0).T.astype(o_ref.dtype)
                if lse_out:
                    lse_ref[out_rows, cols] = jnp.concatenate(lses, axis=0).T


def _window_attention(pn, pr, bias, sink_rows, tq, nsub):
    bsz, seq, _ = pn.shape
    pad = A_WIN
    w = tq + 2 * pad
    lanes = BAND_HPC * tq
    kern = functools.partial(_band_kernel, qi_axis=1, shared_kv=True, tq=tq, nsub=nsub, w=w, pad=pad, m_len=seq,
                             has_sink=True, lse_out=False)
    return pl.pallas_call(
        kern,
        grid=(bsz, seq // (nsub * tq)),
        in_specs=[
            pl.BlockSpec((None, nsub * tq, 512), lambda b, qi: (b, qi, 0)),
            pl.BlockSpec((None, seq, 128), lambda b, qi: (b, 0, 6)),
            pl.BlockSpec((None, seq, 128), lambda b, qi: (b, 0, 7)),
            pl.BlockSpec((3, A_KV, w, lanes), lambda b, qi: (0, 0, 0, 0)),
            pl.BlockSpec((A_KV, 1, lanes), lambda b, qi: (0, 0, 0)),
        ],
        out_specs=pl.BlockSpec((None, nsub * tq, BRANCH_W), lambda b, qi: (b, qi, 0)),
        out_shape=jax.ShapeDtypeStruct((bsz, seq, BRANCH_W), BF16),
        compiler_params=pltpu.CompilerParams(
            dimension_semantics=("parallel", "arbitrary"), vmem_limit_bytes=VMEM_LIMIT),
    )(pn, pr, pr, bias, sink_rows)


def _dilated_group(qk, v, bias, tq, nsub):
    bsz, r, m_len, _ = v.shape
    pad = 64
    w = min(tq + 2 * pad, m_len)
    nsub = min(nsub, m_len // tq)
    nchain = D_HEADS_PER_GROUP // BAND_HPC
    kern = functools.partial(_band_kernel, qi_axis=2, shared_kv=False, tq=tq, nsub=nsub, w=w, pad=pad,
                             m_len=m_len, has_sink=False, lse_out=True)
    row_spec = pl.BlockSpec((None, None, nsub * tq, BRANCH_W), lambda b, c, qi: (b, c, qi, 0))
    return pl.pallas_call(
        kern,
        grid=(bsz, r, m_len // (nsub * tq)),
        in_specs=[
            row_spec,
            pl.BlockSpec((None, None, m_len, 512), lambda b, c, qi: (b, c, 0, 1)),
            pl.BlockSpec((None, None, m_len, 512), lambda b, c, qi: (b, c, 0, 0)),
            pl.BlockSpec((3, nchain, w, BAND_HPC * tq), lambda b, c, qi: (0, 0, 0, 0)),
        ],
        out_specs=[row_spec, row_spec],
        out_shape=[jax.ShapeDtypeStruct((bsz, r, m_len, BRANCH_W), BF16),
                   jax.ShapeDtypeStruct((bsz, r, m_len, BRANCH_W), F32)],
        compiler_params=pltpu.CompilerParams(
            dimension_semantics=("parallel", "parallel", "arbitrary"), vmem_limit_bytes=VMEM_LIMIT),
    )(qk, qk, v, bias)


DIFF_UNIT = 512


def _saturation_distance():
    nb = REL_BUCKETS // 2
    exact = nb // 2
    n = np.arange(exact, 4 * REL_MAX_DIST)
    large = exact + (np.log(n.astype(np.float32) / exact) / math.log(REL_MAX_DIST / exact)
                     * (nb - exact)).astype(np.int32)
    return int(n[large < nb - 1].max()) + 1 + 2


def _ceil_div(a, b):
    return -(-a // b)


def _diff_tiles(tq, tk):
    far = _saturation_distance()
    return -_ceil_div(far + tk - 1, DIFF_UNIT), _ceil_div(far + tq - 1, DIFF_UNIT)


def _diff_kernel(lv_ref, sg_ref, q_ref, k_ref, v_ref, bias_ref, o_ref, st_ref, *, tq, tk, nk, lo, hi, lam_init):
    qi = pl.program_id(2)
    vd = 2 * HEAD_DIM
    qt = q_ref[...].T
    first = lax.broadcasted_iota(jnp.int32, (vd, tq), 0) < HEAD_DIM
    zero = jnp.zeros((vd, tq), BF16)
    qbd = jnp.concatenate([jnp.where(first, qt, zero), jnp.where(first, zero, qt)], axis=1)
    m = jnp.full((1, 2 * tq), NEG, F32)
    acc = jnp.zeros((vd + ONES_ROWS, 2 * tq), F32)
    ones = jnp.ones((ONES_ROWS, tk), BF16)

    def scores(t):
        dd = jnp.clip(t * (tk // DIFF_UNIT) - qi * (tq // DIFF_UNIT), lo, hi) - lo
        b = bias_ref[dd]
        st = jnp.dot(k_ref[t * tk:(t + 1) * tk, :], qbd, preferred_element_type=F32)
        return jnp.concatenate([st[:, :tq] + b, st[:, tq:] + b], axis=1)

    st_ref[0] = scores(0)
    for t in range(nk):
        if t + 1 < nk:
            st_ref[(t + 1) % 2] = scores(t + 1)
        st = st_ref[t % 2]
        vt = jnp.concatenate([v_ref[t * tk:(t + 1) * tk, :].T, ones], axis=0)
        m_new = jnp.maximum(m, jnp.max(st, axis=0, keepdims=True))
        pt = jnp.exp2(st - m_new).astype(BF16)
        acc = jnp.exp2(m - m_new) * acc + jnp.dot(vt, pt, preferred_element_type=F32)
        m = m_new

    lv = lv_ref[...]
    lam = (jnp.exp(jnp.sum(lv[0:1] * lv[1:2], axis=-1, keepdims=True))
           - jnp.exp(jnp.sum(lv[2:3] * lv[3:4], axis=-1, keepdims=True)) + lam_init)
    ot = acc[:vd, :tq] / acc[vd:vd + 1, :tq] - lam * (acc[:vd, tq:] / acc[vd:vd + 1, tq:])
    ms = jnp.mean(ot * ot, axis=0, keepdims=True)
    ot = ot * lax.rsqrt(ms + EPS) * sg_ref[...] * (1.0 - lam_init)
    o_ref[...] = ot.T.astype(o_ref.dtype)


def _diff_bias(tbl, tq, tk):
    lo, hi = _diff_tiles(tq, tk)
    r0 = max(-lo, hi) * DIFF_UNIT + tk + tq
    vec = _lookup(tbl, _bucket(jnp.arange(-r0, r0 + 1))) * LOG2E
    starts = [r0 + d * DIFF_UNIT for d in range(lo, hi + 1)]
    return _toeplitz_tiles(vec, starts, tk, tq, (tbl.shape[1], len(starts), tk, tq), lambda h, t: (h, t, 0, 0))


def _diff_attention(pn, pbv, bias, lam_vec, sub_gain, lam_init, tq, tk):
    bsz, seq, _ = pn.shape
    lo, hi = _diff_tiles(tq, tk)
    nt = hi - lo + 1
    vd = 2 * HEAD_DIM
    kern = functools.partial(_diff_kernel, tq=tq, tk=tk, nk=seq // tk, lo=lo, hi=hi, lam_init=lam_init)
    return pl.pallas_call(
        kern,
        grid=(B_HEADS, bsz, seq // tq),
        in_specs=[
            pl.BlockSpec((4, HEAD_DIM), lambda h, b, qi: (0, 0)),
            pl.BlockSpec((vd, 1), lambda h, b, qi: (0, 0)),
            pl.BlockSpec((None, tq, vd), lambda h, b, qi: (b, qi, 512 // vd + h)),
            pl.BlockSpec((None, seq, vd), lambda h, b, qi: (b, 0, 1024 // vd + h)),
            pl.BlockSpec((None, seq, vd), lambda h, b, qi: (b, 0, h)),
            pl.BlockSpec((None, nt, tk, tq), lambda h, b, qi: (h, 0, 0, 0), pipeline_mode=pl.Buffered(1)),
        ],
        out_specs=pl.BlockSpec((None, tq, vd), lambda h, b, qi: (b, qi, h)),
        out_shape=jax.ShapeDtypeStruct((bsz, seq, BRANCH_W), BF16),
        scratch_shapes=[pltpu.VMEM((2, tk, 2 * tq), F32)],
        compiler_params=pltpu.CompilerParams(
            dimension_semantics=("parallel", "parallel", "arbitrary"), vmem_limit_bytes=VMEM_LIMIT),
    )(lam_vec, sub_gain.reshape(vd, 1), pn, pn, pbv, bias)


def _gqa_kernel(q0_ref, q1_ref, k_ref, v_ref, o_ref, st_ref, *, tq, tk, nk):
    grp = C_HEADS // C_KV
    qts = []
    for q_ref in (q0_ref, q1_ref):
        qt = q_ref[...].T
        qts.append(jnp.concatenate([qt[g * HEAD_DIM:(g + 1) * HEAD_DIM] for g in range(grp)], axis=1))
    ones = jnp.ones((ONES_ROWS, tk), BF16)
    state = [(jnp.full((1, grp * tq), NEG, F32), jnp.zeros((HEAD_DIM + ONES_ROWS, grp * tq), F32))
             for _ in range(C_KV)]

    def scores(t, slot):
        for kv in range(C_KV):
            k = k_ref[t * tk:(t + 1) * tk, kv * HEAD_DIM:(kv + 1) * HEAD_DIM]
            st_ref[slot, kv] = jnp.dot(k, qts[kv], preferred_element_type=F32)

    scores(0, 0)
    for t in range(nk):
        if t + 1 < nk:
            scores(t + 1, (t + 1) % 2)
        for kv in range(C_KV):
            m, acc = state[kv]
            st = st_ref[t % 2, kv]
            v = v_ref[t * tk:(t + 1) * tk, kv * HEAD_DIM:(kv + 1) * HEAD_DIM]
            vt = jnp.concatenate([v.T, ones], axis=0)
            m_new = jnp.maximum(m, jnp.max(st, axis=0, keepdims=True))
            pt = jnp.exp2(st - m_new).astype(BF16)
            acc = jnp.exp2(m - m_new) * acc + jnp.dot(vt, pt, preferred_element_type=F32)
            state[kv] = (m_new, acc)

    for kv in range(C_KV):
        acc = state[kv][1]
        ot = acc[:HEAD_DIM] / acc[HEAD_DIM:HEAD_DIM + 1]
        for g in range(grp):
            h = kv * grp + g
            o_ref[:, h * HEAD_DIM:(h + 1) * HEAD_DIM] = ot[:, g * tq:(g + 1) * tq].T.astype(o_ref.dtype)


def _dense_gqa(pr, tq=512, tk=512):
    bsz, seq, _ = pr.shape
    kern = functools.partial(_gqa_kernel, tq=tq, tk=tk, nk=seq // tk)
    return pl.pallas_call(
        kern,
        grid=(bsz, seq // tq),
        in_specs=[
            pl.BlockSpec((None, tq, 256), lambda b, qi: (b, qi, 0)),
            pl.BlockSpec((None, tq, 256), lambda b, qi: (b, qi, 1)),
            pl.BlockSpec((None, seq, 128), lambda b, qi: (b, 0, 4)),
            pl.BlockSpec((None, seq, 128), lambda b, qi: (b, 0, 5)),
        ],
        out_specs=pl.BlockSpec((None, tq, BRANCH_W), lambda b, qi: (b, qi, 0)),
        out_shape=jax.ShapeDtypeStruct((bsz, seq, BRANCH_W), BF16),
        scratch_shapes=[pltpu.VMEM((2, C_KV, tk, (C_HEADS // C_KV) * tq), F32)],
        compiler_params=pltpu.CompilerParams(
            dimension_semantics=("parallel", "arbitrary"), vmem_limit_bytes=VMEM_LIMIT),
    )(pr, pr, pr, pr)


def _merge_kernel(*refs, tm):
    x_ref, oa_ref, ob_ref, oc_ref = refs[:4]
    d_refs = refs[4:10]
    gate_ref, merge_ref, wb_ref, wo_ref, out_ref, so_ref, sl_ref = refs[10:17]
    nslab = BRANCH_W // LANES

    def natural(ref, scr, r):
        if r == 1:
            return ref[0].astype(F32)
        for c in range(r):
            blk = ref[c].astype(F32)
            for s in range(nslab):
                scr[s, pl.ds(c, tm // r, stride=r), :] = blk[:, s * LANES:(s + 1) * LANES]
        return jnp.concatenate([scr[s] for s in range(nslab)], axis=1)

    def gate(n):
        return gate_ref[:, n * BRANCH_W:(n + 1) * BRANCH_W]

    def term(n, gated, half):
        y = jnp.dot(gated, wb_ref[n, :, half * 512:(half + 1) * 512], preferred_element_type=F32)
        return merge_ref[:, n * D_MODEL + half * 512:n * D_MODEL + (half + 1) * 512].astype(F32) * y

    partial = [None, None]
    for n, o_ref in enumerate((oa_ref, ob_ref, oc_ref)):
        gated = o_ref[...] * gate(n)
        for half in range(2):
            t = term(n, gated, half)
            partial[half] = t if partial[half] is None else partial[half] + t

    os_, ls_ = [], []
    for g, (_, r) in enumerate(D_PAIRS):
        os_.append(natural(d_refs[2 * g], so_ref, r))
        ls_.append(natural(d_refs[2 * g + 1], sl_ref, r))
    mx = jnp.maximum(jnp.maximum(ls_[0], ls_[1]), ls_[2])
    es = [jnp.exp2(l - mx) for l in ls_]
    od = (es[0] * os_[0] + es[1] * os_[1] + es[2] * os_[2]) / (es[0] + es[1] + es[2])
    gated_d = (od * gate(N_BRANCH - 1).astype(F32)).astype(BF16)
    merged = jnp.concatenate([(partial[half] + term(N_BRANCH - 1, gated_d, half)).astype(BF16)
                              for half in range(2)], axis=1)
    out_ref[...] = x_ref[...] + jnp.dot(merged, wo_ref[...], preferred_element_type=F32)


def _merge(x, oa, ob, oc, d_outs, gate, merge, wb, wo, seq, tm=512):
    n = x.shape[0]
    ns = seq // tm
    row = pl.BlockSpec((tm, BRANCH_W), lambda i: (i, 0))
    in_specs = [pl.BlockSpec((tm, D_MODEL), lambda i: (i, 0)), row, row, row]
    for _, r in D_PAIRS:
        in_specs += [pl.BlockSpec((None, r, tm // r, BRANCH_W), lambda i: (i // ns, 0, i % ns, 0))] * 2
    in_specs += [pl.BlockSpec((tm, N_BRANCH * BRANCH_W), lambda i: (i, 0)),
                 pl.BlockSpec((tm, N_BRANCH * D_MODEL), lambda i: (i, 0)),
                 pl.BlockSpec((N_BRANCH, BRANCH_W, D_MODEL), lambda i: (0, 0, 0), pipeline_mode=pl.Buffered(1)),
                 pl.BlockSpec((D_MODEL, D_MODEL), lambda i: (0, 0), pipeline_mode=pl.Buffered(1))]
    return pl.pallas_call(
        functools.partial(_merge_kernel, tm=tm),
        grid=(n // tm,),
        in_specs=in_specs,
        out_specs=pl.BlockSpec((tm, D_MODEL), lambda i: (i, 0)),
        out_shape=jax.ShapeDtypeStruct((n, D_MODEL), F32),
        scratch_shapes=[pltpu.VMEM((BRANCH_W // LANES, tm, LANES), F32)] * 2,
        compiler_params=pltpu.CompilerParams(
            dimension_semantics=("parallel",), vmem_limit_bytes=VMEM_LIMIT),
    )(x, oa, ob, oc, *d_outs, gate, merge, wb, wo)


def kernel(x, w_in, w_branch, w_out, norm_gain, qk_gain, sink, lambda_vec, sub_norm_gain, rel_bias):
    bsz, seq, _ = x.shape
    n = bsz * seq
    depth = w_in.shape[0]
    rb = rel_bias.astype(F32)

    rows = seq // GRID_W
    row = jnp.repeat(jnp.arange(rows), GRID_W).astype(F32)
    col = jnp.tile(jnp.arange(GRID_W), rows).astype(F32)
    nf = HEAD_DIM // 4
    freqs = ROPE_THETA ** (-jnp.arange(nf, dtype=F32) / nf)
    ang = jnp.concatenate([row[:, None] * freqs, col[:, None] * freqs], axis=-1)
    cos2 = jnp.tile(jnp.cos(ang), (1, 4))
    sin2 = jnp.tile(jnp.concatenate([-jnp.sin(ang), jnp.sin(ang)], axis=-1), (1, 2))

    tq_a = 128
    bias_a = _band_bias(rb[:, :A_HEADS], 1, A_WIN, tq_a, tq_a + 2 * A_WIN, A_WIN, BAND_HPC)
    tq_b, tk_b = 1024, 512
    bias_b = _diff_bias(rb[:, A_HEADS:A_HEADS + B_HEADS], tq_b, tk_b)
    tq_d = 128
    bias_d = []
    for g, (win, r) in enumerate(D_PAIRS):
        lo = A_HEADS + B_HEADS + g * D_HEADS_PER_GROUP
        wd = min(tq_d + 128, seq // r)
        bias_d.append(_band_bias(rb[:, lo:lo + D_HEADS_PER_GROUP], r, win // (2 * r), tq_d, wd, 64, BAND_HPC))

    flag_np, rflag_np = _col_flags()
    flag = _permute_cols(jnp.asarray(flag_np))
    rflag = _permute_cols(jnp.asarray(rflag_np))
    blk = np.arange(NORM_W) // HEAD_DIM
    bd = jnp.asarray((blk[:, None] == blk[None, :]).astype(np.float32), dtype=BF16)

    xf = x.reshape(n, D_MODEL)
    for l in range(depth):
        h = _prenorm(xf, norm_gain[l].reshape(1, D_MODEL))
        w = _permute_cols(w_in[l].astype(BF16))
        gain = _permute_cols(_gain_cols(qk_gain[l]))
        norm_args = dict(gain=gain, flag=flag, bd=bd)
        pr = _proj_call(h, w, kind=K_ROPE, col0=P_ROPE, step=1, nj=2, seq=seq, rflag=rflag, cos=cos2, sin=sin2,
                        **norm_args).reshape(bsz, seq, -1)
        pn = _proj_call(h, w, kind=K_NORM, col0=P_NORM, step=1, nj=3, seq=seq, **norm_args).reshape(bsz, seq, -1)
        pbv = _proj_call(h, w, kind=K_RAW, col0=P_BV, step=1, nj=1, seq=seq).reshape(bsz, seq, -1)
        gate = _proj_call(h, w, kind=K_SILU, col0=OFF_GATE, step=1, nj=4, seq=seq)
        merge = _proj_call(h, w, kind=K_SIGMOID, col0=OFF_MERGE, step=1, nj=8, seq=seq)
        sink_rows = jnp.repeat(sink[l].astype(F32) * LOG2E, tq_a).reshape(A_KV, 1, BAND_HPC * tq_a)
        oa = _window_attention(pn, pr, bias_a, sink_rows, tq=tq_a, nsub=BAND_NSUB)
        lam_init = 0.8 - 0.6 * math.exp(-0.3 * l)
        ob = _diff_attention(pn, pbv, bias_b, lambda_vec[l], sub_norm_gain[l], lam_init, tq=tq_b, tk=tk_b)
        oc = _dense_gqa(pr)
        d_outs = []
        for g, (win, r) in enumerate(D_PAIRS):
            dqk = _proj_call(h, w, kind=K_NORM, col0=OFF_DQ + g * 512, step=(OFF_DK - OFF_DQ) // 512, nj=2,
                             seq=seq, r=r, **norm_args)
            dv = _proj_call(h, w, kind=K_RAW, col0=OFF_DV + g * 512, step=1, nj=1, seq=seq, r=r)
            if r == 1:
                dqk, dv = dqk.reshape(bsz, 1, seq, -1), dv.reshape(bsz, 1, seq, -1)
            d_outs += _dilated_group(dqk, dv, bias_d[g], tq=tq_d, nsub=BAND_NSUB)
        xf = _merge(xf, oa.reshape(n, -1), ob.reshape(n, -1), oc.reshape(n, -1), d_outs, gate, merge,
                    w_branch[l].astype(BF16), w_out[l].astype(BF16), seq)
    return xf.reshape(bsz, seq, D_MODEL)
```

```python
import functools
import math

import numpy as np
import jax
import jax.numpy as jnp
from jax import lax
from jax.experimental import pallas as pl
from jax.experimental.pallas import tpu as pltpu

F32 = jnp.float32
BF16 = jnp.bfloat16

D_MODEL = 1024
HEAD_DIM = 64
GRID_W = 64
EPS = 1e-6
A_HEADS, A_KV, A_WIN = 8, 2, 128
B_HEADS = 4
C_HEADS, C_KV = 8, 2
ROPE_THETA = 10000.0
D_PAIRS = ((128, 1), (512, 4), (2048, 16))
D_HEADS_PER_GROUP = 8
N_BRANCH = 4
BRANCH_W = 512
REL_BUCKETS = 32
REL_MAX_DIST = 1024
IN_WIDTH = 13824
NEG = -1e30
LOG2E = math.log2(math.e)
LANES = 128
ONES_ROWS = 16

OFF_AQ, OFF_AK, OFF_AV = 0, 512, 640
OFF_BQ, OFF_BK, OFF_BV = 768, 1280, 1792
OFF_CQ, OFF_CK, OFF_CV = 2304, 2816, 2944
OFF_DQ, OFF_DK, OFF_DV = 3072, 4608, 6144
OFF_GATE, OFF_MERGE = 7680, 9728
PERM = ((OFF_CQ, 768), (OFF_AK, 256), (OFF_AQ, 512), (OFF_BQ, 1024), (OFF_BV, 512))
P_ROPE, P_NORM, P_BV = 0, 1024, 2560

VMEM_LIMIT = 56 * 1024 * 1024

K_RAW, K_NORM, K_ROPE, K_SILU, K_SIGMOID = range(5)


def _permute_cols(a):
    return jnp.concatenate([a[..., lo:lo + n] for lo, n in PERM] + [a[..., OFF_DQ:]], axis=-1)


def _col_flags():
    norm = np.zeros((IN_WIDTH,), np.float32)
    for lo, hi in ((OFF_AQ, OFF_AV), (OFF_BQ, OFF_BV), (OFF_CQ, OFF_CV), (OFF_DQ, OFF_DV)):
        norm[lo:hi] = 1.0
    rope = np.zeros((IN_WIDTH,), np.float32)
    rope[OFF_CQ:OFF_CV] = 1.0
    return norm.reshape(1, IN_WIDTH), rope.reshape(1, IN_WIDTH)


def _gain_cols(g):
    sc = HEAD_DIM ** -0.5 * LOG2E
    one = lambda n: jnp.ones((n,), F32)
    parts = [jnp.tile(g[0, 0], A_HEADS) * sc, jnp.tile(g[0, 1], A_KV), one(128),
             jnp.tile(g[1, 0], 2 * B_HEADS) * sc, jnp.tile(g[1, 1], 2 * B_HEADS), one(512),
             jnp.tile(g[2, 0], C_HEADS) * sc, jnp.tile(g[2, 1], C_KV), one(128),
             jnp.tile(g[3, 0], 24) * sc, jnp.tile(g[3, 1], 24), one(1536),
             one(IN_WIDTH - OFF_GATE)]
    return jnp.concatenate(parts).reshape(1, IN_WIDTH)


def _prenorm_kernel(x_ref, g_ref, o_ref):
    x = x_ref[...]
    ms = jnp.mean(x * x, axis=-1, keepdims=True)
    o_ref[...] = (x * lax.rsqrt(ms + EPS) * g_ref[...]).astype(o_ref.dtype)


def _prenorm(x, g, tm=1024):
    n = x.shape[0]
    return pl.pallas_call(
        _prenorm_kernel,
        grid=(n // tm,),
        in_specs=[pl.BlockSpec((tm, D_MODEL), lambda i: (i, 0)),
                  pl.BlockSpec((1, D_MODEL), lambda i: (0, 0))],
        out_specs=pl.BlockSpec((tm, D_MODEL), lambda i: (i, 0)),
        out_shape=jax.ShapeDtypeStruct((n, D_MODEL), BF16),
        compiler_params=pltpu.CompilerParams(dimension_semantics=("parallel",), vmem_limit_bytes=VMEM_LIMIT),
    )(x, g)


PROJ_TM = 4096
PROJ_TN = 512
PROJ_CHUNK = 256
NORM_W = 256
DEINT_STRIDE = 4


def _proj_kernel(*refs, kind, r, tm, tn):
    refs = list(refs)
    h_ref, w_ref = refs[:2]
    pos = 2
    if kind in (K_NORM, K_ROPE):
        gain_ref, flag_ref, bd_ref = refs[pos:pos + 3]; pos += 3
    if kind == K_ROPE:
        rflag_ref, cos_ref, sin_ref = refs[pos:pos + 3]; pos += 3
    o_ref = refs[pos]; pos += 1
    slab_ref = refs[pos] if r > 1 else None
    slab2_ref = refs[pos + 1] if r > DEINT_STRIDE else None

    nchunk = tm // PROJ_CHUNK

    def main(rc):
        return jnp.dot(h_ref[rc * PROJ_CHUNK:(rc + 1) * PROJ_CHUNK, :], w_ref[...], preferred_element_type=F32)

    nxt = main(0)
    for rc in range(nchunk):
        rows = slice(rc * PROJ_CHUNK, (rc + 1) * PROJ_CHUNK)
        acc = nxt
        if rc + 1 < nchunk:
            nxt = main(rc + 1)
        if kind in (K_NORM, K_ROPE):
            sq = (acc * acc).astype(BF16)
            bd = bd_ref[...]
            ms = jnp.concatenate([jnp.dot(sq[:, c:c + NORM_W], bd, preferred_element_type=F32)
                                  for c in range(0, tn, NORM_W)], axis=1) * (1.0 / HEAD_DIM)
            y = acc * jnp.where(flag_ref[...] > 0.0, lax.rsqrt(ms + EPS), 1.0) * gain_ref[...]
            if kind == K_ROPE:
                c = jnp.concatenate([cos_ref[rows, :]] * (tn // LANES), axis=1)
                s = jnp.concatenate([sin_ref[rows, :]] * (tn // LANES), axis=1)
                lane = lax.broadcasted_iota(jnp.int32, y.shape, 1) & (HEAD_DIM - 1)
                half = HEAD_DIM // 2
                partner = jnp.where(lane < half, pltpu.roll(y, tn - half, axis=1), pltpu.roll(y, half, axis=1))
                y = jnp.where(rflag_ref[...] > 0.0, y * c + partner * s, y)
        elif kind == K_SILU:
            y = acc * (0.5 * jnp.tanh(0.5 * acc) + 0.5)
        elif kind == K_SIGMOID:
            y = 0.5 * jnp.tanh(0.5 * acc) + 0.5
        else:
            y = acc
        if r == 1:
            o_ref[rows, :] = y.astype(o_ref.dtype)
        else:
            for s_ in range(tn // LANES):
                slab_ref[s_, rows, :] = y[:, s_ * LANES:(s_ + 1) * LANES]
    if r > 1:
        r1 = min(r, DEINT_STRIDE)
        r2 = r // r1
        for s_ in range(tn // LANES):
            if r2 > 1:
                for c1 in range(r1):
                    slab2_ref[s_, c1 * (tm // r1):(c1 + 1) * (tm // r1), :] = (
                        slab_ref[s_, pl.ds(c1, tm // r1, stride=r1), :])
            for c1 in range(r1):
                for c2 in range(r2):
                    if r2 > 1:
                        rows_c = slab2_ref[s_, pl.ds(c1 * (tm // r1) + c2, tm // r, stride=r2), :]
                    else:
                        rows_c = slab_ref[s_, pl.ds(c1, tm // r, stride=r), :]
                    o_ref[c1 + r1 * c2, :, s_ * LANES:(s_ + 1) * LANES] = rows_c.astype(o_ref.dtype)


def _proj_call(h, w, *, kind, col0, step, nj, seq, r=1, gain=None, flag=None, rflag=None, bd=None, cos=None,
               sin=None):
    tm, tn = PROJ_TM, PROJ_TN
    n = h.shape[0]
    ns = seq // tm
    cb = col0 // tn
    col_spec = pl.BlockSpec((1, tn), lambda i, j: (0, cb + j * step))
    in_specs = [pl.BlockSpec((tm, D_MODEL), lambda i, j: (i, 0)),
                pl.BlockSpec((D_MODEL, tn), lambda i, j: (0, cb + j * step))]
    args = [h, w]
    if kind in (K_NORM, K_ROPE):
        in_specs += [col_spec, col_spec, pl.BlockSpec((NORM_W, NORM_W), lambda i, j: (0, 0))]
        args += [gain, flag, bd]
    if kind == K_ROPE:
        in_specs += [col_spec] + [pl.BlockSpec((tm, LANES), lambda i, j: (i % ns, 0))] * 2
        args += [rflag, cos, sin]
    scratch = []
    if r == 1:
        out_shape = jax.ShapeDtypeStruct((n, nj * tn), BF16)
        out_spec = pl.BlockSpec((tm, tn), lambda i, j: (i, j))
    else:
        out_shape = jax.ShapeDtypeStruct((n // seq, r, seq // r, nj * tn), BF16)
        out_spec = pl.BlockSpec((None, r, tm // r, tn), lambda i, j: (i // ns, 0, i % ns, j))
        scratch = [pltpu.VMEM((tn // LANES, tm, LANES), F32)] * (2 if r > DEINT_STRIDE else 1)
    return pl.pallas_call(
        functools.partial(_proj_kernel, kind=kind, r=r, tm=tm, tn=tn),
        grid=(n // tm, nj),
        in_specs=in_specs,
        out_specs=out_spec,
        out_shape=out_shape,
        scratch_shapes=scratch,
        compiler_params=pltpu.CompilerParams(
            dimension_semantics=("parallel", "parallel"), vmem_limit_bytes=VMEM_LIMIT),
    )(*args)


def _bucket(rel):
    nb = REL_BUCKETS // 2
    exact = nb // 2
    n = jnp.abs(rel)
    large = exact + (jnp.log(jnp.maximum(n, exact).astype(F32) / exact)
                     / math.log(REL_MAX_DIST / exact) * (nb - exact)).astype(jnp.int32)
    large = jnp.minimum(large, nb - 1)
    return jnp.where(rel > 0, nb, 0) + jnp.where(n < exact, n, large)


def _lookup(tbl, bucket):
    shape = (tbl.shape[1],) + (1,) * bucket.ndim
    out = jnp.zeros((tbl.shape[1],) + bucket.shape, F32)
    for j in range(REL_BUCKETS):
        out = jnp.where(bucket[None] == j, tbl[j].reshape(shape), out)
    return out


def _toeplitz_kernel(u_ref, o_ref, *, n_rows, n_cols):
    x = jnp.broadcast_to(u_ref[...], (n_rows, u_ref.shape[-1]))
    o_ref[...] = pltpu.roll(x, 0, 1, stride=1, stride_axis=0)[:, n_rows:n_rows + n_cols]


def _toeplitz_tiles(vec, starts, n_rows, n_cols, out_shape, out_index):
    assert n_rows % LANES == 0 and n_cols % LANES == 0
    nh = vec.shape[0]
    span = n_rows + n_cols - 1
    lp = _ceil_div(1 + span, LANES) * LANES
    rows = [jnp.pad(vec[:, s - (n_cols - 1):s + n_rows][:, ::-1], ((0, 0), (1, lp - 1 - span))) for s in starts]
    u = jnp.stack(rows, axis=1)[:, :, None, :]
    return pl.pallas_call(
        functools.partial(_toeplitz_kernel, n_rows=n_rows, n_cols=n_cols),
        grid=(nh, len(starts)),
        in_specs=[pl.BlockSpec((None, None, 1, lp), lambda h, t: (h, t, 0, 0))],
        out_specs=pl.BlockSpec((None, None, n_rows, n_cols), out_index),
        out_shape=jax.ShapeDtypeStruct(out_shape, F32),
        compiler_params=pltpu.CompilerParams(dimension_semantics=("parallel", "parallel")),
    )(u)


def _band_bias(tbl, stride, half, tq, w, pad, hpc):
    nh = tbl.shape[1]
    r0 = w + tq
    rel = jnp.arange(-r0, r0 + 1)
    vec = jnp.where((jnp.abs(rel) <= half)[None], _lookup(tbl, _bucket(rel * stride)) * LOG2E, NEG)
    starts = [r0 - off for off in (0, pad, w - tq)]
    return _toeplitz_tiles(vec, starts, w, tq, (3, nh // hpc, w, hpc * tq),
                           lambda h, t: (t, h // hpc, 0, h % hpc))


BAND_HPC = 4
BAND_NSUB = 8


def _band_kernel(*refs, qi_axis, shared_kv, tq, nsub, w, pad, m_len, has_sink, lse_out):
    refs = list(refs)
    q_ref, k_ref, v_ref, bias_ref = refs[:4]
    pos = 4
    sink_ref = lse_ref = None
    if has_sink:
        sink_ref = refs[pos]; pos += 1
    o_ref = refs[pos]; pos += 1
    if lse_out:
        lse_ref = refs[pos]

    hpc = BAND_HPC
    kw = HEAD_DIM if shared_kv else hpc * HEAD_DIM
    last = m_len // tq - 1
    qi = pl.program_id(qi_axis)
    qt = q_ref[...].T
    ones = jnp.ones((ONES_ROWS, w), BF16)
    for j in range(nsub):
        qb = qi * nsub + j
        ks = pl.multiple_of(jnp.clip(qb * tq - pad, 0, m_len - w), 64)
        case = jnp.where(qb == 0, 0, jnp.where(qb == last, 2, 1))
        for c in range(q_ref.shape[1] // (hpc * HEAD_DIM)):
            slab = qt[c * hpc * HEAD_DIM:(c + 1) * hpc * HEAD_DIM, j * tq:(j + 1) * tq]
            if shared_kv:
                qc = jnp.concatenate([slab[g * HEAD_DIM:(g + 1) * HEAD_DIM] for g in range(hpc)], axis=1)
            else:
                head = lax.broadcasted_iota(jnp.int32, slab.shape, 0) // HEAD_DIM
                qc = jnp.concatenate([jnp.where(head == g, slab, jnp.zeros_like(slab)) for g in range(hpc)],
                                     axis=1)
            k = k_ref[pl.ds(ks, w), c * kw:(c + 1) * kw]
            v = v_ref[pl.ds(ks, w), c * kw:(c + 1) * kw]
            vt = jnp.concatenate([v.T, ones], axis=0)
            st = jnp.dot(k, qc, preferred_element_type=F32) + bias_ref[case, c]
            m = jnp.max(st, axis=0, keepdims=True)
            if has_sink:
                sk = sink_ref[c]
                m = jnp.maximum(m, sk)
            pt = jnp.exp2(st - m).astype(BF16)
            acc = jnp.dot(vt, pt, preferred_element_type=F32)
            l = acc[kw:kw + 1]
            if has_sink:
                l = l + jnp.exp2(sk - m)
            lse = m + jnp.log2(l)
            for g in range(0, hpc, 2):
                ots, lses = [], []
                for gg in (g, g + 1):
                    lanes = slice(gg * tq, (gg + 1) * tq)
                    rows = slice(0, HEAD_DIM) if shared_kv else slice(gg * HEAD_DIM, (gg + 1) * HEAD_DIM)
                    ots.append(acc[rows, lanes] / l[:, lanes])
                    lses.append(jnp.broadcast_to(lse[:, lanes], (HEAD_DIM, tq)))
                out_rows = slice(j * tq, (j + 1) * tq)
                cols = slice((c * hpc + g) * HEAD_DIM, (c * hpc + g + 2) * HEAD_DIM)
                o_ref[out_rows, cols] = jnp.concatenate(ots, axis=0).T.astype(o_ref.dtype)
                if lse_out:
                    lse_ref[out_rows, cols] = jnp.concatenate(lses, axis=0).T


def _window_attention(pn, pr, bias, sink_rows, tq, nsub):
    bsz, seq, _ = pn.shape
    pad = A_WIN
    w = tq + 2 * pad
    lanes = BAND_HPC * tq
    kern = functools.partial(_band_kernel, qi_axis=1, shared_kv=True, tq=tq, nsub=nsub, w=w, pad=pad, m_len=seq,
                             has_sink=True, lse_out=False)
    return pl.pallas_call(
        kern,
        grid=(bsz, seq // (nsub * tq)),
        in_specs=[
            pl.BlockSpec((None, nsub * tq, 512), lambda b, qi: (b, qi, 0)),
            pl.BlockSpec((None, seq, 128), lambda b, qi: (b, 0, 6)),
            pl.BlockSpec((None, seq, 128), lambda b, qi: (b, 0, 7)),
            pl.BlockSpec((3, A_KV, w, lanes), lambda b, qi: (0, 0, 0, 0)),
            pl.BlockSpec((A_KV, 1, lanes), lambda b, qi: (0, 0, 0)),
        ],
        out_specs=pl.BlockSpec((None, nsub * tq, BRANCH_W), lambda b, qi: (b, qi, 0)),
        out_shape=jax.ShapeDtypeStruct((bsz, seq, BRANCH_W), BF16),
        compiler_params=pltpu.CompilerParams(
            dimension_semantics=("parallel", "arbitrary"), vmem_limit_bytes=VMEM_LIMIT),
    )(pn, pr, pr, bias, sink_rows)


def _dilated_group(qk, v, bias, tq, nsub):
    bsz, r, m_len, _ = v.shape
    pad = 64
    w = min(tq + 2 * pad, m_len)
    nsub = min(nsub, m_len // tq)
    nchain = D_HEADS_PER_GROUP // BAND_HPC
    kern = functools.partial(_band_kernel, qi_axis=2, shared_kv=False, tq=tq, nsub=nsub, w=w, pad=pad,
                             m_len=m_len, has_sink=False, lse_out=True)
    row_spec = pl.BlockSpec((None, None, nsub * tq, BRANCH_W), lambda b, c, qi: (b, c, qi, 0))
    return pl.pallas_call(
        kern,
        grid=(bsz, r, m_len // (nsub * tq)),
        in_specs=[
            row_spec,
            pl.BlockSpec((None, None, m_len, 512), lambda b, c, qi: (b, c, 0, 1)),
            pl.BlockSpec((None, None, m_len, 512), lambda b, c, qi: (b, c, 0, 0)),
            pl.BlockSpec((3, nchain, w, BAND_HPC * tq), lambda b, c, qi: (0, 0, 0, 0)),
        ],
        out_specs=[row_spec, row_spec],
        out_shape=[jax.ShapeDtypeStruct((bsz, r, m_len, BRANCH_W), BF16),
                   jax.ShapeDtypeStruct((bsz, r, m_len, BRANCH_W), F32)],
        compiler_params=pltpu.CompilerParams(
            dimension_semantics=("parallel", "parallel", "arbitrary"), vmem_limit_bytes=VMEM_LIMIT),
    )(qk, qk, v, bias)


DIFF_UNIT = 512


def _saturation_distance():
    nb = REL_BUCKETS // 2
    exact = nb // 2
    n = np.arange(exact, 4 * REL_MAX_DIST)
    large = exact + (np.log(n.astype(np.float32) / exact) / math.log(REL_MAX_DIST / exact)
                     * (nb - exact)).astype(np.int32)
    return int(n[large < nb - 1].max()) + 1 + 2


def _ceil_div(a, b):
    return -(-a // b)


def _diff_tiles(tq, tk):
    far = _saturation_distance()
    return -_ceil_div(far + tk - 1, DIFF_UNIT), _ceil_div(far + tq - 1, DIFF_UNIT)


def _diff_kernel(lv_ref, sg_ref, q_ref, k_ref, v_ref, bias_ref, o_ref, st_ref, *, tq, tk, nk, lo, hi, lam_init):
    qi = pl.program_id(2)
    vd = 2 * HEAD_DIM
    qt = q_ref[...].T
    first = lax.broadcasted_iota(jnp.int32, (vd, tq), 0) < HEAD_DIM
    zero = jnp.zeros((vd, tq), BF16)
    qbd = jnp.concatenate([jnp.where(first, qt, zero), jnp.where(first, zero, qt)], axis=1)
    m = jnp.full((1, 2 * tq), NEG, F32)
    acc = jnp.zeros((vd + ONES_ROWS, 2 * tq), F32)
    ones = jnp.ones((ONES_ROWS, tk), BF16)

    def scores(t):
        dd = jnp.clip(t * (tk // DIFF_UNIT) - qi * (tq // DIFF_UNIT), lo, hi) - lo
        b = bias_ref[dd]
        st = jnp.dot(k_ref[t * tk:(t + 1) * tk, :], qbd, preferred_element_type=F32)
        return jnp.concatenate([st[:, :tq] + b, st[:, tq:] + b], axis=1)

    st_ref[0] = scores(0)
    for t in range(nk):
        if t + 1 < nk:
            st_ref[(t + 1) % 2] = scores(t + 1)
        st = st_ref[t % 2]
        vt = jnp.concatenate([v_ref[t * tk:(t + 1) * tk, :].T, ones], axis=0)
        m_new = jnp.maximum(m, jnp.max(st, axis=0, keepdims=True))
        pt = jnp.exp2(st - m_new).astype(BF16)
        acc = jnp.exp2(m - m_new) * acc + jnp.dot(vt, pt, preferred_element_type=F32)
        m = m_new

    lv = lv_ref[...]
    lam = (jnp.exp(jnp.sum(lv[0:1] * lv[1:2], axis=-1, keepdims=True))
           - jnp.exp(jnp.sum(lv[2:3] * lv[3:4], axis=-1, keepdims=True)) + lam_init)
    ot = acc[:vd, :tq] / acc[vd:vd + 1, :tq] - lam * (acc[:vd, tq:] / acc[vd:vd + 1, tq:])
    ms = jnp.mean(ot * ot, axis=0, keepdims=True)
    ot = ot * lax.rsqrt(ms + EPS) * sg_ref[...] * (1.0 - lam_init)
    o_ref[...] = ot.T.astype(o_ref.dtype)


def _diff_bias(tbl, tq, tk):
    lo, hi = _diff_tiles(tq, tk)
    r0 = max(-lo, hi) * DIFF_UNIT + tk + tq
    vec = _lookup(tbl, _bucket(jnp.arange(-r0, r0 + 1))) * LOG2E
    starts = [r0 + d * DIFF_UNIT for d in range(lo, hi + 1)]
    return _toeplitz_tiles(vec, starts, tk, tq, (tbl.shape[1], len(starts), tk, tq), lambda h, t: (h, t, 0, 0))


def _diff_attention(pn, pbv, bias, lam_vec, sub_gain, lam_init, tq, tk):
    bsz, seq, _ = pn.shape
    lo, hi = _diff_tiles(tq, tk)
    nt = hi - lo + 1
    vd = 2 * HEAD_DIM
    kern = functools.partial(_diff_kernel, tq=tq, tk=tk, nk=seq // tk, lo=lo, hi=hi, lam_init=lam_init)
    return pl.pallas_call(
        kern,
        grid=(B_HEADS, bsz, seq // tq),
        in_specs=[
            pl.BlockSpec((4, HEAD_DIM), lambda h, b, qi: (0, 0)),
            pl.BlockSpec((vd, 1), lambda h, b, qi: (0, 0)),
            pl.BlockSpec((None, tq, vd), lambda h, b, qi: (b, qi, 512 // vd + h)),
            pl.BlockSpec((None, seq, vd), lambda h, b, qi: (b, 0, 1024 // vd + h)),
            pl.BlockSpec((None, seq, vd), lambda h, b, qi: (b, 0, h)),
            pl.BlockSpec((None, nt, tk, tq), lambda h, b, qi: (h, 0, 0, 0), pipeline_mode=pl.Buffered(1)),
        ],
        out_specs=pl.BlockSpec((None, tq, vd), lambda h, b, qi: (b, qi, h)),
        out_shape=jax.ShapeDtypeStruct((bsz, seq, BRANCH_W), BF16),
        scratch_shapes=[pltpu.VMEM((2, tk, 2 * tq), F32)],
        compiler_params=pltpu.CompilerParams(
            dimension_semantics=("parallel", "parallel", "arbitrary"), vmem_limit_bytes=VMEM_LIMIT),
    )(lam_vec, sub_gain.reshape(vd, 1), pn, pn, pbv, bias)


def _gqa_kernel(q0_ref, q1_ref, k_ref, v_ref, o_ref, st_ref, *, tq, tk, nk):
    grp = C_HEADS // C_KV
    qts = []
    for q_ref in (q0_ref, q1_ref):
        qt = q_ref[...].T
        qts.append(jnp.concatenate([qt[g * HEAD_DIM:(g + 1) * HEAD_DIM] for g in range(grp)], axis=1))
    ones = jnp.ones((ONES_ROWS, tk), BF16)
    state = [(jnp.full((1, grp * tq), NEG, F32), jnp.zeros((HEAD_DIM + ONES_ROWS, grp * tq), F32))
             for _ in range(C_KV)]

    def scores(t, slot):
        for kv in range(C_KV):
            k = k_ref[t * tk:(t + 1) * tk, kv * HEAD_DIM:(kv + 1) * HEAD_DIM]
            st_ref[slot, kv] = jnp.dot(k, qts[kv], preferred_element_type=F32)

    scores(0, 0)
    for t in range(nk):
        if t + 1 < nk:
            scores(t + 1, (t + 1) % 2)
        for kv in range(C_KV):
            m, acc = state[kv]
            st = st_ref[t % 2, kv]
            v = v_ref[t * tk:(t + 1) * tk, kv * HEAD_DIM:(kv + 1) * HEAD_DIM]
            vt = jnp.concatenate([v.T, ones], axis=0)
            m_new = jnp.maximum(m, jnp.max(st, axis=0, keepdims=True))
            pt = jnp.exp2(st - m_new).astype(BF16)
            acc = jnp.exp2(m - m_new) * acc + jnp.dot(vt, pt, preferred_element_type=F32)
            state[kv] = (m_new, acc)

    for kv in range(C_KV):
        acc = state[kv][1]
        ot = acc[:HEAD_DIM] / acc[HEAD_DIM:HEAD_DIM + 1]
        for g in range(grp):
            h = kv * grp + g
            o_ref[:, h * HEAD_DIM:(h + 1) * HEAD_DIM] = ot[:, g * tq:(g + 1) * tq].T.astype(o_ref.dtype)


def _dense_gqa(pr, tq=512, tk=512):
    bsz, seq, _ = pr.shape
    kern = functools.partial(_gqa_kernel, tq=tq, tk=tk, nk=seq // tk)
    return pl.pallas_call(
        kern,
        grid=(bsz, seq // tq),
        in_specs=[
            pl.BlockSpec((None, tq, 256), lambda b, qi: (b, qi, 0)),
            pl.BlockSpec((None, tq, 256), lambda b, qi: (b, qi, 1)),
            pl.BlockSpec((None, seq, 128), lambda b, qi: (b, 0, 4)),
            pl.BlockSpec((None, seq, 128), lambda b, qi: (b, 0, 5)),
        ],
        out_specs=pl.BlockSpec((None, tq, BRANCH_W), lambda b, qi: (b, qi, 0)),
        out_shape=jax.ShapeDtypeStruct((bsz, seq, BRANCH_W), BF16),
        scratch_shapes=[pltpu.VMEM((2, C_KV, tk, (C_HEADS // C_KV) * tq), F32)],
        compiler_params=pltpu.CompilerParams(
            dimension_semantics=("parallel", "arbitrary"), vmem_limit_bytes=VMEM_LIMIT),
    )(pr, pr, pr, pr)


def _merge_kernel(*refs, tm):
    x_ref, oa_ref, ob_ref, oc_ref = refs[:4]
    d_refs = refs[4:10]
    gate_ref, merge_ref, wb_ref, wo_ref, out_ref, so_ref, sl_ref = refs[10:17]
    nslab = BRANCH_W // LANES

    def natural(ref, scr, r):
        if r == 1:
            return ref[0].astype(F32)
        for c in range(r):
            blk = ref[c].astype(F32)
            for s in range(nslab):
                scr[s, pl.ds(c, tm // r, stride=r), :] = blk[:, s * LANES:(s + 1) * LANES]
        return jnp.concatenate([scr[s] for s in range(nslab)], axis=1)

    def gate(n):
        return gate_ref[:, n * BRANCH_W:(n + 1) * BRANCH_W]

    def term(n, gated, half):
        y = jnp.dot(gated, wb_ref[n, :, half * 512:(half + 1) * 512], preferred_element_type=F32)
        return merge_ref[:, n * D_MODEL + half * 512:n * D_MODEL + (half + 1) * 512].astype(F32) * y

    partial = [None, None]
    for n, o_ref in enumerate((oa_ref, ob_ref, oc_ref)):
        gated = o_ref[...] * gate(n)
        for half in range(2):
            t = term(n, gated, half)
            partial[half] = t if partial[half] is None else partial[half] + t

    os_, ls_ = [], []
    for g, (_, r) in enumerate(D_PAIRS):
        os_.append(natural(d_refs[2 * g], so_ref, r))
        ls_.append(natural(d_refs[2 * g + 1], sl_ref, r))
    mx = jnp.maximum(jnp.maximum(ls_[0], ls_[1]), ls_[2])
    es = [jnp.exp2(l - mx) for l in ls_]
    od = (es[0] * os_[0] + es[1] * os_[1] + es[2] * os_[2]) / (es[0] + es[1] + es[2])
    gated_d = (od * gate(N_BRANCH - 1).astype(F32)).astype(BF16)
    merged = jnp.concatenate([(partial[half] + term(N_BRANCH - 1, gated_d, half)).astype(BF16)
                              for half in range(2)], axis=1)
    out_ref[...] = x_ref[...] + jnp.dot(merged, wo_ref[...], preferred_element_type=F32)


def _merge(x, oa, ob, oc, d_outs, gate, merge, wb, wo, seq, tm=512):
    n = x.shape[0]
    ns = seq // tm
    row = pl.BlockSpec((tm, BRANCH_W), lambda i: (i, 0))
    in_specs = [pl.BlockSpec((tm, D_MODEL), lambda i: (i, 0)), row, row, row]
    for _, r in D_PAIRS:
        in_specs += [pl.BlockSpec((None, r, tm // r, BRANCH_W), lambda i: (i // ns, 0, i % ns, 0))] * 2
    in_specs += [pl.BlockSpec((tm, N_BRANCH * BRANCH_W), lambda i: (i, 0)),
                 pl.BlockSpec((tm, N_BRANCH * D_MODEL), lambda i: (i, 0)),
                 pl.BlockSpec((N_BRANCH, BRANCH_W, D_MODEL), lambda i: (0, 0, 0), pipeline_mode=pl.Buffered(1)),
                 pl.BlockSpec((D_MODEL, D_MODEL), lambda i: (0, 0), pipeline_mode=pl.Buffered(1))]
    return pl.pallas_call(
        functools.partial(_merge_kernel, tm=tm),
        grid=(n // tm,),
        in_specs=in_specs,
        out_specs=pl.BlockSpec((tm, D_MODEL), lambda i: (i, 0)),
        out_shape=jax.ShapeDtypeStruct((n, D_MODEL), F32),
        scratch_shapes=[pltpu.VMEM((BRANCH_W // LANES, tm, LANES), F32)] * 2,
        compiler_params=pltpu.CompilerParams(
            dimension_semantics=("parallel",), vmem_limit_bytes=VMEM_LIMIT),
    )(x, oa, ob, oc, *d_outs, gate, merge, wb, wo)


def kernel(x, w_in, w_branch, w_out, norm_gain, qk_gain, sink, lambda_vec, sub_norm_gain, rel_bias):
    bsz, seq, _ = x.shape
    n = bsz * seq
    depth = w_in.shape[0]
    rb = rel_bias.astype(F32)

    rows = seq // GRID_W
    row = jnp.repeat(jnp.arange(rows), GRID_W).astype(F32)
    col = jnp.tile(jnp.arange(GRID_W), rows).astype(F32)
    nf = HEAD_DIM // 4
    freqs = ROPE_THETA ** (-jnp.arange(nf, dtype=F32) / nf)
    ang = jnp.concatenate([row[:, None] * freqs, col[:, None] * freqs], axis=-1)
    cos2 = jnp.tile(jnp.cos(ang), (1, 4))
    sin2 = jnp.tile(jnp.concatenate([-jnp.sin(ang), jnp.sin(ang)], axis=-1), (1, 2))

    tq_a = 128
    bias_a = _band_bias(rb[:, :A_HEADS], 1, A_WIN, tq_a, tq_a + 2 * A_WIN, A_WIN, BAND_HPC)
    tq_b, tk_b = 1024, 512
    bias_b = _diff_bias(rb[:, A_HEADS:A_HEADS + B_HEADS], tq_b, tk_b)
    tq_d = 128
    bias_d = []
    for g, (win, r) in enumerate(D_PAIRS):
        lo = A_HEADS + B_HEADS + g * D_HEADS_PER_GROUP
        wd = min(tq_d + 128, seq // r)
        bias_d.append(_band_bias(rb[:, lo:lo + D_HEADS_PER_GROUP], r, win // (2 * r), tq_d, wd, 64, BAND_HPC))

    flag_np, rflag_np = _col_flags()
    flag = _permute_cols(jnp.asarray(flag_np))
    rflag = _permute_cols(jnp.asarray(rflag_np))
    blk = np.arange(NORM_W) // HEAD_DIM
    bd = jnp.asarray((blk[:, None] == blk[None, :]).astype(np.float32), dtype=BF16)

    xf = x.reshape(n, D_MODEL)
    for l in range(depth):
        h = _prenorm(xf, norm_gain[l].reshape(1, D_MODEL))
        w = _permute_cols(w_in[l].astype(BF16))
        gain = _permute_cols(_gain_cols(qk_gain[l]))
        norm_args = dict(gain=gain, flag=flag, bd=bd)
        pr = _proj_call(h, w, kind=K_ROPE, col0=P_ROPE, step=1, nj=2, seq=seq, rflag=rflag, cos=cos2, sin=sin2,
                        **norm_args).reshape(bsz, seq, -1)
        pn = _proj_call(h, w, kind=K_NORM, col0=P_NORM, step=1, nj=3, seq=seq, **norm_args).reshape(bsz, seq, -1)
        pbv = _proj_call(h, w, kind=K_RAW, col0=P_BV, step=1, nj=1, seq=seq).reshape(bsz, seq, -1)
        gate = _proj_call(h, w, kind=K_SILU, col0=OFF_GATE, step=1, nj=4, seq=seq)
        merge = _proj_call(h, w, kind=K_SIGMOID, col0=OFF_MERGE, step=1, nj=8, seq=seq)
        sink_rows = jnp.repeat(sink[l].astype(F32) * LOG2E, tq_a).reshape(A_KV, 1, BAND_HPC * tq_a)
        oa = _window_attention(pn, pr, bias_a, sink_rows, tq=tq_a, nsub=BAND_NSUB)
        lam_init = 0.8 - 0.6 * math.exp(-0.3 * l)
        ob = _diff_attention(pn, pbv, bias_b, lambda_vec[l], sub_norm_gain[l], lam_init, tq=tq_b, tk=tk_b)
        oc = _dense_gqa(pr)
        d_outs = []
        for g, (win, r) in enumerate(D_PAIRS):
            dqk = _proj_call(h, w, kind=K_NORM, col0=OFF_DQ + g * 512, step=(OFF_DK - OFF_DQ) // 512, nj=2,
                             seq=seq, r=r, **norm_args)
            dv = _proj_call(h, w, kind=K_RAW, col0=OFF_DV + g * 512, step=1, nj=1, seq=seq, r=r)
            if r == 1:
                dqk, dv = dqk.reshape(bsz, 1, seq, -1), dv.reshape(bsz, 1, seq, -1)
            d_outs += _dilated_group(dqk, dv, bias_d[g], tq=tq_d, nsub=BAND_NSUB)
        xf = _merge(xf, oa.reshape(n, -1), ob.reshape(n, -1), oc.reshape(n, -1), d_outs, gate, merge,
                    w_branch[l].astype(BF16), w_out[l].astype(BF16), seq)
    return xf.reshape(bsz, seq, D_MODEL)
```

```python
import functools
import math

import numpy as np
import jax
import jax.numpy as jnp
from jax import lax
from jax.experimental import pallas as pl
from jax.experimental.pallas import tpu as pltpu

F32 = jnp.float32
BF16 = jnp.bfloat16

D_MODEL = 1024
HEAD_DIM = 64
GRID_W = 64
EPS = 1e-6
A_HEADS, A_KV, A_WIN = 8, 2, 128
B_HEADS = 4
C_HEADS, C_KV = 8, 2
ROPE_THETA = 10000.0
D_PAIRS = ((128, 1), (512, 4), (2048, 16))
D_HEADS_PER_GROUP = 8
N_BRANCH = 4
BRANCH_W = 512
REL_BUCKETS = 32
REL_MAX_DIST = 1024
IN_WIDTH = 13824
NEG = -1e30
LOG2E = math.log2(math.e)
LANES = 128
ONES_ROWS = 16

OFF_AQ, OFF_AK, OFF_AV = 0, 512, 640
OFF_BQ, OFF_BK, OFF_BV = 768, 1280, 1792
OFF_CQ, OFF_CK, OFF_CV = 2304, 2816, 2944
OFF_DQ, OFF_DK, OFF_DV = 3072, 4608, 6144
OFF_GATE, OFF_MERGE = 7680, 9728
PERM = ((OFF_CQ, 768), (OFF_AK, 256), (OFF_AQ, 512), (OFF_BQ, 1024), (OFF_BV, 512))
P_ROPE, P_NORM, P_BV = 0, 1024, 2560

VMEM_LIMIT = 56 * 1024 * 1024

K_RAW, K_NORM, K_ROPE, K_SILU, K_SIGMOID = range(5)


def _permute_cols(a):
    return jnp.concatenate([a[..., lo:lo + n] for lo, n in PERM] + [a[..., OFF_DQ:]], axis=-1)


def _col_flags():
    norm = np.zeros((IN_WIDTH,), np.float32)
    for lo, hi in ((OFF_AQ, OFF_AV), (OFF_BQ, OFF_BV), (OFF_CQ, OFF_CV), (OFF_DQ, OFF_DV)):
        norm[lo:hi] = 1.0
    rope = np.zeros((IN_WIDTH,), np.float32)
    rope[OFF_CQ:OFF_CV] = 1.0
    return norm.reshape(1, IN_WIDTH), rope.reshape(1, IN_WIDTH)


def _gain_cols(g):
    sc = HEAD_DIM ** -0.5 * LOG2E
    one = lambda n: jnp.ones((n,), F32)
    parts = [jnp.tile(g[0, 0], A_HEADS) * sc, jnp.tile(g[0, 1], A_KV), one(128),
             jnp.tile(g[1, 0], 2 * B_HEADS) * sc, jnp.tile(g[1, 1], 2 * B_HEADS), one(512),
             jnp.tile(g[2, 0], C_HEADS) * sc, jnp.tile(g[2, 1], C_KV), one(128),
             jnp.tile(g[3, 0], 24) * sc, jnp.tile(g[3, 1], 24), one(1536),
             one(IN_WIDTH - OFF_GATE)]
    return jnp.concatenate(parts).reshape(1, IN_WIDTH)


def _prenorm_kernel(x_ref, g_ref, o_ref):
    x = x_ref[...]
    ms = jnp.mean(x * x, axis=-1, keepdims=True)
    o_ref[...] = (x * lax.rsqrt(ms + EPS) * g_ref[...]).astype(o_ref.dtype)


def _prenorm(x, g, tm=1024):
    n = x.shape[0]
    return pl.pallas_call(
        _prenorm_kernel,
        grid=(n // tm,),
        in_specs=[pl.BlockSpec((tm, D_MODEL), lambda i: (i, 0)),
                  pl.BlockSpec((1, D_MODEL), lambda i: (0, 0))],
        out_specs=pl.BlockSpec((tm, D_MODEL), lambda i: (i, 0)),
        out_shape=jax.ShapeDtypeStruct((n, D_MODEL), BF16),
        compiler_params=pltpu.CompilerParams(dimension_semantics=("parallel",), vmem_limit_bytes=VMEM_LIMIT),
    )(x, g)


PROJ_TM = 4096
PROJ_TN = 512
PROJ_CHUNK = 256
NORM_W = 256
DEINT_STRIDE = 4


def _proj_kernel(*refs, kind, r, tm, tn):
    refs = list(refs)
    h_ref, w_ref = refs[:2]
    pos = 2
    if kind in (K_NORM, K_ROPE):
        gain_ref, flag_ref, bd_ref = refs[pos:pos + 3]; pos += 3
    if kind == K_ROPE:
        rflag_ref, cos_ref, sin_ref = refs[pos:pos + 3]; pos += 3
    o_ref = refs[pos]; pos += 1
    slab_ref = refs[pos] if r > 1 else None
    slab2_ref = refs[pos + 1] if r > DEINT_STRIDE else None

    nchunk = tm // PROJ_CHUNK

    def main(rc):
        return jnp.dot(h_ref[rc * PROJ_CHUNK:(rc + 1) * PROJ_CHUNK, :], w_ref[...], preferred_element_type=F32)

    nxt = main(0)
    for rc in range(nchunk):
        rows = slice(rc * PROJ_CHUNK, (rc + 1) * PROJ_CHUNK)
        acc = nxt
        if rc + 1 < nchunk:
            nxt = main(rc + 1)
        if kind in (K_NORM, K_ROPE):
            sq = (acc * acc).astype(BF16)
            bd = bd_ref[...]
            ms = jnp.concatenate([jnp.dot(sq[:, c:c + NORM_W], bd, preferred_element_type=F32)
                                  for c in range(0, tn, NORM_W)], axis=1) * (1.0 / HEAD_DIM)
            y = acc * jnp.where(flag_ref[...] > 0.0, lax.rsqrt(ms + EPS), 1.0) * gain_ref[...]
            if kind == K_ROPE:
                c = jnp.concatenate([cos_ref[rows, :]] * (tn // LANES), axis=1)
                s = jnp.concatenate([sin_ref[rows, :]] * (tn // LANES), axis=1)
                lane = lax.broadcasted_iota(jnp.int32, y.shape, 1) & (HEAD_DIM - 1)
                half = HEAD_DIM // 2
                partner = jnp.where(lane < half, pltpu.roll(y, tn - half, axis=1), pltpu.roll(y, half, axis=1))
                y = jnp.where(rflag_ref[...] > 0.0, y * c + partner * s, y)
        elif kind == K_SILU:
            y = acc * (0.5 * jnp.tanh(0.5 * acc) + 0.5)
        elif kind == K_SIGMOID:
            y = 0.5 * jnp.tanh(0.5 * acc) + 0.5
        else:
            y = acc
        if r == 1:
            o_ref[rows, :] = y.astype(o_ref.dtype)
        else:
            for s_ in range(tn // LANES):
                slab_ref[s_, rows, :] = y[:, s_ * LANES:(s_ + 1) * LANES]
    if r > 1:
        r1 = min(r, DEINT_STRIDE)
        r2 = r // r1
        for s_ in range(tn // LANES):
            if r2 > 1:
                for c1 in range(r1):
                    slab2_ref[s_, c1 * (tm // r1):(c1 + 1) * (tm // r1), :] = (
                        slab_ref[s_, pl.ds(c1, tm // r1, stride=r1), :])
            for c1 in range(r1):
                for c2 in range(r2):
                    if r2 > 1:
                        rows_c = slab2_ref[s_, pl.ds(c1 * (tm // r1) + c2, tm // r, stride=r2), :]
                    else:
                        rows_c = slab_ref[s_, pl.ds(c1, tm // r, stride=r), :]
                    o_ref[c1 + r1 * c2, :, s_ * LANES:(s_ + 1) * LANES] = rows_c.astype(o_ref.dtype)


def _proj_call(h, w, *, kind, col0, step, nj, seq, r=1, gain=None, flag=None, rflag=None, bd=None, cos=None,
               sin=None):
    tm, tn = PROJ_TM, PROJ_TN
    n = h.shape[0]
    ns = seq // tm
    cb = col0 // tn
    col_spec = pl.BlockSpec((1, tn), lambda i, j: (0, cb + j * step))
    in_specs = [pl.BlockSpec((tm, D_MODEL), lambda i, j: (i, 0)),
                pl.BlockSpec((D_MODEL, tn), lambda i, j: (0, cb + j * step))]
    args = [h, w]
    if kind in (K_NORM, K_ROPE):
        in_specs += [col_spec, col_spec, pl.BlockSpec((NORM_W, NORM_W), lambda i, j: (0, 0))]
        args += [gain, flag, bd]
    if kind == K_ROPE:
        in_specs += [col_spec] + [pl.BlockSpec((tm, LANES), lambda i, j: (i % ns, 0))] * 2
        args += [rflag, cos, sin]
    scratch = []
    if r == 1:
        out_shape = jax.ShapeDtypeStruct((n, nj * tn), BF16)
        out_spec = pl.BlockSpec((tm, tn), lambda i, j: (i, j))
    else:
        out_shape = jax.ShapeDtypeStruct((n // seq, r, seq // r, nj * tn), BF16)
        out_spec = pl.BlockSpec((None, r, tm // r, tn), lambda i, j: (i // ns, 0, i % ns, j))
        scratch = [pltpu.VMEM((tn // LANES, tm, LANES), F32)] * (2 if r > DEINT_STRIDE else 1)
    return pl.pallas_call(
        functools.partial(_proj_kernel, kind=kind, r=r, tm=tm, tn=tn),
        grid=(n // tm, nj),
        in_specs=in_specs,
        out_specs=out_spec,
        out_shape=out_shape,
        scratch_shapes=scratch,
        compiler_params=pltpu.CompilerParams(
            dimension_semantics=("parallel", "parallel"), vmem_limit_bytes=VMEM_LIMIT),
    )(*args)


def _bucket(rel):
    nb = REL_BUCKETS // 2
    exact = nb // 2
    n = jnp.abs(rel)
    large = exact + (jnp.log(jnp.maximum(n, exact).astype(F32) / exact)
                     / math.log(REL_MAX_DIST / exact) * (nb - exact)).astype(jnp.int32)
    large = jnp.minimum(large, nb - 1)
    return jnp.where(rel > 0, nb, 0) + jnp.where(n < exact, n, large)


def _lookup(tbl, bucket):
    shape = (tbl.shape[1],) + (1,) * bucket.ndim
    out = jnp.zeros((tbl.shape[1],) + bucket.shape, F32)
    for j in range(REL_BUCKETS):
        out = jnp.where(bucket[None] == j, tbl[j].reshape(shape), out)
    return out


def _toeplitz_kernel(u_ref, o_ref, *, n_rows, n_cols):
    x = jnp.broadcast_to(u_ref[...], (n_rows, u_ref.shape[-1]))
    o_ref[...] = pltpu.roll(x, 0, 1, stride=1, stride_axis=0)[:, n_rows:n_rows + n_cols]


def _toeplitz_tiles(vec, starts, n_rows, n_cols, out_shape, out_index):
    assert n_rows % LANES == 0 and n_cols % LANES == 0
    nh = vec.shape[0]
    span = n_rows + n_cols - 1
    lp = _ceil_div(1 + span, LANES) * LANES
    rows = [jnp.pad(vec[:, s - (n_cols - 1):s + n_rows][:, ::-1], ((0, 0), (1, lp - 1 - span))) for s in starts]
    u = jnp.stack(rows, axis=1)[:, :, None, :]
    return pl.pallas_call(
        functools.partial(_toeplitz_kernel, n_rows=n_rows, n_cols=n_cols),
        grid=(nh, len(starts)),
        in_specs=[pl.BlockSpec((None, None, 1, lp), lambda h, t: (h, t, 0, 0))],
        out_specs=pl.BlockSpec((None, None, n_rows, n_cols), out_index),
        out_shape=jax.ShapeDtypeStruct(out_shape, F32),
        compiler_params=pltpu.CompilerParams(dimension_semantics=("parallel", "parallel")),
    )(u)


def _band_bias(tbl, stride, half, tq, w, pad, hpc):
    nh = tbl.shape[1]
    r0 = w + tq
    rel = jnp.arange(-r0, r0 + 1)
    vec = jnp.where((jnp.abs(rel) <= half)[None], _lookup(tbl, _bucket(rel * stride)) * LOG2E, NEG)
    starts = [r0 - off for off in (0, pad, w - tq)]
    return _toeplitz_tiles(vec, starts, w, tq, (3, nh // hpc, w, hpc * tq),
                           lambda h, t: (t, h // hpc, 0, h % hpc))


BAND_HPC = 4
BAND_NSUB = 8


def _band_kernel(*refs, qi_axis, nres, has_sink, lse_out, **cfg):
    refs = list(refs)
    q_ref, k_ref, v_ref, bias_ref = refs[:4]
    pos = 4
    sink_ref = lse_ref = None
    if has_sink:
        sink_ref = refs[pos]; pos += 1
    o_ref = refs[pos]; pos += 1
    if lse_out:
        lse_ref = refs[pos]
    qi = pl.program_id(qi_axis)
    if nres is None:
        _band_block(q_ref, k_ref, v_ref, bias_ref, sink_ref, o_ref, lse_ref, qi, **cfg)
    else:
        for rr in range(nres):
            _band_block(q_ref.at[rr], k_ref.at[rr], v_ref.at[rr], bias_ref, sink_ref, o_ref.at[rr],
                        lse_ref.at[rr] if lse_out else None, qi, **cfg)


def _band_block(q_ref, k_ref, v_ref, bias_ref, sink_ref, o_ref, lse_ref, qi, *, shared_kv, tq, nsub, w, pad, m_len):
    has_sink = sink_ref is not None
    lse_out = lse_ref is not None
    hpc = BAND_HPC
    kw = HEAD_DIM if shared_kv else hpc * HEAD_DIM
    last = m_len // tq - 1
    qt = q_ref[...].T
    ones = jnp.ones((ONES_ROWS, w), BF16)
    for j in range(nsub):
        qb = qi * nsub + j
        ks = pl.multiple_of(jnp.clip(qb * tq - pad, 0, m_len - w), 64)
        case = jnp.where(qb == 0, 0, jnp.where(qb == last, 2, 1))
        for c in range(q_ref.shape[1] // (hpc * HEAD_DIM)):
            slab = qt[c * hpc * HEAD_DIM:(c + 1) * hpc * HEAD_DIM, j * tq:(j + 1) * tq]
            if shared_kv:
                qc = jnp.concatenate([slab[g * HEAD_DIM:(g + 1) * HEAD_DIM] for g in range(hpc)], axis=1)
            else:
                head = lax.broadcasted_iota(jnp.int32, slab.shape, 0) // HEAD_DIM
                qc = jnp.concatenate([jnp.where(head == g, slab, jnp.zeros_like(slab)) for g in range(hpc)],
                                     axis=1)
            k = k_ref[pl.ds(ks, w), c * kw:(c + 1) * kw]
            v = v_ref[pl.ds(ks, w), c * kw:(c + 1) * kw]
            vt = jnp.concatenate([v.T, ones], axis=0)
            st = jnp.dot(k, qc, preferred_element_type=F32) + bias_ref[case, c]
            m = jnp.max(st, axis=0, keepdims=True)
            if has_sink:
                sk = sink_ref[c]
                m = jnp.maximum(m, sk)
            pt = jnp.exp2(st - m).astype(BF16)
            acc = jnp.dot(vt, pt, preferred_element_type=F32)
            l = acc[kw:kw + 1]
            if has_sink:
                l = l + jnp.exp2(sk - m)
            lse = m + jnp.log2(l)
            for g in range(0, hpc, 2):
                ots, lses = [], []
                for gg in (g, g + 1):
                    lanes = slice(gg * tq, (gg + 1) * tq)
                    rows = slice(0, HEAD_DIM) if shared_kv else slice(gg * HEAD_DIM, (gg + 1) * HEAD_DIM)
                    ots.append(acc[rows, lanes] / l[:, lanes])
                    lses.append(jnp.broadcast_to(lse[:, lanes], (HEAD_DIM, tq)))
                out_rows = slice(j * tq, (j + 1) * tq)
                cols = slice((c * hpc + g) * HEAD_DIM, (c * hpc + g + 2) * HEAD_DIM)
                o_ref[out_rows, cols] = jnp.concatenate(ots, axis=0).T.astype(o_ref.dtype)
                if lse_out:
                    lse_ref[out_rows, cols] = jnp.concatenate(lses, axis=0).T


def _window_attention(pn, pr, bias, sink_rows, tq, nsub):
    bsz, seq, _ = pn.shape
    pad = A_WIN
    w = tq + 2 * pad
    lanes = BAND_HPC * tq
    kern = functools.partial(_band_kernel, qi_axis=1, nres=None, has_sink=True, lse_out=False,
                             shared_kv=True, tq=tq, nsub=nsub, w=w, pad=pad, m_len=seq)
    return pl.pallas_call(
        kern,
        grid=(bsz, seq // (nsub * tq)),
        in_specs=[
            pl.BlockSpec((None, nsub * tq, 512), lambda b, qi: (b, qi, 0)),
            pl.BlockSpec((None, seq, 128), lambda b, qi: (b, 0, 6)),
            pl.BlockSpec((None, seq, 128), lambda b, qi: (b, 0, 7)),
            pl.BlockSpec((3, A_KV, w, lanes), lambda b, qi: (0, 0, 0, 0)),
            pl.BlockSpec((A_KV, 1, lanes), lambda b, qi: (0, 0, 0)),
        ],
        out_specs=pl.BlockSpec((None, nsub * tq, BRANCH_W), lambda b, qi: (b, qi, 0)),
        out_shape=jax.ShapeDtypeStruct((bsz, seq, BRANCH_W), BF16),
        compiler_params=pltpu.CompilerParams(
            dimension_semantics=("parallel", "arbitrary"), vmem_limit_bytes=VMEM_LIMIT),
    )(pn, pr, pr, bias, sink_rows)


def _dilated_group(qk, v, bias, tq, nsub):
    bsz, r, m_len, _ = v.shape
    pad = 64
    w = min(tq + 2 * pad, m_len)
    nsub = min(nsub, m_len // tq)
    nchain = D_HEADS_PER_GROUP // BAND_HPC
    nres = max(1, min(r, BAND_NSUB * tq // m_len))
    kern = functools.partial(_band_kernel, qi_axis=2, nres=nres, has_sink=False, lse_out=True,
                             shared_kv=False, tq=tq, nsub=nsub, w=w, pad=pad, m_len=m_len)
    row_spec = pl.BlockSpec((None, nres, nsub * tq, BRANCH_W), lambda b, c, qi: (b, c, qi, 0))
    return pl.pallas_call(
        kern,
        grid=(bsz, r // nres, m_len // (nsub * tq)),
        in_specs=[
            row_spec,
            pl.BlockSpec((None, nres, m_len, 512), lambda b, c, qi: (b, c, 0, 1)),
            pl.BlockSpec((None, nres, m_len, 512), lambda b, c, qi: (b, c, 0, 0)),
            pl.BlockSpec((3, nchain, w, BAND_HPC * tq), lambda b, c, qi: (0, 0, 0, 0)),
        ],
        out_specs=[row_spec, row_spec],
        out_shape=[jax.ShapeDtypeStruct((bsz, r, m_len, BRANCH_W), BF16),
                   jax.ShapeDtypeStruct((bsz, r, m_len, BRANCH_W), F32)],
        compiler_params=pltpu.CompilerParams(
            dimension_semantics=("parallel", "parallel", "arbitrary"), vmem_limit_bytes=VMEM_LIMIT),
    )(qk, qk, v, bias)


DIFF_UNIT = 512


def _saturation_distance():
    nb = REL_BUCKETS // 2
    exact = nb // 2
    n = np.arange(exact, 4 * REL_MAX_DIST)
    large = exact + (np.log(n.astype(np.float32) / exact) / math.log(REL_MAX_DIST / exact)
                     * (nb - exact)).astype(np.int32)
    return int(n[large < nb - 1].max()) + 1 + 2


def _ceil_div(a, b):
    return -(-a // b)


def _diff_tiles(tq, tk):
    far = _saturation_distance()
    return -_ceil_div(far + tk - 1, DIFF_UNIT), _ceil_div(far + tq - 1, DIFF_UNIT)


def _diff_kernel(lv_ref, sg_ref, q_ref, k_ref, v_ref, bias_ref, o_ref, st_ref, *, tq, tk, nk, lo, hi, lam_init):
    qi = pl.program_id(2)
    vd = 2 * HEAD_DIM
    qt = q_ref[...].T
    first = lax.broadcasted_iota(jnp.int32, (vd, tq), 0) < HEAD_DIM
    zero = jnp.zeros((vd, tq), BF16)
    qbd = jnp.concatenate([jnp.where(first, qt, zero), jnp.where(first, zero, qt)], axis=1)
    m = jnp.full((1, 2 * tq), NEG, F32)
    acc = jnp.zeros((vd + ONES_ROWS, 2 * tq), F32)
    ones = jnp.ones((ONES_ROWS, tk), BF16)

    def scores(t):
        dd = jnp.clip(t * (tk // DIFF_UNIT) - qi * (tq // DIFF_UNIT), lo, hi) - lo
        b = bias_ref[dd]
        st = jnp.dot(k_ref[t * tk:(t + 1) * tk, :], qbd, preferred_element_type=F32)
        return jnp.concatenate([st[:, :tq] + b, st[:, tq:] + b], axis=1)

    st_ref[0] = scores(0)
    for t in range(nk):
        if t + 1 < nk:
            st_ref[(t + 1) % 2] = scores(t + 1)
        st = st_ref[t % 2]
        vt = jnp.concatenate([v_ref[t * tk:(t + 1) * tk, :].T, ones], axis=0)
        m_new = jnp.maximum(m, jnp.max(st, axis=0, keepdims=True))
        pt = jnp.exp2(st - m_new).astype(BF16)
        acc = jnp.exp2(m - m_new) * acc + jnp.dot(vt, pt, preferred_element_type=F32)
        m = m_new

    lv = lv_ref[...]
    lam = (jnp.exp(jnp.sum(lv[0:1] * lv[1:2], axis=-1, keepdims=True))
           - jnp.exp(jnp.sum(lv[2:3] * lv[3:4], axis=-1, keepdims=True)) + lam_init)
    ot = acc[:vd, :tq] / acc[vd:vd + 1, :tq] - lam * (acc[:vd, tq:] / acc[vd:vd + 1, tq:])
    ms = jnp.mean(ot * ot, axis=0, keepdims=True)
    ot = ot * lax.rsqrt(ms + EPS) * sg_ref[...] * (1.0 - lam_init)
    o_ref[...] = ot.T.astype(o_ref.dtype)


def _diff_bias(tbl, tq, tk):
    lo, hi = _diff_tiles(tq, tk)
    r0 = max(-lo, hi) * DIFF_UNIT + tk + tq
    vec = _lookup(tbl, _bucket(jnp.arange(-r0, r0 + 1))) * LOG2E
    starts = [r0 + d * DIFF_UNIT for d in range(lo, hi + 1)]
    return _toeplitz_tiles(vec, starts, tk, tq, (tbl.shape[1], len(starts), tk, tq), lambda h, t: (h, t, 0, 0))


def _diff_attention(pn, pbv, bias, lam_vec, sub_gain, lam_init, tq, tk):
    bsz, seq, _ = pn.shape
    lo, hi = _diff_tiles(tq, tk)
    nt = hi - lo + 1
    vd = 2 * HEAD_DIM
    kern = functools.partial(_diff_kernel, tq=tq, tk=tk, nk=seq // tk, lo=lo, hi=hi, lam_init=lam_init)
    return pl.pallas_call(
        kern,
        grid=(B_HEADS, bsz, seq // tq),
        in_specs=[
            pl.BlockSpec((4, HEAD_DIM), lambda h, b, qi: (0, 0)),
            pl.BlockSpec((vd, 1), lambda h, b, qi: (0, 0)),
            pl.BlockSpec((None, tq, vd), lambda h, b, qi: (b, qi, 512 // vd + h)),
            pl.BlockSpec((None, seq, vd), lambda h, b, qi: (b, 0, 1024 // vd + h)),
            pl.BlockSpec((None, seq, vd), lambda h, b, qi: (b, 0, h)),
            pl.BlockSpec((None, nt, tk, tq), lambda h, b, qi: (h, 0, 0, 0), pipeline_mode=pl.Buffered(1)),
        ],
        out_specs=pl.BlockSpec((None, tq, vd), lambda h, b, qi: (b, qi, h)),
        out_shape=jax.ShapeDtypeStruct((bsz, seq, BRANCH_W), BF16),
        scratch_shapes=[pltpu.VMEM((2, tk, 2 * tq), F32)],
        compiler_params=pltpu.CompilerParams(
            dimension_semantics=("parallel", "parallel", "arbitrary"), vmem_limit_bytes=VMEM_LIMIT),
    )(lam_vec, sub_gain.reshape(vd, 1), pn, pn, pbv, bias)


def _gqa_kernel(q0_ref, q1_ref, k_ref, v_ref, o_ref, st_ref, *, tq, tk, nk):
    grp = C_HEADS // C_KV
    qts = []
    for q_ref in (q0_ref, q1_ref):
        qt = q_ref[...].T
        qts.append(jnp.concatenate([qt[g * HEAD_DIM:(g + 1) * HEAD_DIM] for g in range(grp)], axis=1))
    ones = jnp.ones((ONES_ROWS, tk), BF16)
    state = [(jnp.full((1, grp * tq), NEG, F32), jnp.zeros((HEAD_DIM + ONES_ROWS, grp * tq), F32))
             for _ in range(C_KV)]

    def scores(t, slot):
        for kv in range(C_KV):
            k = k_ref[t * tk:(t + 1) * tk, kv * HEAD_DIM:(kv + 1) * HEAD_DIM]
            st_ref[slot, kv] = jnp.dot(k, qts[kv], preferred_element_type=F32)

    scores(0, 0)
    for t in range(nk):
        if t + 1 < nk:
            scores(t + 1, (t + 1) % 2)
        for kv in range(C_KV):
            m, acc = state[kv]
            st = st_ref[t % 2, kv]
            v = v_ref[t * tk:(t + 1) * tk, kv * HEAD_DIM:(kv + 1) * HEAD_DIM]
            vt = jnp.concatenate([v.T, ones], axis=0)
            m_new = jnp.maximum(m, jnp.max(st, axis=0, keepdims=True))
            pt = jnp.exp2(st - m_new).astype(BF16)
            acc = jnp.exp2(m - m_new) * acc + jnp.dot(vt, pt, preferred_element_type=F32)
            state[kv] = (m_new, acc)

    for kv in range(C_KV):
        acc = state[kv][1]
        ot = acc[:HEAD_DIM] / acc[HEAD_DIM:HEAD_DIM + 1]
        for g in range(grp):
            h = kv * grp + g
            o_ref[:, h * HEAD_DIM:(h + 1) * HEAD_DIM] = ot[:, g * tq:(g + 1) * tq].T.astype(o_ref.dtype)


def _dense_gqa(pr, tq=512, tk=512):
    bsz, seq, _ = pr.shape
    kern = functools.partial(_gqa_kernel, tq=tq, tk=tk, nk=seq // tk)
    return pl.pallas_call(
        kern,
        grid=(bsz, seq // tq),
        in_specs=[
            pl.BlockSpec((None, tq, 256), lambda b, qi: (b, qi, 0)),
            pl.BlockSpec((None, tq, 256), lambda b, qi: (b, qi, 1)),
            pl.BlockSpec((None, seq, 128), lambda b, qi: (b, 0, 4)),
            pl.BlockSpec((None, seq, 128), lambda b, qi: (b, 0, 5)),
        ],
        out_specs=pl.BlockSpec((None, tq, BRANCH_W), lambda b, qi: (b, qi, 0)),
        out_shape=jax.ShapeDtypeStruct((bsz, seq, BRANCH_W), BF16),
        scratch_shapes=[pltpu.VMEM((2, C_KV, tk, (C_HEADS // C_KV) * tq), F32)],
        compiler_params=pltpu.CompilerParams(
            dimension_semantics=("parallel", "arbitrary"), vmem_limit_bytes=VMEM_LIMIT),
    )(pr, pr, pr, pr)


def _merge_kernel(*refs, tm):
    x_ref, oa_ref, ob_ref, oc_ref = refs[:4]
    d_refs = refs[4:10]
    gate_ref, merge_ref, wb_ref, wo_ref, out_ref, so_ref, sl_ref = refs[10:17]
    nslab = BRANCH_W // LANES

    def natural(ref, scr, r):
        if r == 1:
            return ref[0].astype(F32)
        for c in range(r):
            blk = ref[c].astype(F32)
            for s in range(nslab):
                scr[s, pl.ds(c, tm // r, stride=r), :] = blk[:, s * LANES:(s + 1) * LANES]
        return jnp.concatenate([scr[s] for s in range(nslab)], axis=1)

    def gate(n):
        return gate_ref[:, n * BRANCH_W:(n + 1) * BRANCH_W]

    def term(n, gated, half):
        y = jnp.dot(gated, wb_ref[n, :, half * 512:(half + 1) * 512], preferred_element_type=F32)
        return merge_ref[:, n * D_MODEL + half * 512:n * D_MODEL + (half + 1) * 512].astype(F32) * y

    partial = [None, None]
    for n, o_ref in enumerate((oa_ref, ob_ref, oc_ref)):
        gated = o_ref[...] * gate(n)
        for half in range(2):
            t = term(n, gated, half)
            partial[half] = t if partial[half] is None else partial[half] + t

    os_, ls_ = [], []
    for g, (_, r) in enumerate(D_PAIRS):
        os_.append(natural(d_refs[2 * g], so_ref, r))
        ls_.append(natural(d_refs[2 * g + 1], sl_ref, r))
    mx = jnp.maximum(jnp.maximum(ls_[0], ls_[1]), ls_[2])
    es = [jnp.exp2(l - mx) for l in ls_]
    od = (es[0] * os_[0] + es[1] * os_[1] + es[2] * os_[2]) / (es[0] + es[1] + es[2])
    gated_d = (od * gate(N_BRANCH - 1).astype(F32)).astype(BF16)
    merged = jnp.concatenate([(partial[half] + term(N_BRANCH - 1, gated_d, half)).astype(BF16)
                              for half in range(2)], axis=1)
    out_ref[...] = x_ref[...] + jnp.dot(merged, wo_ref[...], preferred_element_type=F32)


def _merge(x, oa, ob, oc, d_outs, gate, merge, wb, wo, seq, tm=512):
    n = x.shape[0]
    ns = seq // tm
    row = pl.BlockSpec((tm, BRANCH_W), lambda i: (i, 0))
    in_specs = [pl.BlockSpec((tm, D_MODEL), lambda i: (i, 0)), row, row, row]
    for _, r in D_PAIRS:
        in_specs += [pl.BlockSpec((None, r, tm // r, BRANCH_W), lambda i: (i // ns, 0, i % ns, 0))] * 2
    in_specs += [pl.BlockSpec((tm, N_BRANCH * BRANCH_W), lambda i: (i, 0)),
                 pl.BlockSpec((tm, N_BRANCH * D_MODEL), lambda i: (i, 0)),
                 pl.BlockSpec((N_BRANCH, BRANCH_W, D_MODEL), lambda i: (0, 0, 0), pipeline_mode=pl.Buffered(1)),
                 pl.BlockSpec((D_MODEL, D_MODEL), lambda i: (0, 0), pipeline_mode=pl.Buffered(1))]
    return pl.pallas_call(
        functools.partial(_merge_kernel, tm=tm),
        grid=(n // tm,),
        in_specs=in_specs,
        out_specs=pl.BlockSpec((tm, D_MODEL), lambda i: (i, 0)),
        out_shape=jax.ShapeDtypeStruct((n, D_MODEL), F32),
        scratch_shapes=[pltpu.VMEM((BRANCH_W // LANES, tm, LANES), F32)] * 2,
        compiler_params=pltpu.CompilerParams(
            dimension_semantics=("parallel",), vmem_limit_bytes=VMEM_LIMIT),
    )(x, oa, ob, oc, *d_outs, gate, merge, wb, wo)


def kernel(x, w_in, w_branch, w_out, norm_gain, qk_gain, sink, lambda_vec, sub_norm_gain, rel_bias):
    bsz, seq, _ = x.shape
    n = bsz * seq
    depth = w_in.shape[0]
    rb = rel_bias.astype(F32)

    rows = seq // GRID_W
    row = jnp.repeat(jnp.arange(rows), GRID_W).astype(F32)
    col = jnp.tile(jnp.arange(GRID_W), rows).astype(F32)
    nf = HEAD_DIM // 4
    freqs = ROPE_THETA ** (-jnp.arange(nf, dtype=F32) / nf)
    ang = jnp.concatenate([row[:, None] * freqs, col[:, None] * freqs], axis=-1)
    cos2 = jnp.tile(jnp.cos(ang), (1, 4))
    sin2 = jnp.tile(jnp.concatenate([-jnp.sin(ang), jnp.sin(ang)], axis=-1), (1, 2))

    tq_a = 128
    bias_a = _band_bias(rb[:, :A_HEADS], 1, A_WIN, tq_a, tq_a + 2 * A_WIN, A_WIN, BAND_HPC)
    tq_b, tk_b = 1024, 512
    bias_b = _diff_bias(rb[:, A_HEADS:A_HEADS + B_HEADS], tq_b, tk_b)
    tq_d = 128
    bias_d = []
    for g, (win, r) in enumerate(D_PAIRS):
        lo = A_HEADS + B_HEADS + g * D_HEADS_PER_GROUP
        wd = min(tq_d + 128, seq // r)
        bias_d.append(_band_bias(rb[:, lo:lo + D_HEADS_PER_GROUP], r, win // (2 * r), tq_d, wd, 64, BAND_HPC))

    flag_np, rflag_np = _col_flags()
    flag = _permute_cols(jnp.asarray(flag_np))
    rflag = _permute_cols(jnp.asarray(rflag_np))
    blk = np.arange(NORM_W) // HEAD_DIM
    bd = jnp.asarray((blk[:, None] == blk[None, :]).astype(np.float32), dtype=BF16)

    xf = x.reshape(n, D_MODEL)
    for l in range(depth):
        h = _prenorm(xf, norm_gain[l].reshape(1, D_MODEL))
        w = _permute_cols(w_in[l].astype(BF16))
        gain = _permute_cols(_gain_cols(qk_gain[l]))
        norm_args = dict(gain=gain, flag=flag, bd=bd)
        pr = _proj_call(h, w, kind=K_ROPE, col0=P_ROPE, step=1, nj=2, seq=seq, rflag=rflag, cos=cos2, sin=sin2,
                        **norm_args).reshape(bsz, seq, -1)
        pn = _proj_call(h, w, kind=K_NORM, col0=P_NORM, step=1, nj=3, seq=seq, **norm_args).reshape(bsz, seq, -1)
        pbv = _proj_call(h, w, kind=K_RAW, col0=P_BV, step=1, nj=1, seq=seq).reshape(bsz, seq, -1)
        gate = _proj_call(h, w, kind=K_SILU, col0=OFF_GATE, step=1, nj=4, seq=seq)
        merge = _proj_call(h, w, kind=K_SIGMOID, col0=OFF_MERGE, step=1, nj=8, seq=seq)
        sink_rows = jnp.repeat(sink[l].astype(F32) * LOG2E, tq_a).reshape(A_KV, 1, BAND_HPC * tq_a)
        oa = _window_attention(pn, pr, bias_a, sink_rows, tq=tq_a, nsub=BAND_NSUB)
        lam_init = 0.8 - 0.6 * math.exp(-0.3 * l)
        ob = _diff_attention(pn, pbv, bias_b, lambda_vec[l], sub_norm_gain[l], lam_init, tq=tq_b, tk=tk_b)
        oc = _dense_gqa(pr)
        d_outs = []
        for g, (win, r) in enumerate(D_PAIRS):
            dqk = _proj_call(h, w, kind=K_NORM, col0=OFF_DQ + g * 512, step=(OFF_DK - OFF_DQ) // 512, nj=2,
                             seq=seq, r=r, **norm_args)
            dv = _proj_call(h, w, kind=K_RAW, col0=OFF_DV + g * 512, step=1, nj=1, seq=seq, r=r)
            if r == 1:
                dqk, dv = dqk.reshape(bsz, 1, seq, -1), dv.reshape(bsz, 1, seq, -1)
            d_outs += _dilated_group(dqk, dv, bias_d[g], tq=tq_d, nsub=BAND_NSUB)
        xf = _merge(xf, oa.reshape(n, -1), ob.reshape(n, -1), oc.reshape(n, -1), d_outs, gate, merge,
                    w_branch[l].astype(BF16), w_out[l].astype(BF16), seq)
    return xf.reshape(bsz, seq, D_MODEL)
```

```python
import functools
import math

import numpy as np
import jax
import jax.numpy as jnp
from jax import lax
from jax.experimental import pallas as pl
from jax.experimental.pallas import tpu as pltpu

F32 = jnp.float32
BF16 = jnp.bfloat16

D_MODEL = 1024
HEAD_DIM = 64
GRID_W = 64
EPS = 1e-6
A_HEADS, A_KV, A_WIN = 8, 2, 128
B_HEADS = 4
C_HEADS, C_KV = 8, 2
ROPE_THETA = 10000.0
D_PAIRS = ((128, 1), (512, 4), (2048, 16))
D_HEADS_PER_GROUP = 8
N_BRANCH = 4
BRANCH_W = 512
REL_BUCKETS = 32
REL_MAX_DIST = 1024
IN_WIDTH = 13824
NEG = -1e30
LOG2E = math.log2(math.e)
LANES = 128
ONES_ROWS = 16

OFF_AQ, OFF_AK, OFF_AV = 0, 512, 640
OFF_BQ, OFF_BK, OFF_BV = 768, 1280, 1792
OFF_CQ, OFF_CK, OFF_CV = 2304, 2816, 2944
OFF_DQ, OFF_DK, OFF_DV = 3072, 4608, 6144
OFF_GATE, OFF_MERGE = 7680, 9728
PERM = ((OFF_CQ, 768), (OFF_AK, 256), (OFF_AQ, 512), (OFF_BQ, 1024), (OFF_BV, 512))
P_ROPE, P_NORM, P_BV = 0, 1024, 2560

VMEM_LIMIT = 56 * 1024 * 1024

K_RAW, K_NORM, K_ROPE, K_SILU, K_SIGMOID = range(5)


def _permute_cols(a):
    return jnp.concatenate([a[..., lo:lo + n] for lo, n in PERM] + [a[..., OFF_DQ:]], axis=-1)


def _col_flags():
    norm = np.zeros((IN_WIDTH,), np.float32)
    for lo, hi in ((OFF_AQ, OFF_AV), (OFF_BQ, OFF_BV), (OFF_CQ, OFF_CV), (OFF_DQ, OFF_DV)):
        norm[lo:hi] = 1.0
    rope = np.zeros((IN_WIDTH,), np.float32)
    rope[OFF_CQ:OFF_CV] = 1.0
    return norm.reshape(1, IN_WIDTH), rope.reshape(1, IN_WIDTH)


def _gain_cols(g):
    sc = HEAD_DIM ** -0.5 * LOG2E
    one = lambda n: jnp.ones((n,), F32)
    parts = [jnp.tile(g[0, 0], A_HEADS) * sc, jnp.tile(g[0, 1], A_KV), one(128),
             jnp.tile(g[1, 0], 2 * B_HEADS) * sc, jnp.tile(g[1, 1], 2 * B_HEADS), one(512),
             jnp.tile(g[2, 0], C_HEADS) * sc, jnp.tile(g[2, 1], C_KV), one(128),
             jnp.tile(g[3, 0], 24) * sc, jnp.tile(g[3, 1], 24), one(1536),
             one(IN_WIDTH - OFF_GATE)]
    return jnp.concatenate(parts).reshape(1, IN_WIDTH)


def _prenorm_kernel(x_ref, g_ref, o_ref):
    x = x_ref[...]
    ms = jnp.mean(x * x, axis=-1, keepdims=True)
    o_ref[...] = (x * lax.rsqrt(ms + EPS) * g_ref[...]).astype(o_ref.dtype)


def _prenorm(x, g, tm=1024):
    n = x.shape[0]
    return pl.pallas_call(
        _prenorm_kernel,
        grid=(n // tm,),
        in_specs=[pl.BlockSpec((tm, D_MODEL), lambda i: (i, 0)),
                  pl.BlockSpec((1, D_MODEL), lambda i: (0, 0))],
        out_specs=pl.BlockSpec((tm, D_MODEL), lambda i: (i, 0)),
        out_shape=jax.ShapeDtypeStruct((n, D_MODEL), BF16),
        compiler_params=pltpu.CompilerParams(dimension_semantics=("parallel",), vmem_limit_bytes=VMEM_LIMIT),
    )(x, g)


PROJ_TM = 4096
PROJ_TN = 512
PROJ_CHUNK = 256
NORM_W = 256
DEINT_STRIDE = 4


def _proj_kernel(*refs, kind, r, tm, tn):
    refs = list(refs)
    h_ref, w_ref = refs[:2]
    pos = 2
    if kind in (K_NORM, K_ROPE):
        gain_ref, flag_ref, bd_ref = refs[pos:pos + 3]; pos += 3
    if kind == K_ROPE:
        rflag_ref, cos_ref, sin_ref = refs[pos:pos + 3]; pos += 3
    o_ref = refs[pos]; pos += 1
    slab_ref = refs[pos] if r > 1 else None
    slab2_ref = refs[pos + 1] if r > DEINT_STRIDE else None

    nchunk = tm // PROJ_CHUNK

    def main(rc):
        return jnp.dot(h_ref[rc * PROJ_CHUNK:(rc + 1) * PROJ_CHUNK, :], w_ref[...], preferred_element_type=F32)

    nxt = main(0)
    for rc in range(nchunk):
        rows = slice(rc * PROJ_CHUNK, (rc + 1) * PROJ_CHUNK)
        acc = nxt
        if rc + 1 < nchunk:
            nxt = main(rc + 1)
        if kind in (K_NORM, K_ROPE):
            sq = (acc * acc).astype(BF16)
            bd = bd_ref[...]
            ms = jnp.concatenate([jnp.dot(sq[:, c:c + NORM_W], bd, preferred_element_type=F32)
                                  for c in range(0, tn, NORM_W)], axis=1) * (1.0 / HEAD_DIM)
            y = acc * jnp.where(flag_ref[...] > 0.0, lax.rsqrt(ms + EPS), 1.0) * gain_ref[...]
            if kind == K_ROPE:
                c = jnp.concatenate([cos_ref[rows, :]] * (tn // LANES), axis=1)
                s = jnp.concatenate([sin_ref[rows, :]] * (tn // LANES), axis=1)
                lane = lax.broadcasted_iota(jnp.int32, y.shape, 1) & (HEAD_DIM - 1)
                half = HEAD_DIM // 2
                partner = jnp.where(lane < half, pltpu.roll(y, tn - half, axis=1), pltpu.roll(y, half, axis=1))
                y = jnp.where(rflag_ref[...] > 0.0, y * c + partner * s, y)
        elif kind == K_SILU:
            y = acc * (0.5 * jnp.tanh(0.5 * acc) + 0.5)
        elif kind == K_SIGMOID:
            y = 0.5 * jnp.tanh(0.5 * acc) + 0.5
        else:
            y = acc
        if r == 1:
            o_ref[rows, :] = y.astype(o_ref.dtype)
        else:
            for s_ in range(tn // LANES):
                slab_ref[s_, rows, :] = y[:, s_ * LANES:(s_ + 1) * LANES]
    if r > 1:
        r1 = min(r, DEINT_STRIDE)
        r2 = r // r1
        for s_ in range(tn // LANES):
            if r2 > 1:
                for c1 in range(r1):
                    slab2_ref[s_, c1 * (tm // r1):(c1 + 1) * (tm // r1), :] = (
                        slab_ref[s_, pl.ds(c1, tm // r1, stride=r1), :])
            for c1 in range(r1):
                for c2 in range(r2):
                    if r2 > 1:
                        rows_c = slab2_ref[s_, pl.ds(c1 * (tm // r1) + c2, tm // r, stride=r2), :]
                    else:
                        rows_c = slab_ref[s_, pl.ds(c1, tm // r, stride=r), :]
                    o_ref[c1 + r1 * c2, :, s_ * LANES:(s_ + 1) * LANES] = rows_c.astype(o_ref.dtype)


def _proj_call(h, w, *, kind, col0, step, nj, seq, r=1, gain=None, flag=None, rflag=None, bd=None, cos=None,
               sin=None):
    tm, tn = PROJ_TM, PROJ_TN
    n = h.shape[0]
    ns = seq // tm
    cb = col0 // tn
    col_spec = pl.BlockSpec((1, tn), lambda i, j: (0, cb + j * step))
    in_specs = [pl.BlockSpec((tm, D_MODEL), lambda i, j: (i, 0)),
                pl.BlockSpec((D_MODEL, tn), lambda i, j: (0, cb + j * step))]
    args = [h, w]
    if kind in (K_NORM, K_ROPE):
        in_specs += [col_spec, col_spec, pl.BlockSpec((NORM_W, NORM_W), lambda i, j: (0, 0))]
        args += [gain, flag, bd]
    if kind == K_ROPE:
        in_specs += [col_spec] + [pl.BlockSpec((tm, LANES), lambda i, j: (i % ns, 0))] * 2
        args += [rflag, cos, sin]
    scratch = []
    if r == 1:
        out_shape = jax.ShapeDtypeStruct((n, nj * tn), BF16)
        out_spec = pl.BlockSpec((tm, tn), lambda i, j: (i, j))
    else:
        out_shape = jax.ShapeDtypeStruct((n // seq, r, seq // r, nj * tn), BF16)
        out_spec = pl.BlockSpec((None, r, tm // r, tn), lambda i, j: (i // ns, 0, i % ns, j))
        scratch = [pltpu.VMEM((tn // LANES, tm, LANES), F32)] * (2 if r > DEINT_STRIDE else 1)
    return pl.pallas_call(
        functools.partial(_proj_kernel, kind=kind, r=r, tm=tm, tn=tn),
        grid=(n // tm, nj),
        in_specs=in_specs,
        out_specs=out_spec,
        out_shape=out_shape,
        scratch_shapes=scratch,
        compiler_params=pltpu.CompilerParams(
            dimension_semantics=("parallel", "parallel"), vmem_limit_bytes=VMEM_LIMIT),
    )(*args)


def _bucket(rel):
    nb = REL_BUCKETS // 2
    exact = nb // 2
    n = jnp.abs(rel)
    large = exact + (jnp.log(jnp.maximum(n, exact).astype(F32) / exact)
                     / math.log(REL_MAX_DIST / exact) * (nb - exact)).astype(jnp.int32)
    large = jnp.minimum(large, nb - 1)
    return jnp.where(rel > 0, nb, 0) + jnp.where(n < exact, n, large)


def _lookup(tbl, bucket):
    shape = (tbl.shape[1],) + (1,) * bucket.ndim
    out = jnp.zeros((tbl.shape[1],) + bucket.shape, F32)
    for j in range(REL_BUCKETS):
        out = jnp.where(bucket[None] == j, tbl[j].reshape(shape), out)
    return out


def _toeplitz_kernel(u_ref, o_ref, *, n_rows, n_cols):
    x = jnp.broadcast_to(u_ref[...], (n_rows, u_ref.shape[-1]))
    o_ref[...] = pltpu.roll(x, 0, 1, stride=1, stride_axis=0)[:, n_rows:n_rows + n_cols]


def _toeplitz_tiles(vec, starts, n_rows, n_cols, out_shape, out_index):
    assert n_rows % LANES == 0 and n_cols % LANES == 0
    nh = vec.shape[0]
    span = n_rows + n_cols - 1
    lp = _ceil_div(1 + span, LANES) * LANES
    rows = [jnp.pad(vec[:, s - (n_cols - 1):s + n_rows][:, ::-1], ((0, 0), (1, lp - 1 - span))) for s in starts]
    u = jnp.stack(rows, axis=1)[:, :, None, :]
    return pl.pallas_call(
        functools.partial(_toeplitz_kernel, n_rows=n_rows, n_cols=n_cols),
        grid=(nh, len(starts)),
        in_specs=[pl.BlockSpec((None, None, 1, lp), lambda h, t: (h, t, 0, 0))],
        out_specs=pl.BlockSpec((None, None, n_rows, n_cols), out_index),
        out_shape=jax.ShapeDtypeStruct(out_shape, F32),
        compiler_params=pltpu.CompilerParams(dimension_semantics=("parallel", "parallel")),
    )(u)


def _band_bias(tbl, stride, half, tq, w, pad, hpc):
    nh = tbl.shape[1]
    r0 = w + tq
    rel = jnp.arange(-r0, r0 + 1)
    vec = jnp.where((jnp.abs(rel) <= half)[None], _lookup(tbl, _bucket(rel * stride)) * LOG2E, NEG)
    starts = [r0 - off for off in (0, pad, w - tq)]
    return _toeplitz_tiles(vec, starts, w, tq, (3, nh // hpc, w, hpc * tq),
                           lambda h, t: (t, h // hpc, 0, h % hpc))


BAND_HPC = 4
BAND_NSUB = 16


def _band_kernel(*refs, qi_axis, nres, has_sink, lse_out, **cfg):
    refs = list(refs)
    q_ref, k_ref, v_ref, bias_ref = refs[:4]
    pos = 4
    sink_ref = lse_ref = None
    if has_sink:
        sink_ref = refs[pos]; pos += 1
    o_ref = refs[pos]; pos += 1
    if lse_out:
        lse_ref = refs[pos]
    qi = pl.program_id(qi_axis)
    if nres is None:
        _band_block(q_ref, k_ref, v_ref, bias_ref, sink_ref, o_ref, lse_ref, qi, **cfg)
    else:
        for rr in range(nres):
            _band_block(q_ref.at[rr], k_ref.at[rr], v_ref.at[rr], bias_ref, sink_ref, o_ref.at[rr],
                        lse_ref.at[rr] if lse_out else None, qi, **cfg)


def _band_block(q_ref, k_ref, v_ref, bias_ref, sink_ref, o_ref, lse_ref, qi, *, shared_kv, tq, nsub, w, pad, m_len):
    has_sink = sink_ref is not None
    lse_out = lse_ref is not None
    hpc = BAND_HPC
    kw = HEAD_DIM if shared_kv else hpc * HEAD_DIM
    last = m_len // tq - 1
    qt = q_ref[...].T
    ones = jnp.ones((ONES_ROWS, w), BF16)
    for j in range(nsub):
        qb = qi * nsub + j
        ks = pl.multiple_of(jnp.clip(qb * tq - pad, 0, m_len - w), 64)
        case = jnp.where(qb == 0, 0, jnp.where(qb == last, 2, 1))
        for c in range(q_ref.shape[1] // (hpc * HEAD_DIM)):
            slab = qt[c * hpc * HEAD_DIM:(c + 1) * hpc * HEAD_DIM, j * tq:(j + 1) * tq]
            if shared_kv:
                qc = jnp.concatenate([slab[g * HEAD_DIM:(g + 1) * HEAD_DIM] for g in range(hpc)], axis=1)
            else:
                head = lax.broadcasted_iota(jnp.int32, slab.shape, 0) // HEAD_DIM
                qc = jnp.concatenate([jnp.where(head == g, slab, jnp.zeros_like(slab)) for g in range(hpc)],
                                     axis=1)
            k = k_ref[pl.ds(ks, w), c * kw:(c + 1) * kw]
            v = v_ref[pl.ds(ks, w), c * kw:(c + 1) * kw]
            vt = jnp.concatenate([v.T, ones], axis=0)
            st = jnp.dot(k, qc, preferred_element_type=F32) + bias_ref[case, c]
            m = jnp.max(st, axis=0, keepdims=True)
            if has_sink:
                sk = sink_ref[c]
                m = jnp.maximum(m, sk)
            pt = jnp.exp2(st - m).astype(BF16)
            acc = jnp.dot(vt, pt, preferred_element_type=F32)
            l = acc[kw:kw + 1]
            if has_sink:
                l = l + jnp.exp2(sk - m)
            lse = m + jnp.log2(l)
            for g in range(0, hpc, 2):
                ots, lses = [], []
                for gg in (g, g + 1):
                    lanes = slice(gg * tq, (gg + 1) * tq)
                    rows = slice(0, HEAD_DIM) if shared_kv else slice(gg * HEAD_DIM, (gg + 1) * HEAD_DIM)
                    ots.append(acc[rows, lanes] / l[:, lanes])
                    lses.append(jnp.broadcast_to(lse[:, lanes], (HEAD_DIM, tq)))
                out_rows = slice(j * tq, (j + 1) * tq)
                cols = slice((c * hpc + g) * HEAD_DIM, (c * hpc + g + 2) * HEAD_DIM)
                o_ref[out_rows, cols] = jnp.concatenate(ots, axis=0).T.astype(o_ref.dtype)
                if lse_out:
                    lse_ref[out_rows, cols] = jnp.concatenate(lses, axis=0).T


def _window_attention(pn, pr, bias, sink_rows, tq, nsub):
    bsz, seq, _ = pn.shape
    pad = A_WIN
    w = tq + 2 * pad
    lanes = BAND_HPC * tq
    kern = functools.partial(_band_kernel, qi_axis=1, nres=None, has_sink=True, lse_out=False,
                             shared_kv=True, tq=tq, nsub=nsub, w=w, pad=pad, m_len=seq)
    return pl.pallas_call(
        kern,
        grid=(bsz, seq // (nsub * tq)),
        in_specs=[
            pl.BlockSpec((None, nsub * tq, 512), lambda b, qi: (b, qi, 0)),
            pl.BlockSpec((None, seq, 128), lambda b, qi: (b, 0, 6)),
            pl.BlockSpec((None, seq, 128), lambda b, qi: (b, 0, 7)),
            pl.BlockSpec((3, A_KV, w, lanes), lambda b, qi: (0, 0, 0, 0)),
            pl.BlockSpec((A_KV, 1, lanes), lambda b, qi: (0, 0, 0)),
        ],
        out_specs=pl.BlockSpec((None, nsub * tq, BRANCH_W), lambda b, qi: (b, qi, 0)),
        out_shape=jax.ShapeDtypeStruct((bsz, seq, BRANCH_W), BF16),
        compiler_params=pltpu.CompilerParams(
            dimension_semantics=("parallel", "arbitrary"), vmem_limit_bytes=VMEM_LIMIT),
    )(pn, pr, pr, bias, sink_rows)


def _dilated_group(qk, v, bias, tq, nsub):
    bsz, r, m_len, _ = v.shape
    pad = 64
    w = min(tq + 2 * pad, m_len)
    nsub = min(nsub, m_len // tq)
    nchain = D_HEADS_PER_GROUP // BAND_HPC
    nres = max(1, min(r, BAND_NSUB * tq // m_len))
    kern = functools.partial(_band_kernel, qi_axis=2, nres=nres, has_sink=False, lse_out=True,
                             shared_kv=False, tq=tq, nsub=nsub, w=w, pad=pad, m_len=m_len)
    row_spec = pl.BlockSpec((None, nres, nsub * tq, BRANCH_W), lambda b, c, qi: (b, c, qi, 0))
    return pl.pallas_call(
        kern,
        grid=(bsz, r // nres, m_len // (nsub * tq)),
        in_specs=[
            row_spec,
            pl.BlockSpec((None, nres, m_len, 512), lambda b, c, qi: (b, c, 0, 1)),
            pl.BlockSpec((None, nres, m_len, 512), lambda b, c, qi: (b, c, 0, 0)),
            pl.BlockSpec((3, nchain, w, BAND_HPC * tq), lambda b, c, qi: (0, 0, 0, 0)),
        ],
        out_specs=[row_spec, row_spec],
        out_shape=[jax.ShapeDtypeStruct((bsz, r, m_len, BRANCH_W), BF16),
                   jax.ShapeDtypeStruct((bsz, r, m_len, BRANCH_W), F32)],
        compiler_params=pltpu.CompilerParams(
            dimension_semantics=("parallel", "parallel", "arbitrary"), vmem_limit_bytes=VMEM_LIMIT),
    )(qk, qk, v, bias)


DIFF_UNIT = 512


def _saturation_distance():
    nb = REL_BUCKETS // 2
    exact = nb // 2
    n = np.arange(exact, 4 * REL_MAX_DIST)
    large = exact + (np.log(n.astype(np.float32) / exact) / math.log(REL_MAX_DIST / exact)
                     * (nb - exact)).astype(np.int32)
    return int(n[large < nb - 1].max()) + 1 + 2


def _ceil_div(a, b):
    return -(-a // b)


def _diff_tiles(tq, tk):
    far = _saturation_distance()
    return -_ceil_div(far + tk - 1, DIFF_UNIT), _ceil_div(far + tq - 1, DIFF_UNIT)


def _diff_kernel(lv_ref, sg_ref, q_ref, k_ref, v_ref, bias_ref, o_ref, st_ref, *, tq, tk, nk, lo, hi, lam_init):
    qi = pl.program_id(2)
    vd = 2 * HEAD_DIM
    qt = q_ref[...].T
    first = lax.broadcasted_iota(jnp.int32, (vd, tq), 0) < HEAD_DIM
    zero = jnp.zeros((vd, tq), BF16)
    qbd = jnp.concatenate([jnp.where(first, qt, zero), jnp.where(first, zero, qt)], axis=1)
    m = jnp.full((1, 2 * tq), NEG, F32)
    acc = jnp.zeros((vd + ONES_ROWS, 2 * tq), F32)
    ones = jnp.ones((ONES_ROWS, tk), BF16)

    def scores(t):
        dd = jnp.clip(t * (tk // DIFF_UNIT) - qi * (tq // DIFF_UNIT), lo, hi) - lo
        b = bias_ref[dd]
        st = jnp.dot(k_ref[t * tk:(t + 1) * tk, :], qbd, preferred_element_type=F32)
        return jnp.concatenate([st[:, :tq] + b, st[:, tq:] + b], axis=1)

    st_ref[0] = scores(0)
    for t in range(nk):
        if t + 1 < nk:
            st_ref[(t + 1) % 2] = scores(t + 1)
        st = st_ref[t % 2]
        vt = jnp.concatenate([v_ref[t * tk:(t + 1) * tk, :].T, ones], axis=0)
        m_new = jnp.maximum(m, jnp.max(st, axis=0, keepdims=True))
        pt = jnp.exp2(st - m_new).astype(BF16)
        acc = jnp.exp2(m - m_new) * acc + jnp.dot(vt, pt, preferred_element_type=F32)
        m = m_new

    lv = lv_ref[...]
    lam = (jnp.exp(jnp.sum(lv[0:1] * lv[1:2], axis=-1, keepdims=True))
           - jnp.exp(jnp.sum(lv[2:3] * lv[3:4], axis=-1, keepdims=True)) + lam_init)
    ot = acc[:vd, :tq] / acc[vd:vd + 1, :tq] - lam * (acc[:vd, tq:] / acc[vd:vd + 1, tq:])
    ms = jnp.mean(ot * ot, axis=0, keepdims=True)
    ot = ot * lax.rsqrt(ms + EPS) * sg_ref[...] * (1.0 - lam_init)
    o_ref[...] = ot.T.astype(o_ref.dtype)


def _diff_bias(tbl, tq, tk):
    lo, hi = _diff_tiles(tq, tk)
    r0 = max(-lo, hi) * DIFF_UNIT + tk + tq
    vec = _lookup(tbl, _bucket(jnp.arange(-r0, r0 + 1))) * LOG2E
    starts = [r0 + d * DIFF_UNIT for d in range(lo, hi + 1)]
    return _toeplitz_tiles(vec, starts, tk, tq, (tbl.shape[1], len(starts), tk, tq), lambda h, t: (h, t, 0, 0))


def _diff_attention(pn, pbv, bias, lam_vec, sub_gain, lam_init, tq, tk):
    bsz, seq, _ = pn.shape
    lo, hi = _diff_tiles(tq, tk)
    nt = hi - lo + 1
    vd = 2 * HEAD_DIM
    kern = functools.partial(_diff_kernel, tq=tq, tk=tk, nk=seq // tk, lo=lo, hi=hi, lam_init=lam_init)
    return pl.pallas_call(
        kern,
        grid=(B_HEADS, bsz, seq // tq),
        in_specs=[
            pl.BlockSpec((4, HEAD_DIM), lambda h, b, qi: (0, 0)),
            pl.BlockSpec((vd, 1), lambda h, b, qi: (0, 0)),
            pl.BlockSpec((None, tq, vd), lambda h, b, qi: (b, qi, 512 // vd + h)),
            pl.BlockSpec((None, seq, vd), lambda h, b, qi: (b, 0, 1024 // vd + h)),
            pl.BlockSpec((None, seq, vd), lambda h, b, qi: (b, 0, h)),
            pl.BlockSpec((None, nt, tk, tq), lambda h, b, qi: (h, 0, 0, 0), pipeline_mode=pl.Buffered(1)),
        ],
        out_specs=pl.BlockSpec((None, tq, vd), lambda h, b, qi: (b, qi, h)),
        out_shape=jax.ShapeDtypeStruct((bsz, seq, BRANCH_W), BF16),
        scratch_shapes=[pltpu.VMEM((2, tk, 2 * tq), F32)],
        compiler_params=pltpu.CompilerParams(
            dimension_semantics=("parallel", "parallel", "arbitrary"), vmem_limit_bytes=VMEM_LIMIT),
    )(lam_vec, sub_gain.reshape(vd, 1), pn, pn, pbv, bias)


def _gqa_kernel(q0_ref, q1_ref, k_ref, v_ref, o_ref, st_ref, *, tq, tk, nk):
    grp = C_HEADS // C_KV
    qts = []
    for q_ref in (q0_ref, q1_ref):
        qt = q_ref[...].T
        qts.append(jnp.concatenate([qt[g * HEAD_DIM:(g + 1) * HEAD_DIM] for g in range(grp)], axis=1))
    ones = jnp.ones((ONES_ROWS, tk), BF16)
    state = [(jnp.full((1, grp * tq), NEG, F32), jnp.zeros((HEAD_DIM + ONES_ROWS, grp * tq), F32))
             for _ in range(C_KV)]

    def scores(t, slot):
        for kv in range(C_KV):
            k = k_ref[t * tk:(t + 1) * tk, kv * HEAD_DIM:(kv + 1) * HEAD_DIM]
            st_ref[slot, kv] = jnp.dot(k, qts[kv], preferred_element_type=F32)

    scores(0, 0)
    for t in range(nk):
        if t + 1 < nk:
            scores(t + 1, (t + 1) % 2)
        for kv in range(C_KV):
            m, acc = state[kv]
            st = st_ref[t % 2, kv]
            v = v_ref[t * tk:(t + 1) * tk, kv * HEAD_DIM:(kv + 1) * HEAD_DIM]
            vt = jnp.concatenate([v.T, ones], axis=0)
            m_new = jnp.maximum(m, jnp.max(st, axis=0, keepdims=True))
            pt = jnp.exp2(st - m_new).astype(BF16)
            acc = jnp.exp2(m - m_new) * acc + jnp.dot(vt, pt, preferred_element_type=F32)
            state[kv] = (m_new, acc)

    for kv in range(C_KV):
        acc = state[kv][1]
        ot = acc[:HEAD_DIM] / acc[HEAD_DIM:HEAD_DIM + 1]
        for g in range(grp):
            h = kv * grp + g
            o_ref[:, h * HEAD_DIM:(h + 1) * HEAD_DIM] = ot[:, g * tq:(g + 1) * tq].T.astype(o_ref.dtype)


def _dense_gqa(pr, tq=512, tk=512):
    bsz, seq, _ = pr.shape
    kern = functools.partial(_gqa_kernel, tq=tq, tk=tk, nk=seq // tk)
    return pl.pallas_call(
        kern,
        grid=(bsz, seq // tq),
        in_specs=[
            pl.BlockSpec((None, tq, 256), lambda b, qi: (b, qi, 0)),
            pl.BlockSpec((None, tq, 256), lambda b, qi: (b, qi, 1)),
            pl.BlockSpec((None, seq, 128), lambda b, qi: (b, 0, 4)),
            pl.BlockSpec((None, seq, 128), lambda b, qi: (b, 0, 5)),
        ],
        out_specs=pl.BlockSpec((None, tq, BRANCH_W), lambda b, qi: (b, qi, 0)),
        out_shape=jax.ShapeDtypeStruct((bsz, seq, BRANCH_W), BF16),
        scratch_shapes=[pltpu.VMEM((2, C_KV, tk, (C_HEADS // C_KV) * tq), F32)],
        compiler_params=pltpu.CompilerParams(
            dimension_semantics=("parallel", "arbitrary"), vmem_limit_bytes=VMEM_LIMIT),
    )(pr, pr, pr, pr)


def _merge_kernel(*refs, tm):
    x_ref, oa_ref, ob_ref, oc_ref = refs[:4]
    d_refs = refs[4:10]
    gate_ref, merge_ref, wb_ref, wo_ref, out_ref, so_ref, sl_ref = refs[10:17]
    nslab = BRANCH_W // LANES

    def natural(ref, scr, r):
        if r == 1:
            return ref[0].astype(F32)
        for c in range(r):
            blk = ref[c].astype(F32)
            for s in range(nslab):
                scr[s, pl.ds(c, tm // r, stride=r), :] = blk[:, s * LANES:(s + 1) * LANES]
        return jnp.concatenate([scr[s] for s in range(nslab)], axis=1)

    def gate(n):
        return gate_ref[:, n * BRANCH_W:(n + 1) * BRANCH_W]

    def term(n, gated, half):
        y = jnp.dot(gated, wb_ref[n, :, half * 512:(half + 1) * 512], preferred_element_type=F32)
        return merge_ref[:, n * D_MODEL + half * 512:n * D_MODEL + (half + 1) * 512].astype(F32) * y

    partial = [None, None]
    for n, o_ref in enumerate((oa_ref, ob_ref, oc_ref)):
        gated = o_ref[...] * gate(n)
        for half in range(2):
            t = term(n, gated, half)
            partial[half] = t if partial[half] is None else partial[half] + t

    os_, ls_ = [], []
    for g, (_, r) in enumerate(D_PAIRS):
        os_.append(natural(d_refs[2 * g], so_ref, r))
        ls_.append(natural(d_refs[2 * g + 1], sl_ref, r))
    mx = jnp.maximum(jnp.maximum(ls_[0], ls_[1]), ls_[2])
    es = [jnp.exp2(l - mx) for l in ls_]
    od = (es[0] * os_[0] + es[1] * os_[1] + es[2] * os_[2]) / (es[0] + es[1] + es[2])
    gated_d = (od * gate(N_BRANCH - 1).astype(F32)).astype(BF16)
    merged = jnp.concatenate([(partial[half] + term(N_BRANCH - 1, gated_d, half)).astype(BF16)
                              for half in range(2)], axis=1)
    out_ref[...] = x_ref[...] + jnp.dot(merged, wo_ref[...], preferred_element_type=F32)


def _merge(x, oa, ob, oc, d_outs, gate, merge, wb, wo, seq, tm=512):
    n = x.shape[0]
    ns = seq // tm
    row = pl.BlockSpec((tm, BRANCH_W), lambda i: (i, 0))
    in_specs = [pl.BlockSpec((tm, D_MODEL), lambda i: (i, 0)), row, row, row]
    for _, r in D_PAIRS:
        in_specs += [pl.BlockSpec((None, r, tm // r, BRANCH_W), lambda i: (i // ns, 0, i % ns, 0))] * 2
    in_specs += [pl.BlockSpec((tm, N_BRANCH * BRANCH_W), lambda i: (i, 0)),
                 pl.BlockSpec((tm, N_BRANCH * D_MODEL), lambda i: (i, 0)),
                 pl.BlockSpec((N_BRANCH, BRANCH_W, D_MODEL), lambda i: (0, 0, 0), pipeline_mode=pl.Buffered(1)),
                 pl.BlockSpec((D_MODEL, D_MODEL), lambda i: (0, 0), pipeline_mode=pl.Buffered(1))]
    return pl.pallas_call(
        functools.partial(_merge_kernel, tm=tm),
        grid=(n // tm,),
        in_specs=in_specs,
        out_specs=pl.BlockSpec((tm, D_MODEL), lambda i: (i, 0)),
        out_shape=jax.ShapeDtypeStruct((n, D_MODEL), F32),
        scratch_shapes=[pltpu.VMEM((BRANCH_W // LANES, tm, LANES), F32)] * 2,
        compiler_params=pltpu.CompilerParams(
            dimension_semantics=("parallel",), vmem_limit_bytes=VMEM_LIMIT),
    )(x, oa, ob, oc, *d_outs, gate, merge, wb, wo)


def kernel(x, w_in, w_branch, w_out, norm_gain, qk_gain, sink, lambda_vec, sub_norm_gain, rel_bias):
    bsz, seq, _ = x.shape
    n = bsz * seq
    depth = w_in.shape[0]
    rb = rel_bias.astype(F32)

    rows = seq // GRID_W
    row = jnp.repeat(jnp.arange(rows), GRID_W).astype(F32)
    col = jnp.tile(jnp.arange(GRID_W), rows).astype(F32)
    nf = HEAD_DIM // 4
    freqs = ROPE_THETA ** (-jnp.arange(nf, dtype=F32) / nf)
    ang = jnp.concatenate([row[:, None] * freqs, col[:, None] * freqs], axis=-1)
    cos2 = jnp.tile(jnp.cos(ang), (1, 4))
    sin2 = jnp.tile(jnp.concatenate([-jnp.sin(ang), jnp.sin(ang)], axis=-1), (1, 2))

    tq_a = 128
    bias_a = _band_bias(rb[:, :A_HEADS], 1, A_WIN, tq_a, tq_a + 2 * A_WIN, A_WIN, BAND_HPC)
    tq_b, tk_b = 1024, 512
    bias_b = _diff_bias(rb[:, A_HEADS:A_HEADS + B_HEADS], tq_b, tk_b)
    tq_d = 128
    bias_d = []
    for g, (win, r) in enumerate(D_PAIRS):
        lo = A_HEADS + B_HEADS + g * D_HEADS_PER_GROUP
        wd = min(tq_d + 128, seq // r)
        bias_d.append(_band_bias(rb[:, lo:lo + D_HEADS_PER_GROUP], r, win // (2 * r), tq_d, wd, 64, BAND_HPC))

    flag_np, rflag_np = _col_flags()
    flag = _permute_cols(jnp.asarray(flag_np))
    rflag = _permute_cols(jnp.asarray(rflag_np))
    blk = np.arange(NORM_W) // HEAD_DIM
    bd = jnp.asarray((blk[:, None] == blk[None, :]).astype(np.float32), dtype=BF16)

    xf = x.reshape(n, D_MODEL)
    for l in range(depth):
        h = _prenorm(xf, norm_gain[l].reshape(1, D_MODEL))
        w = _permute_cols(w_in[l].astype(BF16))
        gain = _permute_cols(_gain_cols(qk_gain[l]))
        norm_args = dict(gain=gain, flag=flag, bd=bd)
        pr = _proj_call(h, w, kind=K_ROPE, col0=P_ROPE, step=1, nj=2, seq=seq, rflag=rflag, cos=cos2, sin=sin2,
                        **norm_args).reshape(bsz, seq, -1)
        pn = _proj_call(h, w, kind=K_NORM, col0=P_NORM, step=1, nj=3, seq=seq, **norm_args).reshape(bsz, seq, -1)
        pbv = _proj_call(h, w, kind=K_RAW, col0=P_BV, step=1, nj=1, seq=seq).reshape(bsz, seq, -1)
        gate = _proj_call(h, w, kind=K_SILU, col0=OFF_GATE, step=1, nj=4, seq=seq)
        merge = _proj_call(h, w, kind=K_SIGMOID, col0=OFF_MERGE, step=1, nj=8, seq=seq)
        sink_rows = jnp.repeat(sink[l].astype(F32) * LOG2E, tq_a).reshape(A_KV, 1, BAND_HPC * tq_a)
        oa = _window_attention(pn, pr, bias_a, sink_rows, tq=tq_a, nsub=BAND_NSUB)
        lam_init = 0.8 - 0.6 * math.exp(-0.3 * l)
        ob = _diff_attention(pn, pbv, bias_b, lambda_vec[l], sub_norm_gain[l], lam_init, tq=tq_b, tk=tk_b)
        oc = _dense_gqa(pr)
        d_outs = []
        for g, (win, r) in enumerate(D_PAIRS):
            dqk = _proj_call(h, w, kind=K_NORM, col0=OFF_DQ + g * 512, step=(OFF_DK - OFF_DQ) // 512, nj=2,
                             seq=seq, r=r, **norm_args)
            dv = _proj_call(h, w, kind=K_RAW, col0=OFF_DV + g * 512, step=1, nj=1, seq=seq, r=r)
            if r == 1:
                dqk, dv = dqk.reshape(bsz, 1, seq, -1), dv.reshape(bsz, 1, seq, -1)
            d_outs += _dilated_group(dqk, dv, bias_d[g], tq=tq_d, nsub=BAND_NSUB)
        xf = _merge(xf, oa.reshape(n, -1), ob.reshape(n, -1), oc.reshape(n, -1), d_outs, gate, merge,
                    w_branch[l].astype(BF16), w_out[l].astype(BF16), seq)
    return xf.reshape(bsz, seq, D_MODEL)
```
